```python
import jax, jax.numpy as jnp
from jax import lax
import numpy as np

D_MODEL = 2048
BATCH = 2
SEQ = 4096
DEPTH = 2

N_A_LAYERS = DEPTH // 2
N_B_LAYERS = DEPTH - N_A_LAYERS
DEEPNORM_ALPHA = (2.0 * DEPTH) ** 0.25
DEEPNORM_BETA = (8.0 * DEPTH) ** -0.25
LN_EPS = 1e-5
CONV_WIDTH = 31
N_HEADS = 16
HEAD_DIM = D_MODEL // N_HEADS
N_KV_GROUPS = 4
HEADS_PER_GROUP = N_HEADS // N_KV_GROUPS
N_KV_SLOTS = 6
CMP_BLOCK = 32
CMP_STRIDE = 16
CMP_HIDDEN = 2 * HEAD_DIM
SEL_BLOCK = 64
SEL_TOPK = 16
SEL_Q_BLOCK = 32
WINDOW = 512
WIN_Q_BLOCK = 128
N_WIN_BLOCKS = WINDOW // WIN_Q_BLOCK + 1
N_BRANCHES = 3
FORCE_SCORE = 1e6
MASK_VALUE = -1e30
N_GROUPS = 4
EXPERTS_PER_GROUP = 8
N_EXPERTS = N_GROUPS * EXPERTS_PER_GROUP
EXPERT_TOPK = 2
EXPERT_DFF = 512
MOE_CHUNK = 128

kernel_name = 'hybrid_conformer_nsa_yoco_hmoe'


def layer_norm(x, g, b):
    x32 = x.astype(jnp.float32)
    mu = jnp.mean(x32, -1, keepdims=True)
    var = jnp.mean(jnp.square(x32 - mu), -1, keepdims=True)
    y = (x32 - mu) * lax.rsqrt(var + LN_EPS) * g.astype(jnp.float32) + b.astype(jnp.float32)
    return y.astype(x.dtype)


def alibi_slopes():
    h = jnp.arange(1, N_HEADS + 1, dtype=jnp.float32)
    return (2.0 ** (-8.0 * h / N_HEADS)).reshape(N_KV_GROUPS, HEADS_PER_GROUP)


def conformer_conv(x, w_pw1, b_pw1, w_dw, b_dw, g, b, w_pw2):
    h = x @ w_pw1 + b_pw1
    a, gate = jnp.split(h, 2, axis=-1)
    h = a * jax.nn.sigmoid(gate)
    h = lax.conv_general_dilated(h, w_dw[:, None, :], window_strides=(1,),
                                 padding=((CONV_WIDTH - 1, 0),),
                                 dimension_numbers=('NWC', 'WIO', 'NWC'),
                                 feature_group_count=D_MODEL) + b_dw
    h = jax.nn.silu(layer_norm(h, g, b))
    return h @ w_pw2


def shared_kv(h, kv_w, cmp_pos, cmp_w1, cmp_w2):
    B, S, _ = h.shape
    kv = (h @ kv_w).reshape(B, S, N_KV_SLOTS, N_KV_GROUPS, HEAD_DIM)
    n_cmp = (S - CMP_BLOCK) // CMP_STRIDE + 1
    idx = jnp.arange(n_cmp)[:, None] * CMP_STRIDE + jnp.arange(CMP_BLOCK)[None, :]
    tok = jnp.stack([kv[:, :, 0], kv[:, :, 1]], 0)
    blocks = tok[:, :, idx] + cmp_pos[:, None, None, :, None, :]
    flat = blocks.transpose(0, 1, 2, 4, 3, 5).reshape(2, B, n_cmp, N_KV_GROUPS, CMP_BLOCK * HEAD_DIM)
    hid = jax.nn.gelu(jnp.einsum('cbngf,cfh->cbngh', flat, cmp_w1))
    comp = jnp.einsum('cbngh,chd->cbngd', hid, cmp_w2)
    return (comp[0], comp[1], kv[:, :, 2], kv[:, :, 3], kv[:, :, 4], kv[:, :, 5])


def compressed_branch(q, k_cmp, v_cmp, slopes):
    S = q.shape[1]
    n_cmp = k_cmp.shape[1]
    pos = jnp.arange(S)
    dist = pos[:, None] - (jnp.arange(n_cmp) * CMP_STRIDE + CMP_BLOCK - 1)[None, :]
    valid = dist >= 0
    s = jnp.einsum('bsghd,bngd->bghsn', q, k_cmp, preferred_element_type=jnp.float32)
    s = s - slopes[None, :, :, None, None] * dist.astype(jnp.float32)
    p = jax.nn.softmax(jnp.where(valid, s, MASK_VALUE), axis=-1)
    p = jnp.where(valid, p, 0.0)
    o = jnp.einsum('bghsn,bngd->bsghd', p.astype(v_cmp.dtype), v_cmp)
    return o, p


def select_blocks(p_cmp, S):
    n_cmp = p_cmp.shape[-1]
    n_sel = S // SEL_BLOCK
    c0 = jnp.arange(n_cmp)[:, None] * CMP_STRIDE
    j0 = jnp.arange(n_sel)[None, :] * SEL_BLOCK
    overlap = jnp.maximum(jnp.minimum(c0 + CMP_BLOCK, j0 + SEL_BLOCK) - jnp.maximum(c0, j0), 0)
    overlap = overlap.astype(jnp.float32) / CMP_BLOCK
    imp = jnp.einsum('bghsn,nj->bgsj', p_cmp, overlap)
    cur = (jnp.arange(S) // SEL_BLOCK)[:, None]
    blk = jnp.arange(n_sel)[None, :]
    forced = (blk == 0) | (blk == cur) | (blk == cur - 1)
    imp = jnp.where(blk > cur, -jnp.inf, imp + jnp.where(forced, FORCE_SCORE, 0.0))
    _, idx = lax.top_k(imp, min(SEL_TOPK, n_sel))
    return idx


def selected_branch(q, k_sel, v_sel, sel_idx, slopes):
    B, S = q.shape[:2]
    n_sel = S // SEL_BLOCK
    topk = sel_idx.shape[-1]
    n_qc = S // SEL_Q_BLOCK
    kb = k_sel.reshape(B, n_sel, SEL_BLOCK, N_KV_GROUPS, HEAD_DIM).transpose(0, 3, 1, 2, 4)
    vb = v_sel.reshape(B, n_sel, SEL_BLOCK, N_KV_GROUPS, HEAD_DIM).transpose(0, 3, 1, 2, 4)
    q_c = q.reshape(B, n_qc, SEL_Q_BLOCK, N_KV_GROUPS, HEADS_PER_GROUP, HEAD_DIM).transpose(1, 0, 3, 4, 2, 5)
    i_c = sel_idx.reshape(B, N_KV_GROUPS, n_qc, SEL_Q_BLOCK, topk).transpose(2, 0, 1, 3, 4)
    p_c = jnp.arange(S).reshape(n_qc, SEL_Q_BLOCK)
    bi = jnp.arange(B)[:, None, None, None]
    gi = jnp.arange(N_KV_GROUPS)[None, :, None, None]
    offs = jnp.arange(SEL_BLOCK)
    n_keys = topk * SEL_BLOCK

    def one_chunk(args):
        qc, ic, pc = args
        kg = kb[bi, gi, ic].reshape(B, N_KV_GROUPS, SEL_Q_BLOCK, n_keys, HEAD_DIM)
        vg = vb[bi, gi, ic].reshape(B, N_KV_GROUPS, SEL_Q_BLOCK, n_keys, HEAD_DIM)
        kpos = (ic[..., None] * SEL_BLOCK + offs).reshape(B, N_KV_GROUPS, SEL_Q_BLOCK, n_keys)
        dist = (pc[None, None, :, None] - kpos)[:, :, None]
        s = jnp.einsum('bghqd,bgqkd->bghqk', qc, kg, preferred_element_type=jnp.float32)
        s = s - slopes[None, :, :, None, None] * dist.astype(jnp.float32)
        p = jax.nn.softmax(jnp.where(dist >= 0, s, MASK_VALUE), axis=-1)
        return jnp.einsum('bghqk,bgqkd->bghqd', p.astype(vg.dtype), vg)

    o = lax.map(one_chunk, (q_c, i_c, p_c))
    return o.transpose(1, 0, 4, 2, 3, 5).reshape(B, S, N_KV_GROUPS, HEADS_PER_GROUP, HEAD_DIM)


def window_branch(q, k_win, v_win, slopes):
    B, S = q.shape[:2]
    nqb = S // WIN_Q_BLOCK

    def band(t):
        tb = t.reshape(B, nqb, WIN_Q_BLOCK, N_KV_GROUPS, HEAD_DIM)
        tp = jnp.pad(tb, ((0, 0), (N_WIN_BLOCKS - 1, 0), (0, 0), (0, 0), (0, 0)))
        return jnp.concatenate([tp[:, j:j + nqb] for j in range(N_WIN_BLOCKS)], axis=2)

    kband = band(k_win)
    vband = band(v_win)
    qpos = jnp.arange(S).reshape(nqb, WIN_Q_BLOCK)
    kpos = (jnp.arange(nqb)[:, None] - (N_WIN_BLOCKS - 1)) * WIN_Q_BLOCK + jnp.arange(N_WIN_BLOCKS * WIN_Q_BLOCK)[None, :]
    dist = qpos[:, :, None] - kpos[:, None, :]
    valid = (dist >= 0) & (dist < WINDOW) & (kpos[:, None, :] >= 0)
    qw = q.reshape(B, nqb, WIN_Q_BLOCK, N_KV_GROUPS, HEADS_PER_GROUP, HEAD_DIM)
    s = jnp.einsum('bnqghd,bnkgd->bnghqk', qw, kband, preferred_element_type=jnp.float32)
    s = s - slopes[None, None, :, :, None, None] * dist[None, :, None, None].astype(jnp.float32)
    p = jax.nn.softmax(jnp.where(valid[None, :, None, None], s, MASK_VALUE), axis=-1)
    o = jnp.einsum('bnghqk,bnkgd->bnqghd', p.astype(vband.dtype), vband)
    return o.reshape(B, S, N_KV_GROUPS, HEADS_PER_GROUP, HEAD_DIM)


def nsa_layer(x, w_qg, w_o, k_cmp, v_cmp, k_sel, v_sel, k_win, v_win):
    B, S, _ = x.shape
    hd = N_HEADS * HEAD_DIM
    slopes = alibi_slopes()
    qg = x @ w_qg
    q = (qg[..., :hd] * HEAD_DIM ** -0.5).reshape(B, S, N_KV_GROUPS, HEADS_PER_GROUP, HEAD_DIM)
    gates = jax.nn.sigmoid(qg[..., hd:].astype(jnp.float32)).reshape(B, S, N_KV_GROUPS, HEADS_PER_GROUP, N_BRANCHES)
    o_cmp, p_cmp = compressed_branch(q, k_cmp, v_cmp, slopes)
    sel_idx = select_blocks(p_cmp, S)
    o_sel = selected_branch(q, k_sel, v_sel, sel_idx, slopes)
    o_win = window_branch(q, k_win, v_win, slopes)
    o = gates[..., 0:1] * o_cmp + gates[..., 1:2] * o_sel + gates[..., 2:3] * o_win
    return o.astype(x.dtype).reshape(B, S, hd) @ w_o


def hierarchical_moe(x, wg, bg, we, be, w_gu, w_down):
    B, S, D = x.shape
    T = B * S
    xt = x.reshape(T, D)
    lg = (xt @ wg).astype(jnp.float32) + bg
    pg = jax.nn.softmax(lg, axis=-1)
    grp = jnp.argmax(lg, axis=-1)
    gw = jnp.take_along_axis(pg, grp[:, None], axis=-1)[:, 0]
    le = ((xt @ we).astype(jnp.float32) + be).reshape(T, N_GROUPS, EXPERTS_PER_GROUP)
    le = jnp.take_along_axis(le, grp[:, None, None], axis=1)[:, 0]
    top_v, top_i = lax.top_k(le, EXPERT_TOPK)
    pw = jax.nn.softmax(top_v, axis=-1) * gw[:, None]
    eid = (grp[:, None] * EXPERTS_PER_GROUP + top_i).reshape(-1)
    tok = jnp.repeat(jnp.arange(T), EXPERT_TOPK)
    wts = pw.reshape(-1)
    A = T * EXPERT_TOPK
    order = jnp.argsort(eid)
    e_s, tok_s, w_s = eid[order], tok[order], wts[order]
    counts = jax.ops.segment_sum(jnp.ones((A,), jnp.int32), eid, num_segments=N_EXPERTS)
    padded = (counts + MOE_CHUNK - 1) // MOE_CHUNK * MOE_CHUNK
    pad_end = jnp.cumsum(padded)
    pad_start = pad_end - padded
    start = jnp.cumsum(counts) - counts
    dest = pad_start[e_s] + jnp.arange(A) - start[e_s]
    n_chunks = (A + N_EXPERTS * (MOE_CHUNK - 1) + MOE_CHUNK - 1) // MOE_CHUNK
    buf = jnp.zeros((n_chunks * MOE_CHUNK, D), x.dtype).at[dest].set(xt[tok_s])
    chunk_e = jnp.clip(jnp.searchsorted(pad_end, jnp.arange(n_chunks) * MOE_CHUNK, side='right'), 0, N_EXPERTS - 1)

    def expert_chunk(args):
        xc, e = args
        g, u = jnp.split(xc @ w_gu[e], 2, axis=-1)
        return (jax.nn.silu(g) * u) @ w_down[e]

    out = lax.map(expert_chunk, (buf.reshape(n_chunks, MOE_CHUNK, D), chunk_e)).reshape(-1, D)
    y = jnp.zeros((T, D), x.dtype).at[tok_s].add(out[dest] * w_s[:, None].astype(x.dtype))
    return y.reshape(B, S, D)


def setup_inputs(seed: int = 0) -> dict:
    key = jax.random.key(seed)
    ks = jax.random.split(key, 24)
    f32 = jnp.float32
    D, G, dh, F, E = D_MODEL, N_KV_GROUPS, HEAD_DIM, EXPERT_DFF, N_EXPERTS
    nrm = lambda k, shape, scale: jax.random.normal(k, shape, f32) * scale
    branch_scale = jnp.array([1.0, DEEPNORM_BETA, 1.0, DEEPNORM_BETA, 1.0, DEEPNORM_BETA], f32)
    kv_w = (nrm(ks[8], (D, N_KV_SLOTS, G * dh), D ** -0.5) * branch_scale[None, :, None]).reshape(D, N_KV_SLOTS * G * dh)
    return {
        'x': nrm(ks[0], (BATCH, SEQ, D), 1.0),
        'conv_w_pw1': nrm(ks[1], (N_A_LAYERS, D, 2 * D), D ** -0.5),
        'conv_b_pw1': nrm(ks[2], (N_A_LAYERS, 2 * D), 0.02),
        'conv_w_dw': nrm(ks[3], (N_A_LAYERS, CONV_WIDTH, D), CONV_WIDTH ** -0.5),
        'conv_b_dw': nrm(ks[4], (N_A_LAYERS, D), 0.02),
        'conv_ln_g': 1.0 + nrm(ks[5], (N_A_LAYERS, D), 0.02),
        'conv_ln_b': nrm(ks[6], (N_A_LAYERS, D), 0.02),
        'conv_w_pw2': nrm(ks[7], (N_A_LAYERS, D, D), D ** -0.5 * DEEPNORM_BETA),
        'kv_w': kv_w,
        'cmp_pos': nrm(ks[9], (2, CMP_BLOCK, dh), 0.02),
        'cmp_w1': nrm(ks[10], (2, CMP_BLOCK * dh, CMP_HIDDEN), (CMP_BLOCK * dh) ** -0.5),
        'cmp_w2': nrm(ks[11], (2, CMP_HIDDEN, dh), CMP_HIDDEN ** -0.5),
        'nsa_w_qg': nrm(ks[12], (N_B_LAYERS, D, N_HEADS * dh + N_BRANCHES * N_HEADS), D ** -0.5),
        'nsa_w_o': nrm(ks[13], (N_B_LAYERS, N_HEADS * dh, D), (N_HEADS * dh) ** -0.5 * DEEPNORM_BETA),
        'moe_wg': nrm(ks[14], (DEPTH, D, N_GROUPS), D ** -0.5),
        'moe_bg': nrm(ks[15], (DEPTH, N_GROUPS), 0.01),
        'moe_we': nrm(ks[16], (DEPTH, D, E), D ** -0.5),
        'moe_be': nrm(ks[17], (DEPTH, E), 0.01),
        'moe_w_gu': nrm(ks[18], (DEPTH, E, D, 2 * F), D ** -0.5),
        'moe_w_down': nrm(ks[19], (DEPTH, E, F, D), F ** -0.5 * DEEPNORM_BETA),
        'ln_g': 1.0 + nrm(ks[20], (DEPTH, 2, D), 0.02),
        'ln_b': nrm(ks[21], (DEPTH, 2, D), 0.02),
    }


def reference(x, conv_w_pw1, conv_b_pw1, conv_w_dw, conv_b_dw, conv_ln_g, conv_ln_b, conv_w_pw2,
              kv_w, cmp_pos, cmp_w1, cmp_w2, nsa_w_qg, nsa_w_o,
              moe_wg, moe_bg, moe_we, moe_be, moe_w_gu, moe_w_down, ln_g, ln_b):
    h = x
    kvs = None
    for l in range(DEPTH):
        if l < N_A_LAYERS:
            mix = conformer_conv(h, conv_w_pw1[l], conv_b_pw1[l], conv_w_dw[l], conv_b_dw[l],
                                 conv_ln_g[l], conv_ln_b[l], conv_w_pw2[l])
        else:
            if l == N_A_LAYERS:
                kvs = shared_kv(h, kv_w, cmp_pos, cmp_w1, cmp_w2)
            j = l - N_A_LAYERS
            mix = nsa_layer(h, nsa_w_qg[j], nsa_w_o[j], *kvs)
        h = layer_norm(DEEPNORM_ALPHA * h + mix, ln_g[l, 0], ln_b[l, 0])
        ffn = hierarchical_moe(h, moe_wg[l], moe_bg[l], moe_we[l], moe_be[l], moe_w_gu[l], moe_w_down[l])
        h = layer_norm(DEEPNORM_ALPHA * h + ffn, ln_g[l, 1], ln_b[l, 1])
    return h
```

```python
import functools

import numpy as np
import jax
import jax.numpy as jnp
from jax import lax
from jax.experimental import pallas as pl
from jax.experimental.pallas import tpu as pltpu

F32 = jnp.float32
BF16 = jnp.bfloat16
I32 = jnp.int32

DEPTH = 2
N_A_LAYERS = DEPTH // 2
DEEPNORM_ALPHA = (2.0 * DEPTH) ** 0.25
LN_EPS = 1e-5
CONV_WIDTH = 31
N_HEADS = 16
HEAD_DIM = 128
N_KV_GROUPS = 4
HEADS_PER_GROUP = N_HEADS // N_KV_GROUPS
CMP_BLOCK = 32
CMP_STRIDE = 16
SEL_BLOCK = 64
SEL_TOPK = 16
WINDOW = 512
N_BRANCHES = 3
MASK_VALUE = -1e30
N_GROUPS = 4
EXPERTS_PER_GROUP = 8
N_EXPERTS = N_GROUPS * EXPERTS_PER_GROUP
MOE_CHUNK = 128

LANES = 128
VMEM_LIMIT = 56 * 1024 * 1024

CONV_HALO = 32


def _cparams(sem, vmem=None):
    return pltpu.CompilerParams(dimension_semantics=sem, vmem_limit_bytes=vmem)


def _sigmoid(x):
    return 1.0 / (1.0 + jnp.exp(-x))


def _layer_norm(x, g, b):
    mu = jnp.mean(x, axis=-1, keepdims=True)
    xc = x - mu
    var = jnp.mean(xc * xc, axis=-1, keepdims=True)
    return xc * lax.rsqrt(var + LN_EPS) * g + b


def _dot(a, b):
    return jnp.dot(a, b, preferred_element_type=F32)


def _dot_nt(a, b):
    return lax.dot_general(a, b, (((1,), (1,)), ((), ())), preferred_element_type=F32)


def _pw1_glu_kernel(x_ref, wa_ref, wg_ref, ba_ref, bg_ref, o_ref):
    x = x_ref[...]
    a = _dot(x, wa_ref[...]) + ba_ref[...]
    g = _dot(x, wg_ref[...]) + bg_ref[...]
    o_ref[...] = a * _sigmoid(g)


def _pw1_glu(xb, w, b, tm=512, tn=512):
    T, D = xb.shape
    nj = D // tn
    return pl.pallas_call(
        _pw1_glu_kernel,
        grid=(nj, T // tm),
        in_specs=[
            pl.BlockSpec((tm, D), lambda j, i: (i, 0)),
            pl.BlockSpec((D, tn), lambda j, i: (0, j)),
            pl.BlockSpec((D, tn), lambda j, i: (0, j + nj)),
            pl.BlockSpec((1, tn), lambda j, i: (0, j)),
            pl.BlockSpec((1, tn), lambda j, i: (0, j + nj)),
        ],
        out_specs=pl.BlockSpec((tm, tn), lambda j, i: (i, j)),
        out_shape=jax.ShapeDtypeStruct((T, D), F32),
        compiler_params=_cparams(("parallel", "parallel"), VMEM_LIMIT),
        name="pw1_glu",
    )(xb, w, w, b, b)


def _dwconv_kernel(x_ref, halo_ref, w_ref, b_ref, o_ref, xbuf, *, ts, tc, rc):
    i = pl.program_id(1)
    xbuf[0:CONV_HALO, :] = jnp.where(i > 0, halo_ref[0], 0.0)
    xbuf[CONV_HALO:CONV_HALO + ts, :] = x_ref[0]
    off = CONV_HALO - (CONV_WIDTH - 1)
    for c0 in range(0, tc, LANES):
        for r0 in range(0, ts, rc):
            acc = jnp.broadcast_to(b_ref[:, c0:c0 + LANES], (rc, LANES))
            for k in range(CONV_WIDTH):
                acc = acc + w_ref[k:k + 1, c0:c0 + LANES] * xbuf[r0 + k + off:r0 + k + off + rc, c0:c0 + LANES]
            o_ref[0, r0:r0 + rc, c0:c0 + LANES] = acc


def _dwconv(x, w, b, ts=256, tc=256, rc=128):
    B, S, D = x.shape
    hb = ts // CONV_HALO
    return pl.pallas_call(
        functools.partial(_dwconv_kernel, ts=ts, tc=tc, rc=rc),
        grid=(B, S // ts, D // tc),
        in_specs=[
            pl.BlockSpec((1, ts, tc), lambda bb, i, c: (bb, i, c)),
            pl.BlockSpec((1, CONV_HALO, tc), lambda bb, i, c: (bb, jnp.maximum(i * hb - 1, 0), c)),
            pl.BlockSpec((CONV_WIDTH, tc), lambda bb, i, c: (0, c)),
            pl.BlockSpec((1, tc), lambda bb, i, c: (0, c)),
        ],
        out_specs=pl.BlockSpec((1, ts, tc), lambda bb, i, c: (bb, i, c)),
        out_shape=jax.ShapeDtypeStruct((B, S, D), F32),
        scratch_shapes=[pltpu.VMEM((CONV_HALO + ts, tc), F32)],
        compiler_params=_cparams(("parallel", "parallel", "parallel")),
        name="dwconv",
    )(x, x, w, b)


def _pw2_kernel(c_ref, x_ref, w_ref, cg_ref, cb_ref, g_ref, b_ref, of_ref, ob_ref):
    y = _layer_norm(c_ref[...], cg_ref[...], cb_ref[...])
    y = y * _sigmoid(y)
    mix = _dot(y.astype(BF16), w_ref[...])
    h = _layer_norm(DEEPNORM_ALPHA * x_ref[...] + mix, g_ref[...], b_ref[...])
    of_ref[...] = h
    ob_ref[...] = h.astype(BF16)


def _pw2_res_ln(conv, x, w, cg, cb, g, b, tm=256):
    T, D = x.shape
    row = lambda i: (i, 0)
    fix = lambda i: (0, 0)
    return pl.pallas_call(
        _pw2_kernel,
        grid=(T // tm,),
        in_specs=[
            pl.BlockSpec((tm, D), row), pl.BlockSpec((tm, D), row),
            pl.BlockSpec((D, D), fix),
            pl.BlockSpec((1, D), fix), pl.BlockSpec((1, D), fix),
            pl.BlockSpec((1, D), fix), pl.BlockSpec((1, D), fix),
        ],
        out_specs=[pl.BlockSpec((tm, D), row), pl.BlockSpec((tm, D), row)],
        out_shape=[jax.ShapeDtypeStruct((T, D), F32), jax.ShapeDtypeStruct((T, D), BF16)],
        compiler_params=_cparams(("parallel",), VMEM_LIMIT),
        name="pw2_res_ln",
    )(conv, x, w, cg, cb, g, b)


def _router_kernel(h_ref, w_ref, b_ref, ri_ref, rw_ref, cnt_ref, *, tm):
    s = pl.program_id(0)

    @pl.when(s == 0)
    def _():
        cnt_ref[...] = jnp.zeros_like(cnt_ref)

    h = h_ref[...]
    w = w_ref[...]
    h_hi = h.astype(BF16)
    h_lo = (h - h_hi.astype(F32)).astype(BF16)
    w_hi = w.astype(BF16)
    w_lo = (w - w_hi.astype(F32)).astype(BF16)
    lg = _dot(h_hi, w_hi) + _dot(h_lo, w_hi) + _dot(h_hi, w_lo) + b_ref[...]

    lane = lax.broadcasted_iota(I32, (tm, LANES), 1)
    lanef = lane.astype(F32)
    neg = jnp.float32(-jnp.inf)
    big = jnp.float32(1e9)

    is_g = lane < N_GROUPS
    lgm = jnp.where(is_g, lg, neg)
    mg = jnp.max(lgm, axis=-1, keepdims=True)
    grp = jnp.min(jnp.where(lgm == mg, lanef, big), axis=-1, keepdims=True)
    sg = jnp.sum(jnp.where(is_g, jnp.exp(lg - mg), 0.0), axis=-1, keepdims=True)
    gw = 1.0 / sg

    lo = N_GROUPS + grp * EXPERTS_PER_GROUP
    in_g = (lanef >= lo) & (lanef < lo + EXPERTS_PER_GROUP)
    le = jnp.where(in_g, lg, neg)
    m1 = jnp.max(le, axis=-1, keepdims=True)
    i1 = jnp.min(jnp.where(le == m1, lanef, big), axis=-1, keepdims=True)
    le2 = jnp.where(lanef == i1, neg, le)
    m2 = jnp.max(le2, axis=-1, keepdims=True)
    i2 = jnp.min(jnp.where(le2 == m2, lanef, big), axis=-1, keepdims=True)
    t = jnp.exp(m2 - m1)
    p1 = 1.0 / (1.0 + t)
    w1 = p1 * gw
    w2 = (t * p1) * gw
    e1 = i1 - N_GROUPS
    e2 = i2 - N_GROUPS

    oh1 = (lanef == e1)
    oh2 = (lanef == e2)
    oh = (oh1 | oh2).astype(F32)
    r_i = lax.broadcasted_iota(I32, (tm, tm), 0)
    c_i = lax.broadcasted_iota(I32, (tm, tm), 1)
    tri = (c_i < r_i).astype(BF16)
    cs = _dot(tri, oh.astype(BF16)) + cnt_ref[...].astype(F32)
    rank1 = jnp.sum(jnp.where(oh1, cs, 0.0), axis=-1, keepdims=True)
    rank2 = jnp.sum(jnp.where(oh2, cs, 0.0), axis=-1, keepdims=True)
    cnt_ref[...] = cnt_ref[...] + jnp.sum(oh, axis=0, keepdims=True).astype(I32)

    ri = jnp.where(lane == 0, e1, jnp.where(lane == 1, e2, jnp.where(lane == 2, rank1, jnp.where(lane == 3, rank2, 0.0))))
    ri_ref[...] = ri.astype(I32)
    rw_ref[...] = jnp.where(lane == 0, w1, jnp.where(lane == 1, w2, 0.0))


def _router(h, wcat, bcat, tm=256):
    T, D = h.shape
    return pl.pallas_call(
        functools.partial(_router_kernel, tm=tm),
        grid=(T // tm,),
        in_specs=[
            pl.BlockSpec((tm, D), lambda i: (i, 0)),
            pl.BlockSpec((D, LANES), lambda i: (0, 0)),
            pl.BlockSpec((1, LANES), lambda i: (0, 0)),
        ],
        out_specs=[
            pl.BlockSpec((tm, LANES), lambda i: (i, 0)),
            pl.BlockSpec((tm, LANES), lambda i: (i, 0)),
            pl.BlockSpec((1, LANES), lambda i: (0, 0)),
        ],
        out_shape=[
            jax.ShapeDtypeStruct((T, LANES), I32),
            jax.ShapeDtypeStruct((T, LANES), F32),
            jax.ShapeDtypeStruct((1, LANES), I32),
        ],
        compiler_params=_cparams(("arbitrary",)),
        name="moe_router",
    )(h, wcat, bcat)


def _dispatch_kernel(dest_ref, zs_ref, zc_ref, nu_ref, x_hbm, xs_hbm, zblk, sem, *, tm, n_chunks):
    s = pl.program_id(0)

    def row_copy(t, d):
        return pltpu.make_async_copy(x_hbm.at[pl.ds(t, 1)], xs_hbm.at[pl.ds(d, 1)], sem)

    def zero_row_copy(d):
        return pltpu.make_async_copy(zblk.at[pl.ds(0, 1)], xs_hbm.at[pl.ds(d, 1)], sem)

    def zero_chunk_copy(c):
        return pltpu.make_async_copy(zblk, xs_hbm.at[pl.ds(pl.multiple_of(c * MOE_CHUNK, MOE_CHUNK), MOE_CHUNK)], sem)

    @pl.when(s == 0)
    def _():
        zblk[...] = jnp.zeros_like(zblk)

        def fill(start):
            def per_expert(e, c):
                def per_row(r, c2):
                    cp = zero_row_copy(zs_ref[e] + r)
                    if start:
                        cp.start()
                    else:
                        cp.wait()
                    return c2
                lax.fori_loop(0, zc_ref[e], per_row, 0)
                return c
            lax.fori_loop(0, N_EXPERTS, per_expert, 0)

            def per_chunk(c, c2):
                cp = zero_chunk_copy(c)
                if start:
                    cp.start()
                else:
                    cp.wait()
                return c2
            lax.fori_loop(nu_ref[0], n_chunks, per_chunk, 0)

        fill(True)
        fill(False)

    base = s * tm

    def issue(r, c):
        t = base + r
        row_copy(t, dest_ref[2 * t]).start()
        row_copy(t, dest_ref[2 * t + 1]).start()
        return c

    lax.fori_loop(0, tm, issue, 0)

    def drain(r, c):
        row_copy(0, 0).wait()
        row_copy(0, 0).wait()
        return c

    lax.fori_loop(0, tm, drain, 0)


def _dispatch(x, dest_flat, zstart, zcount, n_used, n_rows, tm=256):
    T, D = x.shape
    return pl.pallas_call(
        functools.partial(_dispatch_kernel, tm=tm, n_chunks=n_rows // MOE_CHUNK),
        grid_spec=pltpu.PrefetchScalarGridSpec(
            num_scalar_prefetch=4,
            grid=(T // tm,),
            in_specs=[pl.BlockSpec(memory_space=pl.ANY)],
            out_specs=pl.BlockSpec(memory_space=pl.ANY),
            scratch_shapes=[pltpu.VMEM((MOE_CHUNK, D), F32), pltpu.SemaphoreType.DMA(())],
        ),
        out_shape=jax.ShapeDtypeStruct((n_rows, D), F32),
        compiler_params=_cparams(("arbitrary",)),
        name="moe_dispatch",
    )(dest_flat, zstart, zcount, n_used, x)


def _moe_mlp_kernel(ce_ref, nu_ref, x_ref, wgu_ref, wd_ref, o_ref, *, F):
    c = pl.program_id(0)

    @pl.when(c < nu_ref[0])
    def _():
        x = x_ref[...].astype(BF16)
        gu = _dot(x, wgu_ref[0].astype(BF16))
        g = gu[:, :F]
        u = gu[:, F:]
        hmid = (g * _sigmoid(g)) * u
        o_ref[...] = _dot(hmid.astype(BF16), wd_ref[0].astype(BF16))

    @pl.when(c >= nu_ref[0])
    def _():
        o_ref[...] = jnp.zeros_like(o_ref)


def _moe_mlp(xs, chunk_e, n_used, w_gu, w_down):
    n_rows, D = xs.shape
    E, _, F2 = w_gu.shape
    F = F2 // 2
    n_chunks = n_rows // MOE_CHUNK

    def x_map(c, ce, nu):
        return (jnp.minimum(c, nu[0] - 1), 0)

    return pl.pallas_call(
        functools.partial(_moe_mlp_kernel, F=F),
        grid_spec=pltpu.PrefetchScalarGridSpec(
            num_scalar_prefetch=2,
            grid=(n_chunks,),
            in_specs=[
                pl.BlockSpec((MOE_CHUNK, D), x_map),
                pl.BlockSpec((1, D, F2), lambda c, ce, nu: (ce[c], 0, 0)),
                pl.BlockSpec((1, F, D), lambda c, ce, nu: (ce[c], 0, 0)),
            ],
            out_specs=pl.BlockSpec((MOE_CHUNK, D), lambda c, ce, nu: (c, 0)),
        ),
        out_shape=jax.ShapeDtypeStruct((n_rows, D), F32),
        compiler_params=_cparams(("arbitrary",), VMEM_LIMIT),
        name="moe_mlp",
    )(chunk_e, n_used, xs, w_gu, w_down)


def _combine_kernel(dest_ref, y_hbm, rw_ref, h_ref, g_ref, b_ref, of_ref, ob_ref, gbuf, sem, *, tm, nsteps):
    s = pl.program_id(0)

    def row_copy(src_row, slot, k, r):
        return pltpu.make_async_copy(y_hbm.at[pl.ds(src_row, 1)], gbuf.at[slot, k, pl.ds(r, 1)], sem.at[slot])

    def issue(step, slot):
        base = step * tm

        def body(r, c):
            t = base + r
            row_copy(dest_ref[2 * t], slot, 0, r).start()
            row_copy(dest_ref[2 * t + 1], slot, 1, r).start()
            return c

        lax.fori_loop(0, tm, body, 0)

    @pl.when(s == 0)
    def _():
        issue(0, 0)

    @pl.when(s + 1 < nsteps)
    def _():
        issue(s + 1, (s + 1) % 2)

    slot = s % 2

    def drain(r, c):
        row_copy(0, slot, 0, 0).wait()
        row_copy(0, slot, 1, 0).wait()
        return c

    lax.fori_loop(0, tm, drain, 0)

    w0 = rw_ref[:, 0:1]
    w1 = rw_ref[:, 1:2]
    y = w0 * gbuf[slot, 0] + w1 * gbuf[slot, 1]
    h = _layer_norm(DEEPNORM_ALPHA * h_ref[...] + y, g_ref[...], b_ref[...])
    of_ref[...] = h
    ob_ref[...] = h.astype(BF16)


def _combine_res_ln(y_rows, dest_flat, rw, h, g, b, tm=256):
    T, D = h.shape
    nsteps = T // tm
    row = lambda i, d: (i, 0)
    fix = lambda i, d: (0, 0)
    return pl.pallas_call(
        functools.partial(_combine_kernel, tm=tm, nsteps=nsteps),
        grid_spec=pltpu.PrefetchScalarGridSpec(
            num_scalar_prefetch=1,
            grid=(nsteps,),
            in_specs=[
                pl.BlockSpec(memory_space=pl.ANY),
                pl.BlockSpec((tm, LANES), row),
                pl.BlockSpec((tm, D), row),
                pl.BlockSpec((1, D), fix), pl.BlockSpec((1, D), fix),
            ],
            out_specs=[pl.BlockSpec((tm, D), row), pl.BlockSpec((tm, D), row)],
            scratch_shapes=[pltpu.VMEM((2, 2, tm, D), F32), pltpu.SemaphoreType.DMA((2,))],
        ),
        out_shape=[jax.ShapeDtypeStruct((T, D), F32), jax.ShapeDtypeStruct((T, D), BF16)],
        compiler_params=_cparams(("arbitrary",), VMEM_LIMIT),
        name="moe_combine",
    )(dest_flat, y_rows, rw, h, g, b)


def _hier_moe_res_ln(h, wg, bg, we, be, w_gu, w_down, g, b):
    T, D = h.shape
    wcat = jnp.zeros((D, LANES), F32).at[:, :N_GROUPS].set(wg).at[:, N_GROUPS:N_GROUPS + N_EXPERTS].set(we)
    bcat = jnp.zeros((1, LANES), F32).at[0, :N_GROUPS].set(bg).at[0, N_GROUPS:N_GROUPS + N_EXPERTS].set(be)
    ri, rw, cnt = _router(h, wcat, bcat)

    counts = cnt[0, :N_EXPERTS]
    padded = (counts + MOE_CHUNK - 1) // MOE_CHUNK * MOE_CHUNK
    pad_end = jnp.cumsum(padded)
    pad_start = pad_end - padded
    A = T * 2
    n_chunks = (A + N_EXPERTS * (MOE_CHUNK - 1) + MOE_CHUNK - 1) // MOE_CHUNK
    n_rows = n_chunks * MOE_CHUNK
    dest = pad_start[ri[:, 0:2]] + ri[:, 2:4]
    dest_flat = dest.reshape(-1).astype(I32)
    n_used = (pad_end[-1] // MOE_CHUNK).astype(I32)
    cidx = jnp.minimum(jnp.arange(n_chunks, dtype=I32), n_used - 1)
    chunk_e = jnp.clip(jnp.searchsorted(pad_end, cidx * MOE_CHUNK, side='right'), 0, N_EXPERTS - 1).astype(I32)
    zstart = (pad_start + counts).astype(I32)
    zcount = (padded - counts).astype(I32)

    n_used = n_used.reshape(1)
    xs = _dispatch(h, dest_flat, zstart, zcount, n_used, n_rows)
    ys = _moe_mlp(xs, chunk_e, n_used, w_gu, w_down)
    return _combine_res_ln(ys, dest_flat, rw, h, g, b)


def _proj_heads_kernel(x_ref, w_ref, o_ref, *, nh, scale):
    acc = _dot(x_ref[...], w_ref[...])
    if scale != 1.0:
        acc = acc * scale
    for c in range(nh):
        o_ref[c] = acc[:, c * HEAD_DIM:(c + 1) * HEAD_DIM].astype(o_ref.dtype)


def _proj_heads(xb, w, out_dtype, scale=1.0, tm=512, tn=512):
    T, D = xb.shape
    N = w.shape[1]
    nh = tn // HEAD_DIM
    return pl.pallas_call(
        functools.partial(_proj_heads_kernel, nh=nh, scale=scale),
        grid=(N // tn, T // tm),
        in_specs=[
            pl.BlockSpec((tm, D), lambda j, i: (i, 0)),
            pl.BlockSpec((D, tn), lambda j, i: (0, j)),
        ],
        out_specs=pl.BlockSpec((nh, tm, HEAD_DIM), lambda j, i: (j, i, 0)),
        out_shape=jax.ShapeDtypeStruct((N // HEAD_DIM, T, HEAD_DIM), out_dtype),
        compiler_params=_cparams(("parallel", "parallel"), VMEM_LIMIT),
        name="proj_heads",
    )(xb, w)


def _gates_kernel(x_ref, w_ref, o_ref):
    o_ref[...] = _sigmoid(_dot(x_ref[...], w_ref[...]))


def _gates(xb, w, tm=512):
    T, D = xb.shape
    return pl.pallas_call(
        _gates_kernel,
        grid=(T // tm,),
        in_specs=[pl.BlockSpec((tm, D), lambda i: (i, 0)), pl.BlockSpec((D, LANES), lambda i: (0, 0))],
        out_specs=pl.BlockSpec((tm, LANES), lambda i: (i, 0)),
        out_shape=jax.ShapeDtypeStruct((T, LANES), F32),
        compiler_params=_cparams(("parallel",)),
        name="nsa_gates",
    )(xb, w)


def _cmp_mlp_kernel(x_ref, pos_ref, w1_ref, w2_ref, o_ref, *, nhb):
    x = x_ref[0, 0]
    half = x.shape[1]
    lo = (x + pos_ref[0, 0:1, :]).astype(BF16)
    hi = (x + pos_ref[0, 1:2, :]).astype(BF16)
    a_lo = _dot(lo, w1_ref[0, 0:half, :])
    a_hi = _dot(hi, w1_ref[0, half:2 * half, :])
    pre = a_lo + pltpu.roll(a_hi, nhb - 1, 0)
    hid = 0.5 * pre * (1.0 + jnp.tanh(0.7978845608028654 * (pre + 0.044715 * pre * pre * pre)))
    comp = _dot(hid.astype(BF16), w2_ref[0])
    row = lax.broadcasted_iota(I32, comp.shape, 0)
    o_ref[0, 0] = jnp.where(row < nhb - 1, comp, 0.0).astype(o_ref.dtype)


def _cmp_mlp(kvc, pos, w1, w2, B):
    R, T, dh = kvc.shape
    S = T // B
    nhb = S // CMP_STRIDE
    half = CMP_STRIDE * dh
    x4 = kvc.reshape(R, B, nhb, half)
    H = w1.shape[-1]
    return pl.pallas_call(
        functools.partial(_cmp_mlp_kernel, nhb=nhb),
        grid=(R, B),
        in_specs=[
            pl.BlockSpec((1, 1, nhb, half), lambda r, bb: (r, bb, 0, 0)),
            pl.BlockSpec((1, 2, half), lambda r, bb: (r // N_KV_GROUPS, 0, 0)),
            pl.BlockSpec((1, 2 * half, H), lambda r, bb: (r // N_KV_GROUPS, 0, 0)),
            pl.BlockSpec((1, H, dh), lambda r, bb: (r // N_KV_GROUPS, 0, 0)),
        ],
        out_specs=pl.BlockSpec((1, 1, nhb, dh), lambda r, bb: (r, bb, 0, 0)),
        out_shape=jax.ShapeDtypeStruct((R, B, nhb, dh), BF16),
        compiler_params=_cparams(("parallel", "parallel"), VMEM_LIMIT),
        name="cmp_mlp",
    )(x4, pos, w1, w2)


def _cmp_attn_kernel(q_ref, k_ref, v_ref, sl_ref, ov_ref, o_ref, sel_ref, *, tq, nhb, n_sel, topk):
    i = pl.program_id(2)
    q0 = i * tq
    qpos = q0 + lax.broadcasted_iota(I32, (tq, 1), 0)
    n = lax.broadcasted_iota(I32, (1, nhb), 1)
    dist = qpos - (n * CMP_STRIDE + (CMP_BLOCK - 1))
    valid = (dist >= 0) & (n < nhb - 1)
    distf = dist.astype(F32)
    k = k_ref[0, 0]
    v = v_ref[0, 0]
    psum = jnp.zeros((tq, nhb), F32)
    for h in range(HEADS_PER_GROUP):
        s = _dot_nt(q_ref[h], k) - sl_ref[0, h:h + 1, 0:1] * distf
        s = jnp.where(valid, s, MASK_VALUE)
        m = jnp.max(s, axis=-1, keepdims=True)
        e = jnp.where(valid, jnp.exp(s - m), 0.0)
        l = jnp.sum(e, axis=-1, keepdims=True)
        p = e / jnp.maximum(l, 1e-30)
        o_ref[h] = _dot(p.astype(BF16), v)
        psum = psum + p

    ov = ov_ref[...]
    p_hi = psum.astype(BF16)
    p_lo = (psum - p_hi.astype(F32)).astype(BF16)
    imp = _dot(p_hi, ov) + _dot(p_lo, ov)

    blk = lax.broadcasted_iota(I32, (tq, LANES), 1)
    blkf = blk.astype(F32)
    cur = lax.shift_right_logical(qpos, 6)
    in_range = (blk <= cur) & (blk < n_sel)
    forced = in_range & ((blk == 0) | (blk == cur) | (blk == cur - 1))
    n_forced = 1 + (cur >= 1).astype(I32) + (cur >= 2).astype(I32)
    k_free = topk - n_forced
    neg = jnp.float32(-jnp.inf)
    score = jnp.where(in_range & jnp.logical_not(forced), imp, neg)
    selected = forced
    for it in range(max(topk - 3, 0)):
        m = jnp.max(score, axis=-1, keepdims=True)
        idx = jnp.min(jnp.where(score == m, blkf, 1e9), axis=-1, keepdims=True)
        pick = (blkf == idx) & (m > neg) & (it < k_free)
        selected = selected | pick
        score = jnp.where(pick, neg, score)
    sel_ref[0] = jnp.where(selected, 1.0, 0.0).astype(sel_ref.dtype)


def _cmp_attn(q_t, kvcmp, slopes, ov, B, tq=256):
    NH, T, dh = q_t.shape
    S = T // B
    nq = S // tq
    nhb = kvcmp.shape[2]
    n_sel = S // SEL_BLOCK
    topk = min(SEL_TOPK, n_sel)
    return pl.pallas_call(
        functools.partial(_cmp_attn_kernel, tq=tq, nhb=nhb, n_sel=n_sel, topk=topk),
        grid=(B, N_KV_GROUPS, nq),
        in_specs=[
            pl.BlockSpec((HEADS_PER_GROUP, tq, dh), lambda bb, g, i: (g, bb * nq + i, 0)),
            pl.BlockSpec((1, 1, nhb, dh), lambda bb, g, i: (g, bb, 0, 0)),
            pl.BlockSpec((1, 1, nhb, dh), lambda bb, g, i: (N_KV_GROUPS + g, bb, 0, 0)),
            pl.BlockSpec((1, 8, LANES), lambda bb, g, i: (g, 0, 0)),
            pl.BlockSpec((nhb, LANES), lambda bb, g, i: (0, 0)),
        ],
        out_specs=[
            pl.BlockSpec((HEADS_PER_GROUP, tq, dh), lambda bb, g, i: (g, bb * nq + i, 0)),
            pl.BlockSpec((1, tq, LANES), lambda bb, g, i: (g, bb * nq + i, 0)),
        ],
        out_shape=[
            jax.ShapeDtypeStruct((NH, T, dh), F32),
            jax.ShapeDtypeStruct((N_KV_GROUPS, T, LANES), BF16),
        ],
        compiler_params=_cparams(("parallel", "parallel", "parallel")),
        name="cmp_attn",
    )(q_t, kvcmp, kvcmp, slopes, ov)


def _flash_kernel(*refs, mode, tq, tk):
    if mode == "sel":
        q_ref, k_ref, v_ref, sl_ref, sel_ref, e_ref, o_ref, m_scr, l_scr, acc_scr = refs
    else:
        q_ref, k_ref, v_ref, sl_ref, o_ref, m_scr, l_scr, acc_scr = refs
    i = pl.program_id(2)
    j = pl.program_id(3)
    nj = pl.num_programs(3)
    q0 = i * tq
    if mode == "sel":
        kt = j
        needed = j * tk <= q0 + tq - 1
    else:
        kt = i * (tq // tk) - WINDOW // tk + j
        needed = kt >= 0
    k0 = kt * tk

    @pl.when(j == 0)
    def _():
        m_scr[...] = jnp.full_like(m_scr, MASK_VALUE)
        l_scr[...] = jnp.zeros_like(l_scr)
        acc_scr[...] = jnp.zeros_like(acc_scr)

    @pl.when(needed)
    def _():
        qpos = q0 + lax.broadcasted_iota(I32, (tq, 1), 0)
        kpos = k0 + lax.broadcasted_iota(I32, (1, tk), 1)
        dist = qpos - kpos
        if mode == "sel":
            picked = _dot(sel_ref[0], e_ref[...])
            valid = (dist >= 0) & (picked > 0.5)
        else:
            valid = (dist >= 0) & (dist < WINDOW)
        krel = (kpos - q0).astype(F32)
        k = k_ref[0]
        v = v_ref[0]
        for h in range(HEADS_PER_GROUP):
            s = _dot_nt(q_ref[h], k) + sl_ref[0, h:h + 1, 0:1] * krel
            s = jnp.where(valid, s, MASK_VALUE)
            m_prev = m_scr[h]
            m_new = jnp.maximum(m_prev, jnp.max(s, axis=-1, keepdims=True))
            alpha = jnp.exp(m_prev - m_new)
            p = jnp.where(valid, jnp.exp(s - m_new), 0.0)
            l_scr[h] = alpha * l_scr[h] + jnp.sum(p, axis=-1, keepdims=True)
            acc_scr[h] = alpha * acc_scr[h] + _dot(p.astype(BF16), v)
            m_scr[h] = m_new

    @pl.when(j == nj - 1)
    def _():
        for h in range(HEADS_PER_GROUP):
            o_ref[h] = acc_scr[h] / l_scr[h]


def _flash(mode, q_t, kv_t, k_base, v_base, slopes, B, sel=None, emat=None, tq=256, tk=256):
    NH, T, dh = q_t.shape
    S = T // B
    nq = S // tq
    nkb = S // tk
    if mode == "sel":
        nkv = nkb

        def kmap(base):
            return lambda bb, g, i, j: (base + g, bb * nkb + jnp.minimum(j, ((i + 1) * tq - 1) // tk), 0)
    else:
        nkv = WINDOW // tk + tq // tk

        def kmap(base):
            return lambda bb, g, i, j: (base + g, bb * nkb + jnp.maximum(i * (tq // tk) - WINDOW // tk + j, 0), 0)

    qmap = lambda bb, g, i, j: (g, bb * nq + i, 0)
    in_specs = [
        pl.BlockSpec((HEADS_PER_GROUP, tq, dh), qmap),
        pl.BlockSpec((1, tk, dh), kmap(k_base)),
        pl.BlockSpec((1, tk, dh), kmap(v_base)),
        pl.BlockSpec((1, 8, LANES), lambda bb, g, i, j: (g, 0, 0)),
    ]
    args = [q_t, kv_t, kv_t, slopes]
    if mode == "sel":
        in_specs += [
            pl.BlockSpec((1, tq, LANES), qmap),
            pl.BlockSpec((LANES, tk), lambda bb, g, i, j: (0, jnp.minimum(j, ((i + 1) * tq - 1) // tk))),
        ]
        args += [sel, emat]
    return pl.pallas_call(
        functools.partial(_flash_kernel, mode=mode, tq=tq, tk=tk),
        grid=(B, N_KV_GROUPS, nq, nkv),
        in_specs=in_specs,
        out_specs=pl.BlockSpec((HEADS_PER_GROUP, tq, dh), qmap),
        out_shape=jax.ShapeDtypeStruct((NH, T, dh), F32),
        scratch_shapes=[
            pltpu.VMEM((HEADS_PER_GROUP, tq, 1), F32),
            pltpu.VMEM((HEADS_PER_GROUP, tq, 1), F32),
            pltpu.VMEM((HEADS_PER_GROUP, tq, dh), F32),
        ],
        compiler_params=_cparams(("parallel", "parallel", "parallel", "arbitrary")),
        name="nsa_" + mode + "_attn",
    )(*args)


def _attn_out_kernel(oc_ref, os_ref, ow_ref, gt_ref, x_ref, w_ref, g_ref, b_ref, of_ref, ob_ref, lhs):
    for hd in range(N_HEADS):
        g0 = gt_ref[:, N_BRANCHES * hd:N_BRANCHES * hd + 1]
        g1 = gt_ref[:, N_BRANCHES * hd + 1:N_BRANCHES * hd + 2]
        g2 = gt_ref[:, N_BRANCHES * hd + 2:N_BRANCHES * hd + 3]
        o = g0 * oc_ref[hd] + g1 * os_ref[hd] + g2 * ow_ref[hd]
        lhs[:, hd * HEAD_DIM:(hd + 1) * HEAD_DIM] = o.astype(BF16)
    mix = _dot(lhs[...], w_ref[...])
    h = _layer_norm(DEEPNORM_ALPHA * x_ref[...] + mix, g_ref[...], b_ref[...])
    of_ref[...] = h
    ob_ref[...] = h.astype(BF16)


def _attn_out_res_ln(o_cmp, o_sel, o_win, gates, x, w, g, b, tm=256):
    T, D = x.shape
    NH = o_cmp.shape[0]
    hmap = lambda i: (0, i, 0)
    row = lambda i: (i, 0)
    fix = lambda i: (0, 0)
    return pl.pallas_call(
        _attn_out_kernel,
        grid=(T // tm,),
        in_specs=[
            pl.BlockSpec((NH, tm, HEAD_DIM), hmap), pl.BlockSpec((NH, tm, HEAD_DIM), hmap),
            pl.BlockSpec((NH, tm, HEAD_DIM), hmap),
            pl.BlockSpec((tm, LANES), row), pl.BlockSpec((tm, D), row),
            pl.BlockSpec((D, D), fix),
            pl.BlockSpec((1, D), fix), pl.BlockSpec((1, D), fix),
        ],
        out_specs=[pl.BlockSpec((tm, D), row), pl.BlockSpec((tm, D), row)],
        out_shape=[jax.ShapeDtypeStruct((T, D), F32), jax.ShapeDtypeStruct((T, D), BF16)],
        scratch_shapes=[pltpu.VMEM((tm, D), BF16)],
        compiler_params=_cparams(("parallel",), VMEM_LIMIT),
        name="attn_out_res_ln",
    )(o_cmp, o_sel, o_win, gates, x, w, g, b)


def _nsa_tables(S):
    nhb = S // CMP_STRIDE
    n_sel = S // SEL_BLOCK
    heads = np.arange(1, N_HEADS + 1, dtype=np.float32)
    slopes = np.zeros((N_KV_GROUPS, 8, LANES), np.float32)
    slopes[:, :HEADS_PER_GROUP, :] = (2.0 ** (-8.0 * heads / N_HEADS)).reshape(N_KV_GROUPS, HEADS_PER_GROUP, 1)
    c0 = np.arange(nhb)[:, None] * CMP_STRIDE
    j0 = np.arange(LANES)[None, :] * SEL_BLOCK
    ov = np.maximum(np.minimum(c0 + CMP_BLOCK, j0 + SEL_BLOCK) - np.maximum(c0, j0), 0).astype(np.float32) / CMP_BLOCK
    ov[nhb - 1, :] = 0.0
    ov[:, n_sel:] = 0.0
    emat = (np.arange(S)[None, :] // SEL_BLOCK == np.arange(LANES)[:, None]).astype(np.float32)
    return jnp.asarray(slopes), jnp.asarray(ov, dtype=BF16), jnp.asarray(emat, dtype=BF16)


def _nsa_layer_res_ln(hf, hb, B, kv_w, cmp_pos, cmp_w1, cmp_w2, w_qg, w_o, g, b):
    T, D = hf.shape
    S = T // B
    hd = N_HEADS * HEAD_DIM
    gsz = N_KV_GROUPS * HEAD_DIM
    slopes, ov, emat = _nsa_tables(S)

    kvw = kv_w.astype(BF16)
    kvc = _proj_heads(hb, kvw[:, :2 * gsz], F32)
    kv_t = _proj_heads(hb, kvw[:, 2 * gsz:], BF16)
    wq = w_qg[:, :hd].astype(BF16)
    wgt = jnp.zeros((D, LANES), F32).at[:, :N_BRANCHES * N_HEADS].set(w_qg[:, hd:]).astype(BF16)
    q_t = _proj_heads(hb, wq, BF16, scale=HEAD_DIM ** -0.5)
    gates = _gates(hb, wgt)

    pos = cmp_pos.reshape(2, 2, CMP_STRIDE * HEAD_DIM)
    kvcmp = _cmp_mlp(kvc, pos, cmp_w1.astype(BF16), cmp_w2.astype(BF16), B)
    o_cmp, sel = _cmp_attn(q_t, kvcmp, slopes, ov, B)
    o_sel = _flash("sel", q_t, kv_t, 0, N_KV_GROUPS, slopes, B, sel=sel, emat=emat)
    o_win = _flash("win", q_t, kv_t, 2 * N_KV_GROUPS, 3 * N_KV_GROUPS, slopes, B)
    return _attn_out_res_ln(o_cmp, o_sel, o_win, gates, hf, w_o.astype(BF16), g, b)


def kernel(x, conv_w_pw1, conv_b_pw1, conv_w_dw, conv_b_dw, conv_ln_g, conv_ln_b, conv_w_pw2, kv_w, cmp_pos, cmp_w1, cmp_w2, nsa_w_qg, nsa_w_o, moe_wg, moe_bg, moe_we, moe_be, moe_w_gu, moe_w_down, ln_g, ln_b):
    B, S, D = x.shape
    T = B * S
    hf = x.reshape(T, D)
    hb = hf.astype(BF16)
    r = lambda v: v.reshape(1, -1)
    for l in range(DEPTH):
        if l < N_A_LAYERS:
            glu = _pw1_glu(hb, conv_w_pw1[l].astype(BF16), r(conv_b_pw1[l]))
            conv = _dwconv(glu.reshape(B, S, D), conv_w_dw[l], r(conv_b_dw[l])).reshape(T, D)
            hf, hb = _pw2_res_ln(conv, hf, conv_w_pw2[l].astype(BF16), r(conv_ln_g[l]), r(conv_ln_b[l]),
                                 r(ln_g[l, 0]), r(ln_b[l, 0]))
        else:
            j = l - N_A_LAYERS
            hf, hb = _nsa_layer_res_ln(hf, hb, B, kv_w, cmp_pos, cmp_w1, cmp_w2, nsa_w_qg[j], nsa_w_o[j],
                                       r(ln_g[l, 0]), r(ln_b[l, 0]))
        hf, hb = _hier_moe_res_ln(hf, moe_wg[l], moe_bg[l], moe_we[l], moe_be[l], moe_w_gu[l], moe_w_down[l],
                                  r(ln_g[l, 1]), r(ln_b[l, 1]))
    return hf.reshape(B, S, D)
```

```python
import functools

import numpy as np
import jax
import jax.numpy as jnp
from jax import lax
from jax.experimental import pallas as pl
from jax.experimental.pallas import tpu as pltpu

F32 = jnp.float32
BF16 = jnp.bfloat16
I32 = jnp.int32

DEPTH = 2
N_A_LAYERS = DEPTH // 2
DEEPNORM_ALPHA = (2.0 * DEPTH) ** 0.25
LN_EPS = 1e-5
CONV_WIDTH = 31
N_HEADS = 16
HEAD_DIM = 128
N_KV_GROUPS = 4
HEADS_PER_GROUP = N_HEADS // N_KV_GROUPS
CMP_BLOCK = 32
CMP_STRIDE = 16
SEL_BLOCK = 64
SEL_TOPK = 16
WINDOW = 512
N_BRANCHES = 3
MASK_VALUE = -1e30
N_GROUPS = 4
EXPERTS_PER_GROUP = 8
N_EXPERTS = N_GROUPS * EXPERTS_PER_GROUP
MOE_CHUNK = 128

LANES = 128
VMEM_LIMIT = 56 * 1024 * 1024

CONV_HALO = 32


def _cparams(sem, vmem=None):
    return pltpu.CompilerParams(dimension_semantics=sem, vmem_limit_bytes=vmem)


def _sigmoid(x):
    return 1.0 / (1.0 + jnp.exp(-x))


def _layer_norm(x, g, b):
    mu = jnp.mean(x, axis=-1, keepdims=True)
    xc = x - mu
    var = jnp.mean(xc * xc, axis=-1, keepdims=True)
    return xc * lax.rsqrt(var + LN_EPS) * g + b


def _dot(a, b):
    return jnp.dot(a, b, preferred_element_type=F32)


def _dot_nt(a, b):
    return lax.dot_general(a, b, (((1,), (1,)), ((), ())), preferred_element_type=F32)


def _pw1_glu_kernel(x_ref, wa_ref, wg_ref, ba_ref, bg_ref, o_ref):
    x = x_ref[...]
    a = _dot(x, wa_ref[...]) + ba_ref[...]
    g = _dot(x, wg_ref[...]) + bg_ref[...]
    o_ref[...] = a * _sigmoid(g)


def _pw1_glu(xb, w, b, tm=512, tn=512):
    T, D = xb.shape
    nj = D // tn
    return pl.pallas_call(
        _pw1_glu_kernel,
        grid=(nj, T // tm),
        in_specs=[
            pl.BlockSpec((tm, D), lambda j, i: (i, 0)),
            pl.BlockSpec((D, tn), lambda j, i: (0, j)),
            pl.BlockSpec((D, tn), lambda j, i: (0, j + nj)),
            pl.BlockSpec((1, tn), lambda j, i: (0, j)),
            pl.BlockSpec((1, tn), lambda j, i: (0, j + nj)),
        ],
        out_specs=pl.BlockSpec((tm, tn), lambda j, i: (i, j)),
        out_shape=jax.ShapeDtypeStruct((T, D), F32),
        compiler_params=_cparams(("parallel", "parallel"), VMEM_LIMIT),
        name="pw1_glu",
    )(xb, w, w, b, b)


def _dwconv_kernel(x_ref, halo_ref, w_ref, b_ref, o_ref, xbuf, *, ts, tc, rc):
    i = pl.program_id(1)
    xbuf[0:CONV_HALO, :] = jnp.where(i > 0, halo_ref[0], 0.0)
    xbuf[CONV_HALO:CONV_HALO + ts, :] = x_ref[0]
    off = CONV_HALO - (CONV_WIDTH - 1)
    for c0 in range(0, tc, LANES):
        for r0 in range(0, ts, rc):
            acc = jnp.broadcast_to(b_ref[:, c0:c0 + LANES], (rc, LANES))
            for k in range(CONV_WIDTH):
                acc = acc + w_ref[k:k + 1, c0:c0 + LANES] * xbuf[r0 + k + off:r0 + k + off + rc, c0:c0 + LANES]
            o_ref[0, r0:r0 + rc, c0:c0 + LANES] = acc


def _dwconv(x, w, b, ts=256, tc=256, rc=128):
    B, S, D = x.shape
    hb = ts // CONV_HALO
    return pl.pallas_call(
        functools.partial(_dwconv_kernel, ts=ts, tc=tc, rc=rc),
        grid=(B, S // ts, D // tc),
        in_specs=[
            pl.BlockSpec((1, ts, tc), lambda bb, i, c: (bb, i, c)),
            pl.BlockSpec((1, CONV_HALO, tc), lambda bb, i, c: (bb, jnp.maximum(i * hb - 1, 0), c)),
            pl.BlockSpec((CONV_WIDTH, tc), lambda bb, i, c: (0, c)),
            pl.BlockSpec((1, tc), lambda bb, i, c: (0, c)),
        ],
        out_specs=pl.BlockSpec((1, ts, tc), lambda bb, i, c: (bb, i, c)),
        out_shape=jax.ShapeDtypeStruct((B, S, D), F32),
        scratch_shapes=[pltpu.VMEM((CONV_HALO + ts, tc), F32)],
        compiler_params=_cparams(("parallel", "parallel", "parallel")),
        name="dwconv",
    )(x, x, w, b)


def _pw2_kernel(c_ref, x_ref, w_ref, cg_ref, cb_ref, g_ref, b_ref, of_ref, ob_ref):
    y = _layer_norm(c_ref[...], cg_ref[...], cb_ref[...])
    y = y * _sigmoid(y)
    mix = _dot(y.astype(BF16), w_ref[...])
    h = _layer_norm(DEEPNORM_ALPHA * x_ref[...] + mix, g_ref[...], b_ref[...])
    of_ref[...] = h
    ob_ref[...] = h.astype(BF16)


def _pw2_res_ln(conv, x, w, cg, cb, g, b, tm=256):
    T, D = x.shape
    row = lambda i: (i, 0)
    fix = lambda i: (0, 0)
    return pl.pallas_call(
        _pw2_kernel,
        grid=(T // tm,),
        in_specs=[
            pl.BlockSpec((tm, D), row), pl.BlockSpec((tm, D), row),
            pl.BlockSpec((D, D), fix),
            pl.BlockSpec((1, D), fix), pl.BlockSpec((1, D), fix),
            pl.BlockSpec((1, D), fix), pl.BlockSpec((1, D), fix),
        ],
        out_specs=[pl.BlockSpec((tm, D), row), pl.BlockSpec((tm, D), row)],
        out_shape=[jax.ShapeDtypeStruct((T, D), F32), jax.ShapeDtypeStruct((T, D), BF16)],
        compiler_params=_cparams(("parallel",), VMEM_LIMIT),
        name="pw2_res_ln",
    )(conv, x, w, cg, cb, g, b)


def _router_kernel(h_ref, w_ref, b_ref, ri_ref, rw_ref, cnt_ref, *, tm):
    s = pl.program_id(0)

    @pl.when(s == 0)
    def _():
        cnt_ref[...] = jnp.zeros_like(cnt_ref)

    h = h_ref[...]
    w = w_ref[...]
    h_hi = h.astype(BF16)
    h_lo = (h - h_hi.astype(F32)).astype(BF16)
    w_hi = w.astype(BF16)
    w_lo = (w - w_hi.astype(F32)).astype(BF16)
    lg = _dot(h_hi, w_hi) + _dot(h_lo, w_hi) + _dot(h_hi, w_lo) + b_ref[...]

    lane = lax.broadcasted_iota(I32, (tm, LANES), 1)
    lanef = lane.astype(F32)
    neg = jnp.float32(-jnp.inf)
    big = jnp.float32(1e9)

    is_g = lane < N_GROUPS
    lgm = jnp.where(is_g, lg, neg)
    mg = jnp.max(lgm, axis=-1, keepdims=True)
    grp = jnp.min(jnp.where(lgm == mg, lanef, big), axis=-1, keepdims=True)
    sg = jnp.sum(jnp.where(is_g, jnp.exp(lg - mg), 0.0), axis=-1, keepdims=True)
    gw = 1.0 / sg

    lo = N_GROUPS + grp * EXPERTS_PER_GROUP
    in_g = (lanef >= lo) & (lanef < lo + EXPERTS_PER_GROUP)
    le = jnp.where(in_g, lg, neg)
    m1 = jnp.max(le, axis=-1, keepdims=True)
    i1 = jnp.min(jnp.where(le == m1, lanef, big), axis=-1, keepdims=True)
    le2 = jnp.where(lanef == i1, neg, le)
    m2 = jnp.max(le2, axis=-1, keepdims=True)
    i2 = jnp.min(jnp.where(le2 == m2, lanef, big), axis=-1, keepdims=True)
    t = jnp.exp(m2 - m1)
    p1 = 1.0 / (1.0 + t)
    w1 = p1 * gw
    w2 = (t * p1) * gw
    e1 = i1 - N_GROUPS
    e2 = i2 - N_GROUPS

    oh1 = (lanef == e1)
    oh2 = (lanef == e2)
    oh = (oh1 | oh2).astype(F32)
    r_i = lax.broadcasted_iota(I32, (tm, tm), 0)
    c_i = lax.broadcasted_iota(I32, (tm, tm), 1)
    tri = (c_i < r_i).astype(BF16)
    cs = _dot(tri, oh.astype(BF16)) + cnt_ref[...].astype(F32)
    rank1 = jnp.sum(jnp.where(oh1, cs, 0.0), axis=-1, keepdims=True)
    rank2 = jnp.sum(jnp.where(oh2, cs, 0.0), axis=-1, keepdims=True)
    cnt_ref[...] = cnt_ref[...] + jnp.sum(oh, axis=0, keepdims=True).astype(I32)

    ri = jnp.where(lane == 0, e1, jnp.where(lane == 1, e2, jnp.where(lane == 2, rank1, jnp.where(lane == 3, rank2, 0.0))))
    ri_ref[...] = ri.astype(I32)
    rw_ref[...] = jnp.where(lane == 0, w1, jnp.where(lane == 1, w2, 0.0))


def _router(h, wcat, bcat, tm=256):
    T, D = h.shape
    return pl.pallas_call(
        functools.partial(_router_kernel, tm=tm),
        grid=(T // tm,),
        in_specs=[
            pl.BlockSpec((tm, D), lambda i: (i, 0)),
            pl.BlockSpec((D, LANES), lambda i: (0, 0)),
            pl.BlockSpec((1, LANES), lambda i: (0, 0)),
        ],
        out_specs=[
            pl.BlockSpec((tm, LANES), lambda i: (i, 0)),
            pl.BlockSpec((tm, LANES), lambda i: (i, 0)),
            pl.BlockSpec((1, LANES), lambda i: (0, 0)),
        ],
        out_shape=[
            jax.ShapeDtypeStruct((T, LANES), I32),
            jax.ShapeDtypeStruct((T, LANES), F32),
            jax.ShapeDtypeStruct((1, LANES), I32),
        ],
        compiler_params=_cparams(("arbitrary",)),
        name="moe_router",
    )(h, wcat, bcat)


def _dispatch_kernel(dest_ref, zs_ref, zc_ref, nu_ref, x_ref, xs_hbm, zblk, sem, *, tm, n_chunks):
    s = pl.program_id(0)

    def row_copy(r, d):
        return pltpu.make_async_copy(x_ref.at[pl.ds(r, 1)], xs_hbm.at[pl.ds(d, 1)], sem)

    def zero_row_copy(d):
        return pltpu.make_async_copy(zblk.at[pl.ds(0, 1)], xs_hbm.at[pl.ds(d, 1)], sem)

    def zero_chunk_copy(c):
        return pltpu.make_async_copy(zblk, xs_hbm.at[pl.ds(pl.multiple_of(c * MOE_CHUNK, MOE_CHUNK), MOE_CHUNK)], sem)

    @pl.when(s == 0)
    def _():
        zblk[...] = jnp.zeros_like(zblk)

        def fill(start):
            def per_expert(e, c):
                def per_row(r, c2):
                    cp = zero_row_copy(zs_ref[e] + r)
                    if start:
                        cp.start()
                    else:
                        cp.wait()
                    return c2
                lax.fori_loop(0, zc_ref[e], per_row, 0)
                return c
            lax.fori_loop(0, N_EXPERTS, per_expert, 0)

            def per_chunk(c, c2):
                cp = zero_chunk_copy(c)
                if start:
                    cp.start()
                else:
                    cp.wait()
                return c2
            lax.fori_loop(nu_ref[0], n_chunks, per_chunk, 0)

        fill(True)
        fill(False)

    base = s * tm

    def issue(r, c):
        t = base + r
        row_copy(r, dest_ref[2 * t]).start()
        row_copy(r, dest_ref[2 * t + 1]).start()
        return c

    lax.fori_loop(0, tm, issue, 0)

    def drain(r, c):
        row_copy(0, 0).wait()
        row_copy(0, 0).wait()
        return c

    lax.fori_loop(0, tm, drain, 0)


def _dispatch(x, dest_flat, zstart, zcount, n_used, n_rows, tm=256):
    T, D = x.shape
    return pl.pallas_call(
        functools.partial(_dispatch_kernel, tm=tm, n_chunks=n_rows // MOE_CHUNK),
        grid_spec=pltpu.PrefetchScalarGridSpec(
            num_scalar_prefetch=4,
            grid=(T // tm,),
            in_specs=[pl.BlockSpec((tm, D), lambda i, *_: (i, 0))],
            out_specs=pl.BlockSpec(memory_space=pl.ANY),
            scratch_shapes=[pltpu.VMEM((MOE_CHUNK, D), F32), pltpu.SemaphoreType.DMA(())],
        ),
        out_shape=jax.ShapeDtypeStruct((n_rows, D), F32),
        compiler_params=_cparams(("arbitrary",)),
        name="moe_dispatch",
    )(dest_flat, zstart, zcount, n_used, x)


def _moe_mlp_kernel(ce_ref, nu_ref, x_ref, wgu_ref, wd_ref, o_ref, *, F):
    c = pl.program_id(0)

    @pl.when(c < nu_ref[0])
    def _():
        x = x_ref[...].astype(BF16)
        gu = _dot(x, wgu_ref[0, 0].astype(BF16))
        g = gu[:, :F]
        u = gu[:, F:]
        hmid = (g * _sigmoid(g)) * u
        o_ref[...] = _dot(hmid.astype(BF16), wd_ref[0, 0].astype(BF16))

    @pl.when(c >= nu_ref[0])
    def _():
        o_ref[...] = jnp.zeros_like(o_ref)


def _moe_mlp(xs, chunk_e, n_used, w_gu, w_down, layer):
    n_rows, D = xs.shape
    F2 = w_gu.shape[-1]
    F = F2 // 2
    n_chunks = n_rows // MOE_CHUNK

    def x_map(c, ce, nu):
        return (jnp.minimum(c, nu[0] - 1), 0)

    return pl.pallas_call(
        functools.partial(_moe_mlp_kernel, F=F),
        grid_spec=pltpu.PrefetchScalarGridSpec(
            num_scalar_prefetch=2,
            grid=(n_chunks,),
            in_specs=[
                pl.BlockSpec((MOE_CHUNK, D), x_map),
                pl.BlockSpec((1, 1, D, F2), lambda c, ce, nu: (layer, ce[c], 0, 0)),
                pl.BlockSpec((1, 1, F, D), lambda c, ce, nu: (layer, ce[c], 0, 0)),
            ],
            out_specs=pl.BlockSpec((MOE_CHUNK, D), lambda c, ce, nu: (c, 0)),
        ),
        out_shape=jax.ShapeDtypeStruct((n_rows, D), F32),
        compiler_params=_cparams(("arbitrary",), VMEM_LIMIT),
        name="moe_mlp",
    )(chunk_e, n_used, xs, w_gu, w_down)


def _combine_kernel(dest_ref, y_hbm, rw_ref, h_ref, g_ref, b_ref, of_ref, ob_ref, gbuf, sem, *, tm, nsteps):
    s = pl.program_id(0)

    def row_copy(src_row, slot, k, r):
        return pltpu.make_async_copy(y_hbm.at[pl.ds(src_row, 1)], gbuf.at[slot, k, pl.ds(r, 1)], sem.at[slot])

    def issue(step, slot):
        base = step * tm

        def body(r, c):
            t = base + r
            row_copy(dest_ref[2 * t], slot, 0, r).start()
            row_copy(dest_ref[2 * t + 1], slot, 1, r).start()
            return c

        lax.fori_loop(0, tm, body, 0)

    @pl.when(s == 0)
    def _():
        issue(0, 0)

    @pl.when(s + 1 < nsteps)
    def _():
        issue(s + 1, (s + 1) % 2)

    slot = s % 2

    def drain(r, c):
        row_copy(0, slot, 0, 0).wait()
        row_copy(0, slot, 1, 0).wait()
        return c

    lax.fori_loop(0, tm, drain, 0)

    w0 = rw_ref[:, 0:1]
    w1 = rw_ref[:, 1:2]
    y = w0 * gbuf[slot, 0] + w1 * gbuf[slot, 1]
    h = _layer_norm(DEEPNORM_ALPHA * h_ref[...] + y, g_ref[...], b_ref[...])
    of_ref[...] = h
    ob_ref[...] = h.astype(BF16)


def _combine_res_ln(y_rows, dest_flat, rw, h, g, b, tm=256):
    T, D = h.shape
    nsteps = T // tm
    row = lambda i, d: (i, 0)
    fix = lambda i, d: (0, 0)
    return pl.pallas_call(
        functools.partial(_combine_kernel, tm=tm, nsteps=nsteps),
        grid_spec=pltpu.PrefetchScalarGridSpec(
            num_scalar_prefetch=1,
            grid=(nsteps,),
            in_specs=[
                pl.BlockSpec(memory_space=pl.ANY),
                pl.BlockSpec((tm, LANES), row),
                pl.BlockSpec((tm, D), row),
                pl.BlockSpec((1, D), fix), pl.BlockSpec((1, D), fix),
            ],
            out_specs=[pl.BlockSpec((tm, D), row), pl.BlockSpec((tm, D), row)],
            scratch_shapes=[pltpu.VMEM((2, 2, tm, D), F32), pltpu.SemaphoreType.DMA((2,))],
        ),
        out_shape=[jax.ShapeDtypeStruct((T, D), F32), jax.ShapeDtypeStruct((T, D), BF16)],
        compiler_params=_cparams(("arbitrary",), VMEM_LIMIT),
        name="moe_combine",
    )(dest_flat, y_rows, rw, h, g, b)


def _hier_moe_res_ln(h, wg, bg, we, be, w_gu, w_down, layer, g, b):
    T, D = h.shape
    wcat = jnp.zeros((D, LANES), F32).at[:, :N_GROUPS].set(wg).at[:, N_GROUPS:N_GROUPS + N_EXPERTS].set(we)
    bcat = jnp.zeros((1, LANES), F32).at[0, :N_GROUPS].set(bg).at[0, N_GROUPS:N_GROUPS + N_EXPERTS].set(be)
    ri, rw, cnt = _router(h, wcat, bcat)

    counts = cnt[0, :N_EXPERTS]
    padded = (counts + MOE_CHUNK - 1) // MOE_CHUNK * MOE_CHUNK
    pad_end = jnp.cumsum(padded)
    pad_start = pad_end - padded
    A = T * 2
    n_chunks = (A + N_EXPERTS * (MOE_CHUNK - 1) + MOE_CHUNK - 1) // MOE_CHUNK
    n_rows = n_chunks * MOE_CHUNK
    dest = pad_start[ri[:, 0:2]] + ri[:, 2:4]
    dest_flat = dest.reshape(-1).astype(I32)
    n_used = (pad_end[-1] // MOE_CHUNK).astype(I32)
    cidx = jnp.minimum(jnp.arange(n_chunks, dtype=I32), n_used - 1)
    chunk_e = jnp.sum((pad_end[None, :] <= (cidx * MOE_CHUNK)[:, None]).astype(I32), axis=1)
    chunk_e = jnp.minimum(chunk_e, N_EXPERTS - 1)
    zstart = (pad_start + counts).astype(I32)
    zcount = (padded - counts).astype(I32)

    n_used = n_used.reshape(1)
    xs = _dispatch(h, dest_flat, zstart, zcount, n_used, n_rows)
    ys = _moe_mlp(xs, chunk_e, n_used, w_gu, w_down, layer)
    return _combine_res_ln(ys, dest_flat, rw, h, g, b)


def _proj_heads_kernel(x_ref, w_ref, o_ref, *, nh, scale):
    acc = _dot(x_ref[...], w_ref[...])
    if scale != 1.0:
        acc = acc * scale
    for c in range(nh):
        o_ref[c] = acc[:, c * HEAD_DIM:(c + 1) * HEAD_DIM].astype(o_ref.dtype)


def _proj_heads(xb, w, out_dtype, scale=1.0, tm=512, tn=512):
    T, D = xb.shape
    N = w.shape[1]
    nh = tn // HEAD_DIM
    return pl.pallas_call(
        functools.partial(_proj_heads_kernel, nh=nh, scale=scale),
        grid=(N // tn, T // tm),
        in_specs=[
            pl.BlockSpec((tm, D), lambda j, i: (i, 0)),
            pl.BlockSpec((D, tn), lambda j, i: (0, j)),
        ],
        out_specs=pl.BlockSpec((nh, tm, HEAD_DIM), lambda j, i: (j, i, 0)),
        out_shape=jax.ShapeDtypeStruct((N // HEAD_DIM, T, HEAD_DIM), out_dtype),
        compiler_params=_cparams(("parallel", "parallel"), VMEM_LIMIT),
        name="proj_heads",
    )(xb, w)


def _gates_kernel(x_ref, w_ref, o_ref):
    o_ref[...] = _sigmoid(_dot(x_ref[...], w_ref[...]))


def _gates(xb, w, tm=512):
    T, D = xb.shape
    return pl.pallas_call(
        _gates_kernel,
        grid=(T // tm,),
        in_specs=[pl.BlockSpec((tm, D), lambda i: (i, 0)), pl.BlockSpec((D, LANES), lambda i: (0, 0))],
        out_specs=pl.BlockSpec((tm, LANES), lambda i: (i, 0)),
        out_shape=jax.ShapeDtypeStruct((T, LANES), F32),
        compiler_params=_cparams(("parallel",)),
        name="nsa_gates",
    )(xb, w)


def _cmp_mlp_kernel(x_ref, pos_ref, w1_ref, w2_ref, o_ref, *, nhb):
    x = x_ref[0, 0]
    half = x.shape[1]
    lo = (x + pos_ref[0, 0:1, :]).astype(BF16)
    hi = (x + pos_ref[0, 1:2, :]).astype(BF16)
    a_lo = _dot(lo, w1_ref[0, 0:half, :])
    a_hi = _dot(hi, w1_ref[0, half:2 * half, :])
    pre = a_lo + pltpu.roll(a_hi, nhb - 1, 0)
    hid = 0.5 * pre * (1.0 + jnp.tanh(0.7978845608028654 * (pre + 0.044715 * pre * pre * pre)))
    comp = _dot(hid.astype(BF16), w2_ref[0])
    row = lax.broadcasted_iota(I32, comp.shape, 0)
    o_ref[0, 0] = jnp.where(row < nhb - 1, comp, 0.0).astype(o_ref.dtype)


def _cmp_mlp(kvc, pos, w1, w2, B):
    R, T, dh = kvc.shape
    S = T // B
    nhb = S // CMP_STRIDE
    half = CMP_STRIDE * dh
    x4 = kvc.reshape(R, B, nhb, half)
    H = w1.shape[-1]
    return pl.pallas_call(
        functools.partial(_cmp_mlp_kernel, nhb=nhb),
        grid=(R, B),
        in_specs=[
            pl.BlockSpec((1, 1, nhb, half), lambda r, bb: (r, bb, 0, 0)),
            pl.BlockSpec((1, 2, half), lambda r, bb: (r // N_KV_GROUPS, 0, 0)),
            pl.BlockSpec((1, 2 * half, H), lambda r, bb: (r // N_KV_GROUPS, 0, 0)),
            pl.BlockSpec((1, H, dh), lambda r, bb: (r // N_KV_GROUPS, 0, 0)),
        ],
        out_specs=pl.BlockSpec((1, 1, nhb, dh), lambda r, bb: (r, bb, 0, 0)),
        out_shape=jax.ShapeDtypeStruct((R, B, nhb, dh), BF16),
        compiler_params=_cparams(("parallel", "parallel"), VMEM_LIMIT),
        name="cmp_mlp",
    )(x4, pos, w1, w2)


def _cmp_attn_kernel(q_ref, k_ref, v_ref, sl_ref, ov_ref, o_ref, sel_ref, *, tq, nhb, n_sel, topk):
    i = pl.program_id(2)
    q0 = i * tq
    qpos = q0 + lax.broadcasted_iota(I32, (tq, 1), 0)
    n = lax.broadcasted_iota(I32, (1, nhb), 1)
    dist = qpos - (n * CMP_STRIDE + (CMP_BLOCK - 1))
    valid = (dist >= 0) & (n < nhb - 1)
    distf = dist.astype(F32)
    k = k_ref[0, 0]
    v = v_ref[0, 0]
    psum = jnp.zeros((tq, nhb), F32)
    for h in range(HEADS_PER_GROUP):
        s = _dot_nt(q_ref[h], k) - sl_ref[0, h:h + 1, 0:1] * distf
        s = jnp.where(valid, s, MASK_VALUE)
        m = jnp.max(s, axis=-1, keepdims=True)
        e = jnp.where(valid, jnp.exp(s - m), 0.0)
        l = jnp.sum(e, axis=-1, keepdims=True)
        p = e / jnp.maximum(l, 1e-30)
        o_ref[h] = _dot(p.astype(BF16), v)
        psum = psum + p

    ov = ov_ref[...]
    p_hi = psum.astype(BF16)
    p_lo = (psum - p_hi.astype(F32)).astype(BF16)
    imp = _dot(p_hi, ov) + _dot(p_lo, ov)

    blk = lax.broadcasted_iota(I32, (tq, LANES), 1)
    blkf = blk.astype(F32)
    cur = lax.shift_right_logical(qpos, 6)
    in_range = (blk <= cur) & (blk < n_sel)
    forced = in_range & ((blk == 0) | (blk == cur) | (blk == cur - 1))
    n_forced = 1 + (cur >= 1).astype(I32) + (cur >= 2).astype(I32)
    k_free = topk - n_forced
    neg = jnp.float32(-jnp.inf)
    score = jnp.where(in_range & jnp.logical_not(forced), imp, neg)
    selected = forced
    for it in range(max(topk - 3, 0)):
        m = jnp.max(score, axis=-1, keepdims=True)
        idx = jnp.min(jnp.where(score == m, blkf, 1e9), axis=-1, keepdims=True)
        pick = (blkf == idx) & (m > neg) & (it < k_free)
        selected = selected | pick
        score = jnp.where(pick, neg, score)
    sel_ref[0] = jnp.where(selected, 1.0, 0.0).astype(sel_ref.dtype)


def _cmp_attn(q_t, kvcmp, slopes, ov, B, tq=256):
    NH, T, dh = q_t.shape
    S = T // B
    nq = S // tq
    nhb = kvcmp.shape[2]
    n_sel = S // SEL_BLOCK
    topk = min(SEL_TOPK, n_sel)
    return pl.pallas_call(
        functools.partial(_cmp_attn_kernel, tq=tq, nhb=nhb, n_sel=n_sel, topk=topk),
        grid=(B, N_KV_GROUPS, nq),
        in_specs=[
            pl.BlockSpec((HEADS_PER_GROUP, tq, dh), lambda bb, g, i: (g, bb * nq + i, 0)),
            pl.BlockSpec((1, 1, nhb, dh), lambda bb, g, i: (g, bb, 0, 0)),
            pl.BlockSpec((1, 1, nhb, dh), lambda bb, g, i: (N_KV_GROUPS + g, bb, 0, 0)),
            pl.BlockSpec((1, 8, LANES), lambda bb, g, i: (g, 0, 0)),
            pl.BlockSpec((nhb, LANES), lambda bb, g, i: (0, 0)),
        ],
        out_specs=[
            pl.BlockSpec((HEADS_PER_GROUP, tq, dh), lambda bb, g, i: (g, bb * nq + i, 0)),
            pl.BlockSpec((1, tq, LANES), lambda bb, g, i: (g, bb * nq + i, 0)),
        ],
        out_shape=[
            jax.ShapeDtypeStruct((NH, T, dh), F32),
            jax.ShapeDtypeStruct((N_KV_GROUPS, T, LANES), BF16),
        ],
        compiler_params=_cparams(("parallel", "parallel", "parallel")),
        name="cmp_attn",
    )(q_t, kvcmp, kvcmp, slopes, ov)


def _sel_attn_kernel(q_ref, k_ref, v_ref, sl_ref, sel_ref, e_ref, o_ref, m_scr, l_scr, acc_scr, *, tq, tk):
    i = pl.program_id(2)
    q0 = i * tq
    m_scr[...] = jnp.full_like(m_scr, MASK_VALUE)
    l_scr[...] = jnp.zeros_like(l_scr)
    acc_scr[...] = jnp.zeros_like(acc_scr)
    qpos = q0 + lax.broadcasted_iota(I32, (tq, 1), 0)
    selb = sel_ref[0]
    n_kv = (q0 + tq + tk - 1) // tk

    def kv_step(j, carry):
        k0 = pl.multiple_of(j * tk, tk)
        k = k_ref[0, pl.ds(k0, tk), :]
        v = v_ref[0, pl.ds(k0, tk), :]
        kpos = k0 + lax.broadcasted_iota(I32, (1, tk), 1)
        picked = _dot(selb, e_ref[j])
        valid = (kpos <= qpos) & (picked > 0.5)
        krel = (kpos - q0).astype(F32)
        for h in range(HEADS_PER_GROUP):
            s = _dot_nt(q_ref[h], k) + sl_ref[0, h:h + 1, 0:1] * krel
            s = jnp.where(valid, s, MASK_VALUE)
            m_prev = m_scr[h]
            m_new = jnp.maximum(m_prev, jnp.max(s, axis=-1, keepdims=True))
            alpha = jnp.exp(m_prev - m_new)
            p = jnp.exp(s - m_new)
            l_scr[h] = alpha * l_scr[h] + jnp.sum(p, axis=-1, keepdims=True)
            acc_scr[h] = alpha * acc_scr[h] + _dot(p.astype(BF16), v)
            m_scr[h] = m_new
        return carry

    lax.fori_loop(0, n_kv, kv_step, 0)
    for h in range(HEADS_PER_GROUP):
        o_ref[h] = acc_scr[h] / l_scr[h]


def _sel_attn(q_t, kv_t, slopes, sel, emat, B, tq=256, tk=512):
    NH, T, dh = q_t.shape
    S = T // B
    nq = S // tq
    qmap = lambda bb, g, i: (g, bb * nq + i, 0)
    return pl.pallas_call(
        functools.partial(_sel_attn_kernel, tq=tq, tk=tk),
        grid=(B, N_KV_GROUPS, nq),
        in_specs=[
            pl.BlockSpec((HEADS_PER_GROUP, tq, dh), qmap),
            pl.BlockSpec((1, S, dh), lambda bb, g, i: (g, bb, 0)),
            pl.BlockSpec((1, S, dh), lambda bb, g, i: (N_KV_GROUPS + g, bb, 0)),
            pl.BlockSpec((1, 8, LANES), lambda bb, g, i: (g, 0, 0)),
            pl.BlockSpec((1, tq, LANES), qmap),
            pl.BlockSpec((S // tk, LANES, tk), lambda bb, g, i: (0, 0, 0)),
        ],
        out_specs=pl.BlockSpec((HEADS_PER_GROUP, tq, dh), qmap),
        out_shape=jax.ShapeDtypeStruct((NH, T, dh), F32),
        scratch_shapes=[
            pltpu.VMEM((HEADS_PER_GROUP, tq, 1), F32),
            pltpu.VMEM((HEADS_PER_GROUP, tq, 1), F32),
            pltpu.VMEM((HEADS_PER_GROUP, tq, dh), F32),
        ],
        compiler_params=_cparams(("parallel", "parallel", "parallel")),
        name="nsa_sel_attn",
    )(q_t, kv_t, kv_t, slopes, sel, emat)


def _win_attn_kernel(q_ref, k_ref, v_ref, sl_ref, o_ref, *, tq):
    i = pl.program_id(2)
    q0 = i * tq
    kw = WINDOW + tq
    k0 = pl.multiple_of(jnp.maximum(q0 - WINDOW, 0), tq)
    k = k_ref[0, pl.ds(k0, kw), :]
    v = v_ref[0, pl.ds(k0, kw), :]
    qpos = q0 + lax.broadcasted_iota(I32, (tq, 1), 0)
    kpos = k0 + lax.broadcasted_iota(I32, (1, kw), 1)
    dist = qpos - kpos
    valid = (dist >= 0) & (dist < WINDOW)
    krel = (kpos - q0).astype(F32)
    for h in range(HEADS_PER_GROUP):
        s = _dot_nt(q_ref[h], k) + sl_ref[0, h:h + 1, 0:1] * krel
        s = jnp.where(valid, s, MASK_VALUE)
        m = jnp.max(s, axis=-1, keepdims=True)
        p = jnp.exp(s - m)
        l = jnp.sum(p, axis=-1, keepdims=True)
        o_ref[h] = _dot(p.astype(BF16), v) / l


def _win_attn(q_t, kv_t, slopes, B, tq=256):
    NH, T, dh = q_t.shape
    S = T // B
    assert S >= WINDOW + tq and WINDOW % tq == 0
    nq = S // tq
    qmap = lambda bb, g, i: (g, bb * nq + i, 0)
    return pl.pallas_call(
        functools.partial(_win_attn_kernel, tq=tq),
        grid=(B, N_KV_GROUPS, nq),
        in_specs=[
            pl.BlockSpec((HEADS_PER_GROUP, tq, dh), qmap),
            pl.BlockSpec((1, S, dh), lambda bb, g, i: (2 * N_KV_GROUPS + g, bb, 0)),
            pl.BlockSpec((1, S, dh), lambda bb, g, i: (3 * N_KV_GROUPS + g, bb, 0)),
            pl.BlockSpec((1, 8, LANES), lambda bb, g, i: (g, 0, 0)),
        ],
        out_specs=pl.BlockSpec((HEADS_PER_GROUP, tq, dh), qmap),
        out_shape=jax.ShapeDtypeStruct((NH, T, dh), F32),
        compiler_params=_cparams(("parallel", "parallel", "parallel")),
        name="nsa_win_attn",
    )(q_t, kv_t, kv_t, slopes)


def _attn_out_kernel(oc_ref, os_ref, ow_ref, gt_ref, x_ref, w_ref, g_ref, b_ref, of_ref, ob_ref, lhs):
    for hd in range(N_HEADS):
        g0 = gt_ref[:, N_BRANCHES * hd:N_BRANCHES * hd + 1]
        g1 = gt_ref[:, N_BRANCHES * hd + 1:N_BRANCHES * hd + 2]
        g2 = gt_ref[:, N_BRANCHES * hd + 2:N_BRANCHES * hd + 3]
        o = g0 * oc_ref[hd] + g1 * os_ref[hd] + g2 * ow_ref[hd]
        lhs[:, hd * HEAD_DIM:(hd + 1) * HEAD_DIM] = o.astype(BF16)
    mix = _dot(lhs[...], w_ref[...])
    h = _layer_norm(DEEPNORM_ALPHA * x_ref[...] + mix, g_ref[...], b_ref[...])
    of_ref[...] = h
    ob_ref[...] = h.astype(BF16)


def _attn_out_res_ln(o_cmp, o_sel, o_win, gates, x, w, g, b, tm=256):
    T, D = x.shape
    NH = o_cmp.shape[0]
    hmap = lambda i: (0, i, 0)
    row = lambda i: (i, 0)
    fix = lambda i: (0, 0)
    return pl.pallas_call(
        _attn_out_kernel,
        grid=(T // tm,),
        in_specs=[
            pl.BlockSpec((NH, tm, HEAD_DIM), hmap), pl.BlockSpec((NH, tm, HEAD_DIM), hmap),
            pl.BlockSpec((NH, tm, HEAD_DIM), hmap),
            pl.BlockSpec((tm, LANES), row), pl.BlockSpec((tm, D), row),
            pl.BlockSpec((D, D), fix),
            pl.BlockSpec((1, D), fix), pl.BlockSpec((1, D), fix),
        ],
        out_specs=[pl.BlockSpec((tm, D), row), pl.BlockSpec((tm, D), row)],
        out_shape=[jax.ShapeDtypeStruct((T, D), F32), jax.ShapeDtypeStruct((T, D), BF16)],
        scratch_shapes=[pltpu.VMEM((tm, D), BF16)],
        compiler_params=_cparams(("parallel",), VMEM_LIMIT),
        name="attn_out_res_ln",
    )(o_cmp, o_sel, o_win, gates, x, w, g, b)


def _nsa_tables(S, sel_tk):
    nhb = S // CMP_STRIDE
    n_sel = S // SEL_BLOCK
    heads = np.arange(1, N_HEADS + 1, dtype=np.float32)
    slopes = np.zeros((N_KV_GROUPS, 8, LANES), np.float32)
    slopes[:, :HEADS_PER_GROUP, :] = (2.0 ** (-8.0 * heads / N_HEADS)).reshape(N_KV_GROUPS, HEADS_PER_GROUP, 1)
    c0 = np.arange(nhb)[:, None] * CMP_STRIDE
    j0 = np.arange(LANES)[None, :] * SEL_BLOCK
    ov = np.maximum(np.minimum(c0 + CMP_BLOCK, j0 + SEL_BLOCK) - np.maximum(c0, j0), 0).astype(np.float32) / CMP_BLOCK
    ov[nhb - 1, :] = 0.0
    ov[:, n_sel:] = 0.0
    emat = (np.arange(S)[None, :] // SEL_BLOCK == np.arange(LANES)[:, None]).astype(np.float32)
    emat = emat.reshape(LANES, S // sel_tk, sel_tk).transpose(1, 0, 2)
    return jnp.asarray(slopes), jnp.asarray(ov, dtype=BF16), jnp.asarray(emat, dtype=BF16)


def _nsa_layer_res_ln(hf, hb, B, kv_w, cmp_pos, cmp_w1, cmp_w2, w_qg, w_o, g, b):
    T, D = hf.shape
    S = T // B
    hd = N_HEADS * HEAD_DIM
    gsz = N_KV_GROUPS * HEAD_DIM
    sel_tk = 512
    slopes, ov, emat = _nsa_tables(S, sel_tk)

    kvw = kv_w.astype(BF16)
    kvc = _proj_heads(hb, kvw[:, :2 * gsz], F32)
    kv_t = _proj_heads(hb, kvw[:, 2 * gsz:], BF16)
    wq = w_qg[:, :hd].astype(BF16)
    wgt = jnp.zeros((D, LANES), F32).at[:, :N_BRANCHES * N_HEADS].set(w_qg[:, hd:]).astype(BF16)
    q_t = _proj_heads(hb, wq, BF16, scale=HEAD_DIM ** -0.5)
    gates = _gates(hb, wgt)

    pos = cmp_pos.reshape(2, 2, CMP_STRIDE * HEAD_DIM)
    kvcmp = _cmp_mlp(kvc, pos, cmp_w1.astype(BF16), cmp_w2.astype(BF16), B)
    o_cmp, sel = _cmp_attn(q_t, kvcmp, slopes, ov, B)
    o_sel = _sel_attn(q_t, kv_t, slopes, sel, emat, B, tk=sel_tk)
    o_win = _win_attn(q_t, kv_t, slopes, B)
    return _attn_out_res_ln(o_cmp, o_sel, o_win, gates, hf, w_o.astype(BF16), g, b)


def kernel(x, conv_w_pw1, conv_b_pw1, conv_w_dw, conv_b_dw, conv_ln_g, conv_ln_b, conv_w_pw2, kv_w, cmp_pos, cmp_w1, cmp_w2, nsa_w_qg, nsa_w_o, moe_wg, moe_bg, moe_we, moe_be, moe_w_gu, moe_w_down, ln_g, ln_b):
    B, S, D = x.shape
    T = B * S
    hf = x.reshape(T, D)
    hb = hf.astype(BF16)
    r = lambda v: v.reshape(1, -1)
    for l in range(DEPTH):
        if l < N_A_LAYERS:
            glu = _pw1_glu(hb, conv_w_pw1[l].astype(BF16), r(conv_b_pw1[l]))
            conv = _dwconv(glu.reshape(B, S, D), conv_w_dw[l], r(conv_b_dw[l])).reshape(T, D)
            hf, hb = _pw2_res_ln(conv, hf, conv_w_pw2[l].astype(BF16), r(conv_ln_g[l]), r(conv_ln_b[l]),
                                 r(ln_g[l, 0]), r(ln_b[l, 0]))
        else:
            j = l - N_A_LAYERS
            hf, hb = _nsa_layer_res_ln(hf, hb, B, kv_w, cmp_pos, cmp_w1, cmp_w2, nsa_w_qg[j], nsa_w_o[j],
                                       r(ln_g[l, 0]), r(ln_b[l, 0]))
        hf, hb = _hier_moe_res_ln(hf, moe_wg[l], moe_bg[l], moe_we[l], moe_be[l], moe_w_gu, moe_w_down, l,
                                  r(ln_g[l, 1]), r(ln_b[l, 1]))
    return hf.reshape(B, S, D)
```

```python
import functools

import numpy as np
import jax
import jax.numpy as jnp
from jax import lax
from jax.experimental import pallas as pl
from jax.experimental.pallas import tpu as pltpu

F32 = jnp.float32
BF16 = jnp.bfloat16
I32 = jnp.int32

DEPTH = 2
N_A_LAYERS = DEPTH // 2
DEEPNORM_ALPHA = (2.0 * DEPTH) ** 0.25
LN_EPS = 1e-5
CONV_WIDTH = 31
N_HEADS = 16
HEAD_DIM = 128
N_KV_GROUPS = 4
HEADS_PER_GROUP = N_HEADS // N_KV_GROUPS
CMP_BLOCK = 32
CMP_STRIDE = 16
SEL_BLOCK = 64
SEL_TOPK = 16
WINDOW = 512
N_BRANCHES = 3
MASK_VALUE = -1e30
N_GROUPS = 4
EXPERTS_PER_GROUP = 8
N_EXPERTS = N_GROUPS * EXPERTS_PER_GROUP
MOE_CHUNK = 128

LANES = 128
SUBLANES = 8
VMEM_LIMIT = 56 * 1024 * 1024

CONV_HALO = 32
SEL_ROWS = 64
SEL_KEY_TILE = 512


def _cparams(sem, vmem=None):
    return pltpu.CompilerParams(dimension_semantics=sem, vmem_limit_bytes=vmem)


def _sigmoid(x):
    return 1.0 / (1.0 + jnp.exp(-x))


def _layer_norm(x, g, b):
    mu = jnp.mean(x, axis=-1, keepdims=True)
    xc = x - mu
    var = jnp.mean(xc * xc, axis=-1, keepdims=True)
    return xc * lax.rsqrt(var + LN_EPS) * g + b


def _dot(a, b):
    return jnp.dot(a, b, preferred_element_type=F32)


def _dot_nt(a, b):
    return lax.dot_general(a, b, (((1,), (1,)), ((), ())), preferred_element_type=F32)


def _pw1_glu_kernel(x_ref, wa_ref, wg_ref, ba_ref, bg_ref, o_ref):
    x = x_ref[...]
    a = _dot(x, wa_ref[...]) + ba_ref[...]
    g = _dot(x, wg_ref[...]) + bg_ref[...]
    o_ref[...] = a * _sigmoid(g)


def _pw1_glu(xb, w, b, tm=512, tn=512):
    T, D = xb.shape
    nj = D // tn
    return pl.pallas_call(
        _pw1_glu_kernel,
        grid=(nj, T // tm),
        in_specs=[
            pl.BlockSpec((tm, D), lambda j, i: (i, 0)),
            pl.BlockSpec((D, tn), lambda j, i: (0, j)),
            pl.BlockSpec((D, tn), lambda j, i: (0, j + nj)),
            pl.BlockSpec((1, tn), lambda j, i: (0, j)),
            pl.BlockSpec((1, tn), lambda j, i: (0, j + nj)),
        ],
        out_specs=pl.BlockSpec((tm, tn), lambda j, i: (i, j)),
        out_shape=jax.ShapeDtypeStruct((T, D), F32),
        compiler_params=_cparams(("parallel", "parallel"), VMEM_LIMIT),
        name="pw1_glu",
    )(xb, w, w, b, b)


def _dwconv_kernel(x_ref, halo_ref, w_ref, b_ref, o_ref, xsh, *, ts, tc, rc):
    i = pl.program_id(1)
    xsh[0, 0:CONV_HALO, :] = jnp.where(i > 0, halo_ref[0], 0.0)
    xsh[0, CONV_HALO:CONV_HALO + ts, :] = x_ref[0]
    span = ts + CONV_HALO - SUBLANES
    for p in range(1, SUBLANES):
        xsh[p, 0:span, :] = xsh[0, p:p + span, :]
    off = CONV_HALO - (CONV_WIDTH - 1)
    for c0 in range(0, tc, LANES):
        for r0 in range(0, ts, rc):
            acc = jnp.broadcast_to(b_ref[:, c0:c0 + LANES], (rc, LANES))
            for k in range(CONV_WIDTH):
                p = (k + off) % SUBLANES
                base = r0 + k + off - p
                acc = acc + w_ref[k:k + 1, c0:c0 + LANES] * xsh[p, base:base + rc, c0:c0 + LANES]
            o_ref[0, r0:r0 + rc, c0:c0 + LANES] = acc


def _dwconv(x, w, b, ts=256, tc=256, rc=128):
    B, S, D = x.shape
    hb = ts // CONV_HALO
    return pl.pallas_call(
        functools.partial(_dwconv_kernel, ts=ts, tc=tc, rc=rc),
        grid=(B, S // ts, D // tc),
        in_specs=[
            pl.BlockSpec((1, ts, tc), lambda bb, i, c: (bb, i, c)),
            pl.BlockSpec((1, CONV_HALO, tc), lambda bb, i, c: (bb, jnp.maximum(i * hb - 1, 0), c)),
            pl.BlockSpec((CONV_WIDTH, tc), lambda bb, i, c: (0, c)),
            pl.BlockSpec((1, tc), lambda bb, i, c: (0, c)),
        ],
        out_specs=pl.BlockSpec((1, ts, tc), lambda bb, i, c: (bb, i, c)),
        out_shape=jax.ShapeDtypeStruct((B, S, D), F32),
        scratch_shapes=[pltpu.VMEM((SUBLANES, CONV_HALO + ts, tc), F32)],
        compiler_params=_cparams(("parallel", "parallel", "parallel")),
        name="dwconv",
    )(x, x, w, b)


def _pw2_kernel(c_ref, x_ref, w_ref, cg_ref, cb_ref, g_ref, b_ref, of_ref, ob_ref):
    y = _layer_norm(c_ref[...], cg_ref[...], cb_ref[...])
    y = y * _sigmoid(y)
    mix = _dot(y.astype(BF16), w_ref[...])
    h = _layer_norm(DEEPNORM_ALPHA * x_ref[...] + mix, g_ref[...], b_ref[...])
    of_ref[...] = h
    ob_ref[...] = h.astype(BF16)


def _pw2_res_ln(conv, x, w, cg, cb, g, b, tm=256):
    T, D = x.shape
    row = lambda i: (i, 0)
    fix = lambda i: (0, 0)
    return pl.pallas_call(
        _pw2_kernel,
        grid=(T // tm,),
        in_specs=[
            pl.BlockSpec((tm, D), row), pl.BlockSpec((tm, D), row),
            pl.BlockSpec((D, D), fix),
            pl.BlockSpec((1, D), fix), pl.BlockSpec((1, D), fix),
            pl.BlockSpec((1, D), fix), pl.BlockSpec((1, D), fix),
        ],
        out_specs=[pl.BlockSpec((tm, D), row), pl.BlockSpec((tm, D), row)],
        out_shape=[jax.ShapeDtypeStruct((T, D), F32), jax.ShapeDtypeStruct((T, D), BF16)],
        compiler_params=_cparams(("parallel",), VMEM_LIMIT),
        name="pw2_res_ln",
    )(conv, x, w, cg, cb, g, b)


def _router_kernel(h_ref, w_ref, b_ref, ri_ref, rw_ref, cnt_ref, *, tm):
    s = pl.program_id(0)

    @pl.when(s == 0)
    def _():
        cnt_ref[...] = jnp.zeros_like(cnt_ref)

    h = h_ref[...]
    w = w_ref[...]
    h_hi = h.astype(BF16)
    h_lo = (h - h_hi.astype(F32)).astype(BF16)
    w_hi = w.astype(BF16)
    w_lo = (w - w_hi.astype(F32)).astype(BF16)
    lg = _dot(h_hi, w_hi) + _dot(h_lo, w_hi) + _dot(h_hi, w_lo) + b_ref[...]

    lane = lax.broadcasted_iota(I32, (tm, LANES), 1)
    lanef = lane.astype(F32)
    neg = jnp.float32(-jnp.inf)
    big = jnp.float32(1e9)

    is_g = lane < N_GROUPS
    lgm = jnp.where(is_g, lg, neg)
    mg = jnp.max(lgm, axis=-1, keepdims=True)
    grp = jnp.min(jnp.where(lgm == mg, lanef, big), axis=-1, keepdims=True)
    sg = jnp.sum(jnp.where(is_g, jnp.exp(lg - mg), 0.0), axis=-1, keepdims=True)
    gw = 1.0 / sg

    lo = N_GROUPS + grp * EXPERTS_PER_GROUP
    in_g = (lanef >= lo) & (lanef < lo + EXPERTS_PER_GROUP)
    le = jnp.where(in_g, lg, neg)
    m1 = jnp.max(le, axis=-1, keepdims=True)
    i1 = jnp.min(jnp.where(le == m1, lanef, big), axis=-1, keepdims=True)
    le2 = jnp.where(lanef == i1, neg, le)
    m2 = jnp.max(le2, axis=-1, keepdims=True)
    i2 = jnp.min(jnp.where(le2 == m2, lanef, big), axis=-1, keepdims=True)
    t = jnp.exp(m2 - m1)
    p1 = 1.0 / (1.0 + t)
    w1 = p1 * gw
    w2 = (t * p1) * gw
    e1 = i1 - N_GROUPS
    e2 = i2 - N_GROUPS

    oh1 = (lanef == e1)
    oh2 = (lanef == e2)
    oh = (oh1 | oh2).astype(F32)
    r_i = lax.broadcasted_iota(I32, (tm, tm), 0)
    c_i = lax.broadcasted_iota(I32, (tm, tm), 1)
    tri = (c_i < r_i).astype(BF16)
    cs = _dot(tri, oh.astype(BF16)) + cnt_ref[...].astype(F32)
    rank1 = jnp.sum(jnp.where(oh1, cs, 0.0), axis=-1, keepdims=True)
    rank2 = jnp.sum(jnp.where(oh2, cs, 0.0), axis=-1, keepdims=True)
    cnt_ref[...] = cnt_ref[...] + jnp.sum(oh, axis=0, keepdims=True).astype(I32)

    ri = jnp.where(lane == 0, e1, jnp.where(lane == 1, e2, jnp.where(lane == 2, rank1, jnp.where(lane == 3, rank2, 0.0))))
    ri_ref[...] = ri.astype(I32)
    rw_ref[...] = jnp.where(lane == 0, w1, jnp.where(lane == 1, w2, 0.0))


def _router(h, wcat, bcat, tm=256):
    T, D = h.shape
    return pl.pallas_call(
        functools.partial(_router_kernel, tm=tm),
        grid=(T // tm,),
        in_specs=[
            pl.BlockSpec((tm, D), lambda i: (i, 0)),
            pl.BlockSpec((D, LANES), lambda i: (0, 0)),
            pl.BlockSpec((1, LANES), lambda i: (0, 0)),
        ],
        out_specs=[
            pl.BlockSpec((tm, LANES), lambda i: (i, 0)),
            pl.BlockSpec((tm, LANES), lambda i: (i, 0)),
            pl.BlockSpec((1, LANES), lambda i: (0, 0)),
        ],
        out_shape=[
            jax.ShapeDtypeStruct((T, LANES), I32),
            jax.ShapeDtypeStruct((T, LANES), F32),
            jax.ShapeDtypeStruct((1, LANES), I32),
        ],
        compiler_params=_cparams(("arbitrary",)),
        name="moe_router",
    )(h, wcat, bcat)


def _dispatch_kernel(dest_ref, zs_ref, zc_ref, nu_ref, x_ref, xs_hbm, zblk, sem, *, tm, n_chunks):
    s = pl.program_id(0)

    def row_copy(r, d):
        return pltpu.make_async_copy(x_ref.at[pl.ds(r, 1)], xs_hbm.at[pl.ds(d, 1)], sem)

    def zero_row_copy(d):
        return pltpu.make_async_copy(zblk.at[pl.ds(0, 1)], xs_hbm.at[pl.ds(d, 1)], sem)

    def zero_chunk_copy(c):
        return pltpu.make_async_copy(zblk, xs_hbm.at[pl.ds(pl.multiple_of(c * MOE_CHUNK, MOE_CHUNK), MOE_CHUNK)], sem)

    @pl.when(s == 0)
    def _():
        zblk[...] = jnp.zeros_like(zblk)

        def fill(start):
            def per_expert(e, c):
                def per_row(r, c2):
                    cp = zero_row_copy(zs_ref[e] + r)
                    if start:
                        cp.start()
                    else:
                        cp.wait()
                    return c2
                lax.fori_loop(0, zc_ref[e], per_row, 0)
                return c
            lax.fori_loop(0, N_EXPERTS, per_expert, 0)

            def per_chunk(c, c2):
                cp = zero_chunk_copy(c)
                if start:
                    cp.start()
                else:
                    cp.wait()
                return c2
            lax.fori_loop(nu_ref[0], n_chunks, per_chunk, 0)

        fill(True)
        fill(False)

    base = s * tm

    def issue(r, c):
        t = base + r
        row_copy(r, dest_ref[2 * t]).start(priority=0)
        row_copy(r, dest_ref[2 * t + 1]).start(priority=1)
        return c

    lax.fori_loop(0, tm, issue, 0, unroll=8)

    def drain(r, c):
        row_copy(0, 0).wait()
        row_copy(0, 0).wait()
        return c

    lax.fori_loop(0, tm, drain, 0, unroll=8)


def _dispatch(x, dest_flat, zstart, zcount, n_used, n_rows, tm=256):
    T, D = x.shape
    return pl.pallas_call(
        functools.partial(_dispatch_kernel, tm=tm, n_chunks=n_rows // MOE_CHUNK),
        grid_spec=pltpu.PrefetchScalarGridSpec(
            num_scalar_prefetch=4,
            grid=(T // tm,),
            in_specs=[pl.BlockSpec((tm, D), lambda i, *_: (i, 0))],
            out_specs=pl.BlockSpec(memory_space=pl.ANY),
            scratch_shapes=[pltpu.VMEM((MOE_CHUNK, D), F32), pltpu.SemaphoreType.DMA(())],
        ),
        out_shape=jax.ShapeDtypeStruct((n_rows, D), F32),
        compiler_params=_cparams(("arbitrary",)),
        name="moe_dispatch",
    )(dest_flat, zstart, zcount, n_used, x)


def _moe_mlp_kernel(ce_ref, nu_ref, first_ref, nxt_ref, x_ref, wgu_hbm, wd_hbm, o_ref,
                    wgu_stage, wd_stage, wgu_bf, wd_bf, sem, *, F, layer):
    c = pl.program_id(0)

    def w_copies(e):
        return (pltpu.make_async_copy(wgu_hbm.at[layer, e], wgu_stage, sem.at[0]),
                pltpu.make_async_copy(wd_hbm.at[layer, e], wd_stage, sem.at[1]))

    @pl.when(c < nu_ref[0])
    def _():
        @pl.when(first_ref[c] == 1)
        def _():
            @pl.when(c == 0)
            def _():
                for cp in w_copies(ce_ref[0]):
                    cp.start()

            for cp in w_copies(ce_ref[c]):
                cp.wait()
            wgu_bf[...] = wgu_stage[...].astype(BF16)
            wd_bf[...] = wd_stage[...].astype(BF16)

            @pl.when(nxt_ref[c] >= 0)
            def _():
                for cp in w_copies(nxt_ref[c]):
                    cp.start()

        x = x_ref[...].astype(BF16)
        gu = _dot(x, wgu_bf[...])
        g = gu[:, :F]
        u = gu[:, F:]
        hmid = (g * _sigmoid(g)) * u
        o_ref[...] = _dot(hmid.astype(BF16), wd_bf[...])

    @pl.when(c >= nu_ref[0])
    def _():
        o_ref[...] = jnp.zeros_like(o_ref)


def _moe_mlp(xs, chunk_e, n_used, first, nxt, w_gu, w_down, layer):
    n_rows, D = xs.shape
    F2 = w_gu.shape[-1]
    F = F2 // 2
    n_chunks = n_rows // MOE_CHUNK

    def x_map(c, ce, nu, fi, nx):
        return (jnp.minimum(c, nu[0] - 1), 0)

    return pl.pallas_call(
        functools.partial(_moe_mlp_kernel, F=F, layer=layer),
        grid_spec=pltpu.PrefetchScalarGridSpec(
            num_scalar_prefetch=4,
            grid=(n_chunks,),
            in_specs=[
                pl.BlockSpec((MOE_CHUNK, D), x_map),
                pl.BlockSpec(memory_space=pl.ANY),
                pl.BlockSpec(memory_space=pl.ANY),
            ],
            out_specs=pl.BlockSpec((MOE_CHUNK, D), lambda c, ce, nu, fi, nx: (c, 0)),
            scratch_shapes=[
                pltpu.VMEM((D, F2), F32), pltpu.VMEM((F, D), F32),
                pltpu.VMEM((D, F2), BF16), pltpu.VMEM((F, D), BF16),
                pltpu.SemaphoreType.DMA((2,)),
            ],
        ),
        out_shape=jax.ShapeDtypeStruct((n_rows, D), F32),
        compiler_params=_cparams(("arbitrary",), VMEM_LIMIT),
        name="moe_mlp",
    )(chunk_e, n_used, first, nxt, xs, w_gu, w_down)


def _combine_kernel(dest_ref, y_hbm, rw_ref, h_ref, g_ref, b_ref, of_ref, ob_ref, gbuf, sem, *, tm, nsteps):
    s = pl.program_id(0)

    def row_copy(src_row, slot, k, r):
        return pltpu.make_async_copy(y_hbm.at[pl.ds(src_row, 1)], gbuf.at[slot, k, pl.ds(r, 1)], sem.at[slot])

    def issue(step, slot):
        base = step * tm

        def body(r, c):
            t = base + r
            row_copy(dest_ref[2 * t], slot, 0, r).start(priority=0)
            row_copy(dest_ref[2 * t + 1], slot, 1, r).start(priority=1)
            return c

        lax.fori_loop(0, tm, body, 0, unroll=8)

    @pl.when(s == 0)
    def _():
        issue(0, 0)

    @pl.when(s + 1 < nsteps)
    def _():
        issue(s + 1, (s + 1) % 2)

    slot = s % 2

    def drain(r, c):
        row_copy(0, slot, 0, 0).wait()
        row_copy(0, slot, 1, 0).wait()
        return c

    lax.fori_loop(0, tm, drain, 0, unroll=8)

    w0 = rw_ref[:, 0:1]
    w1 = rw_ref[:, 1:2]
    y = w0 * gbuf[slot, 0] + w1 * gbuf[slot, 1]
    h = _layer_norm(DEEPNORM_ALPHA * h_ref[...] + y, g_ref[...], b_ref[...])
    of_ref[...] = h
    ob_ref[...] = h.astype(BF16)


def _combine_res_ln(y_rows, dest_flat, rw, h, g, b, tm=256):
    T, D = h.shape
    nsteps = T // tm
    row = lambda i, d: (i, 0)
    fix = lambda i, d: (0, 0)
    return pl.pallas_call(
        functools.partial(_combine_kernel, tm=tm, nsteps=nsteps),
        grid_spec=pltpu.PrefetchScalarGridSpec(
            num_scalar_prefetch=1,
            grid=(nsteps,),
            in_specs=[
                pl.BlockSpec(memory_space=pl.ANY),
                pl.BlockSpec((tm, LANES), row),
                pl.BlockSpec((tm, D), row),
                pl.BlockSpec((1, D), fix), pl.BlockSpec((1, D), fix),
            ],
            out_specs=[pl.BlockSpec((tm, D), row), pl.BlockSpec((tm, D), row)],
            scratch_shapes=[pltpu.VMEM((2, 2, tm, D), F32), pltpu.SemaphoreType.DMA((2,))],
        ),
        out_shape=[jax.ShapeDtypeStruct((T, D), F32), jax.ShapeDtypeStruct((T, D), BF16)],
        compiler_params=_cparams(("arbitrary",), VMEM_LIMIT),
        name="moe_combine",
    )(dest_flat, y_rows, rw, h, g, b)


def _hier_moe_res_ln(h, wg, bg, we, be, w_gu, w_down, layer, g, b):
    T, D = h.shape
    wcat = jnp.zeros((D, LANES), F32).at[:, :N_GROUPS].set(wg).at[:, N_GROUPS:N_GROUPS + N_EXPERTS].set(we)
    bcat = jnp.zeros((1, LANES), F32).at[0, :N_GROUPS].set(bg).at[0, N_GROUPS:N_GROUPS + N_EXPERTS].set(be)
    ri, rw, cnt = _router(h, wcat, bcat)

    counts = cnt[0, :N_EXPERTS]
    padded = (counts + MOE_CHUNK - 1) // MOE_CHUNK * MOE_CHUNK
    pad_end = jnp.cumsum(padded)
    pad_start = pad_end - padded
    A = T * 2
    n_chunks = (A + N_EXPERTS * (MOE_CHUNK - 1) + MOE_CHUNK - 1) // MOE_CHUNK
    n_rows = n_chunks * MOE_CHUNK
    dest = pad_start[ri[:, 0:2]] + ri[:, 2:4]
    dest_flat = dest.reshape(-1).astype(I32)
    n_used = (pad_end[-1] // MOE_CHUNK).astype(I32)
    cidx = jnp.minimum(jnp.arange(n_chunks, dtype=I32), n_used - 1)
    chunk_e = jnp.sum((pad_end[None, :] <= (cidx * MOE_CHUNK)[:, None]).astype(I32), axis=1)
    chunk_e = jnp.minimum(chunk_e, N_EXPERTS - 1)
    zstart = (pad_start + counts).astype(I32)
    zcount = (padded - counts).astype(I32)
    carange = jnp.arange(n_chunks, dtype=I32)
    prev_e = jnp.concatenate([jnp.full((1,), -1, I32), chunk_e[:-1]])
    first = ((chunk_e != prev_e) & (carange < n_used)).astype(I32)
    run_end = (pad_end[chunk_e] // MOE_CHUNK).astype(I32)
    nxt = jnp.where(run_end < n_used, chunk_e[jnp.minimum(run_end, n_chunks - 1)], -1).astype(I32)

    n_used = n_used.reshape(1)
    xs = _dispatch(h, dest_flat, zstart, zcount, n_used, n_rows)
    ys = _moe_mlp(xs, chunk_e, n_used, first, nxt, w_gu, w_down, layer)
    return _combine_res_ln(ys, dest_flat, rw, h, g, b)


def _proj_heads_kernel(x_ref, w_ref, o_ref, *, nh, scale):
    acc = _dot(x_ref[...], w_ref[...])
    if scale != 1.0:
        acc = acc * scale
    for c in range(nh):
        o_ref[c] = acc[:, c * HEAD_DIM:(c + 1) * HEAD_DIM].astype(o_ref.dtype)


def _proj_heads(xb, w, out_dtype, scale=1.0, tm=512, tn=512):
    T, D = xb.shape
    N = w.shape[1]
    nh = tn // HEAD_DIM
    return pl.pallas_call(
        functools.partial(_proj_heads_kernel, nh=nh, scale=scale),
        grid=(N // tn, T // tm),
        in_specs=[
            pl.BlockSpec((tm, D), lambda j, i: (i, 0)),
            pl.BlockSpec((D, tn), lambda j, i: (0, j)),
        ],
        out_specs=pl.BlockSpec((nh, tm, HEAD_DIM), lambda j, i: (j, i, 0)),
        out_shape=jax.ShapeDtypeStruct((N // HEAD_DIM, T, HEAD_DIM), out_dtype),
        compiler_params=_cparams(("parallel", "parallel"), VMEM_LIMIT),
        name="proj_heads",
    )(xb, w)


def _gates_kernel(x_ref, w_ref, o_ref):
    o_ref[...] = _sigmoid(_dot(x_ref[...], w_ref[...]))


def _gates(xb, w, tm=512):
    T, D = xb.shape
    return pl.pallas_call(
        _gates_kernel,
        grid=(T // tm,),
        in_specs=[pl.BlockSpec((tm, D), lambda i: (i, 0)), pl.BlockSpec((D, LANES), lambda i: (0, 0))],
        out_specs=pl.BlockSpec((tm, LANES), lambda i: (i, 0)),
        out_shape=jax.ShapeDtypeStruct((T, LANES), F32),
        compiler_params=_cparams(("parallel",)),
        name="nsa_gates",
    )(xb, w)


def _cmp_mlp_kernel(x_ref, pos_ref, w1_ref, w2_ref, o_ref, *, nhb):
    x = x_ref[0, 0]
    half = x.shape[1]
    lo = (x + pos_ref[0, 0:1, :]).astype(BF16)
    hi = (x + pos_ref[0, 1:2, :]).astype(BF16)
    a_lo = _dot(lo, w1_ref[0, 0:half, :])
    a_hi = _dot(hi, w1_ref[0, half:2 * half, :])
    pre = a_lo + pltpu.roll(a_hi, nhb - 1, 0)
    hid = 0.5 * pre * (1.0 + jnp.tanh(0.7978845608028654 * (pre + 0.044715 * pre * pre * pre)))
    comp = _dot(hid.astype(BF16), w2_ref[0])
    row = lax.broadcasted_iota(I32, comp.shape, 0)
    o_ref[0, 0] = jnp.where(row < nhb - 1, comp, 0.0).astype(o_ref.dtype)


def _cmp_mlp(kvc, pos, w1, w2, B):
    R, T, dh = kvc.shape
    S = T // B
    nhb = S // CMP_STRIDE
    half = CMP_STRIDE * dh
    x4 = kvc.reshape(R, B, nhb, half)
    H = w1.shape[-1]
    return pl.pallas_call(
        functools.partial(_cmp_mlp_kernel, nhb=nhb),
        grid=(R, B),
        in_specs=[
            pl.BlockSpec((1, 1, nhb, half), lambda r, bb: (r, bb, 0, 0)),
            pl.BlockSpec((1, 2, half), lambda r, bb: (r // N_KV_GROUPS, 0, 0)),
            pl.BlockSpec((1, 2 * half, H), lambda r, bb: (r // N_KV_GROUPS, 0, 0)),
            pl.BlockSpec((1, H, dh), lambda r, bb: (r // N_KV_GROUPS, 0, 0)),
        ],
        out_specs=pl.BlockSpec((1, 1, nhb, dh), lambda r, bb: (r, bb, 0, 0)),
        out_shape=jax.ShapeDtypeStruct((R, B, nhb, dh), BF16),
        compiler_params=_cparams(("parallel", "parallel"), VMEM_LIMIT),
        name="cmp_mlp",
    )(x4, pos, w1, w2)


def _cmp_attn_kernel(q_ref, k_ref, v_ref, sl_ref, ovt_ref, qf_ref, o_ref, qa_ref, *, tq, nhb, n_sel, topk):
    i = pl.program_id(2)
    q0 = i * tq
    qpos = q0 + lax.broadcasted_iota(I32, (tq, 1), 0)
    n = lax.broadcasted_iota(I32, (1, nhb), 1)
    dist = qpos - (n * CMP_STRIDE + (CMP_BLOCK - 1))
    valid = (dist >= 0) & (n < nhb - 1)
    distf = dist.astype(F32)
    k = k_ref[0, 0]
    v = v_ref[0, 0]
    psum = jnp.zeros((tq, nhb), F32)
    for h in range(HEADS_PER_GROUP):
        s = _dot_nt(q_ref[h], k) - sl_ref[0, h:h + 1, 0:1] * distf
        s = jnp.where(valid, s, MASK_VALUE)
        m = jnp.max(s, axis=-1, keepdims=True)
        e = jnp.where(valid, jnp.exp(s - m), 0.0)
        l = jnp.sum(e, axis=-1, keepdims=True)
        p = e / jnp.maximum(l, 1e-30)
        o_ref[h] = _dot(p.astype(BF16), v)
        psum = psum + p

    ovt = ovt_ref[...]
    p_hi = psum.astype(BF16)
    p_lo = (psum - p_hi.astype(F32)).astype(BF16)
    imp_t = _dot_nt(ovt, p_hi) + _dot_nt(ovt, p_lo)

    blk = lax.broadcasted_iota(I32, (SEL_ROWS, tq), 0)
    cur = lax.shift_right_logical(q0 + lax.broadcasted_iota(I32, (1, tq), 1), 6)
    in_range = (blk <= cur) & (blk < n_sel)
    forced = in_range & ((blk == 0) | (blk == cur) | (blk == cur - 1))
    k_free = (topk - 1 - (cur >= 1).astype(I32) - (cur >= 2).astype(I32)).astype(F32)
    neg = jnp.float32(-jnp.inf)
    x = jnp.where(in_range & jnp.logical_not(forced), imp_t, neg)
    n_grp = (n_sel + 7) // 8
    xg = [x[8 * r:8 * r + 8, :] for r in range(n_grp)]
    rank_g = [jnp.zeros((8, tq), F32) for _ in range(n_grp)]
    sub = lax.broadcasted_iota(I32, (8, tq), 0)
    for c in range(n_sel):
        vc = jnp.broadcast_to(x[c:c + 1, :], (8, tq))
        for r in range(n_grp):
            if 8 * r > c:
                hit = jnp.where(vc >= xg[r], 1.0, 0.0)
            elif 8 * r + 7 <= c:
                hit = jnp.where(vc > xg[r], 1.0, 0.0)
            else:
                hit = jnp.where(sub > c - 8 * r, jnp.where(vc >= xg[r], 1.0, 0.0), jnp.where(vc > xg[r], 1.0, 0.0))
            rank_g[r] = rank_g[r] + hit
    rank = jnp.concatenate(rank_g + [jnp.zeros((SEL_ROWS - 8 * n_grp, tq), F32)] * (SEL_ROWS > 8 * n_grp), axis=0)
    selected = forced | ((x > neg) & (rank < k_free))
    bias_t = jnp.where(selected, 0.0, MASK_VALUE)
    bias = jnp.concatenate([bias_t, jnp.zeros((LANES - SEL_ROWS, tq), F32)], axis=0).T
    lane = lax.broadcasted_iota(I32, (tq, LANES), 1)
    for h in range(HEADS_PER_GROUP):
        qa_ref[h, :, 0:HEAD_DIM] = q_ref[h]
        qa_ref[h, :, HEAD_DIM:2 * HEAD_DIM] = jnp.where(lane < SEL_ROWS, bias, qf_ref[0, h:h + 1, :]).astype(BF16)


def _cmp_attn(q_t, kvcmp, slopes, ovt, qfeat, B, tq=256):
    NH, T, dh = q_t.shape
    S = T // B
    nq = S // tq
    nhb = kvcmp.shape[2]
    n_sel = S // SEL_BLOCK
    assert n_sel <= SEL_ROWS
    topk = min(SEL_TOPK, n_sel)
    qmap = lambda bb, g, i: (g, bb * nq + i, 0)
    return pl.pallas_call(
        functools.partial(_cmp_attn_kernel, tq=tq, nhb=nhb, n_sel=n_sel, topk=topk),
        grid=(B, N_KV_GROUPS, nq),
        in_specs=[
            pl.BlockSpec((HEADS_PER_GROUP, tq, dh), qmap),
            pl.BlockSpec((1, 1, nhb, dh), lambda bb, g, i: (g, bb, 0, 0)),
            pl.BlockSpec((1, 1, nhb, dh), lambda bb, g, i: (N_KV_GROUPS + g, bb, 0, 0)),
            pl.BlockSpec((1, 8, LANES), lambda bb, g, i: (g, 0, 0)),
            pl.BlockSpec((SEL_ROWS, nhb), lambda bb, g, i: (0, 0)),
            pl.BlockSpec((1, 8, LANES), lambda bb, g, i: (g, 0, 0)),
        ],
        out_specs=[
            pl.BlockSpec((HEADS_PER_GROUP, tq, dh), qmap),
            pl.BlockSpec((HEADS_PER_GROUP, tq, 2 * dh), qmap),
        ],
        out_shape=[
            jax.ShapeDtypeStruct((NH, T, dh), F32),
            jax.ShapeDtypeStruct((NH, T, 2 * dh), BF16),
        ],
        compiler_params=_cparams(("parallel", "parallel", "parallel")),
        name="cmp_attn",
    )(q_t, kvcmp, kvcmp, slopes, ovt, qfeat)


def _sel_attn_kernel(q_ref, k_ref, v_ref, o_ref, m_scr, acc_scr, *, tq, tk):
    i = pl.program_id(2)
    q0 = i * tq
    m_scr[...] = jnp.full_like(m_scr, MASK_VALUE)
    acc_scr[...] = jnp.zeros_like(acc_scr)

    def scores(j):
        ka_t = k_ref[0, j]
        return tuple(_dot(q_ref[h], ka_t) for h in range(HEADS_PER_GROUP))

    def absorb(j, s_all, diagonal):
        k0 = pl.multiple_of(j * tk, tk)
        va = v_ref[0, pl.ds(k0, tk), :]
        if diagonal:
            qpos = q0 + lax.broadcasted_iota(I32, (tq, 1), 0)
            kpos = k0 + lax.broadcasted_iota(I32, (1, tk), 1)
            causal = kpos <= qpos
        for h in range(HEADS_PER_GROUP):
            s = s_all[h]
            if diagonal:
                s = jnp.where(causal, s, MASK_VALUE)
            m_prev = m_scr[h]
            m_new = jnp.maximum(m_prev, jnp.max(s, axis=-1, keepdims=True))
            alpha = jnp.exp(m_prev - m_new)
            p = jnp.exp(s - m_new)
            acc_scr[h] = alpha * acc_scr[h] + _dot(p.astype(BF16), va)
            m_scr[h] = m_new

    def full_tile(j, s_all):
        s_next = scores(j + 1)
        absorb(j, s_all, False)
        return s_next

    n_full = q0 // tk
    s_last = lax.fori_loop(0, n_full, full_tile, scores(0))
    absorb(n_full, s_last, True)
    for h in range(HEADS_PER_GROUP):
        o_ref[h] = acc_scr[h, :, 0:HEAD_DIM] / acc_scr[h, :, HEAD_DIM:HEAD_DIM + 1]


def _sel_attn(q_aug, k_aug_t, v_aug, B, tq=256, tk=SEL_KEY_TILE):
    NH, T, dq = q_aug.shape
    dh = dq // 2
    S = T // B
    assert tk % tq == 0 and S % tk == 0
    nq = S // tq
    nkt = S // tk
    qmap = lambda bb, g, i: (g, bb * nq + i, 0)
    return pl.pallas_call(
        functools.partial(_sel_attn_kernel, tq=tq, tk=tk),
        grid=(B, N_KV_GROUPS, nq),
        in_specs=[
            pl.BlockSpec((HEADS_PER_GROUP, tq, dq), qmap),
            pl.BlockSpec((1, nkt, dq, tk), lambda bb, g, i: (g, bb, 0, 0)),
            pl.BlockSpec((1, S, dq), lambda bb, g, i: (g, bb, 0)),
        ],
        out_specs=pl.BlockSpec((HEADS_PER_GROUP, tq, dh), qmap),
        out_shape=jax.ShapeDtypeStruct((NH, T, dh), F32),
        scratch_shapes=[
            pltpu.VMEM((HEADS_PER_GROUP, tq, 1), F32),
            pltpu.VMEM((HEADS_PER_GROUP, tq, dq), F32),
        ],
        compiler_params=_cparams(("parallel", "parallel", "parallel")),
        name="nsa_sel_attn",
    )(q_aug, k_aug_t, v_aug)


def _win_attn_kernel(q_ref, k_ref, v_ref, sl_ref, o_ref, *, tq):
    i = pl.program_id(2)
    q0 = i * tq
    kw = WINDOW + tq
    k0 = pl.multiple_of(jnp.maximum(q0 - WINDOW, 0), tq)
    k = k_ref[0, pl.ds(k0, kw), :]
    v = v_ref[0, pl.ds(k0, kw), :]
    qpos = q0 + lax.broadcasted_iota(I32, (tq, 1), 0)
    kpos = k0 + lax.broadcasted_iota(I32, (1, kw), 1)
    dist = qpos - kpos
    valid = (dist >= 0) & (dist < WINDOW)
    krel = (kpos - q0).astype(F32)
    for h in range(HEADS_PER_GROUP):
        s = _dot_nt(q_ref[h], k) + sl_ref[0, h:h + 1, 0:1] * krel
        s = jnp.where(valid, s, MASK_VALUE)
        m = jnp.max(s, axis=-1, keepdims=True)
        p = jnp.exp(s - m)
        l = jnp.sum(p, axis=-1, keepdims=True)
        o_ref[h] = _dot(p.astype(BF16), v) / l


def _win_attn(q_t, kv_t, slopes, B, tq=256):
    NH, T, dh = q_t.shape
    S = T // B
    assert S >= WINDOW + tq and WINDOW % tq == 0
    nq = S // tq
    qmap = lambda bb, g, i: (g, bb * nq + i, 0)
    return pl.pallas_call(
        functools.partial(_win_attn_kernel, tq=tq),
        grid=(B, N_KV_GROUPS, nq),
        in_specs=[
            pl.BlockSpec((HEADS_PER_GROUP, tq, dh), qmap),
            pl.BlockSpec((1, S, dh), lambda bb, g, i: (2 * N_KV_GROUPS + g, bb, 0)),
            pl.BlockSpec((1, S, dh), lambda bb, g, i: (3 * N_KV_GROUPS + g, bb, 0)),
            pl.BlockSpec((1, 8, LANES), lambda bb, g, i: (g, 0, 0)),
        ],
        out_specs=pl.BlockSpec((HEADS_PER_GROUP, tq, dh), qmap),
        out_shape=jax.ShapeDtypeStruct((NH, T, dh), F32),
        compiler_params=_cparams(("parallel", "parallel", "parallel")),
        name="nsa_win_attn",
    )(q_t, kv_t, kv_t, slopes)


def _attn_out_kernel(oc_ref, os_ref, ow_ref, gt_ref, x_ref, w_ref, g_ref, b_ref, of_ref, ob_ref, lhs):
    for hd in range(N_HEADS):
        g0 = gt_ref[:, N_BRANCHES * hd:N_BRANCHES * hd + 1]
        g1 = gt_ref[:, N_BRANCHES * hd + 1:N_BRANCHES * hd + 2]
        g2 = gt_ref[:, N_BRANCHES * hd + 2:N_BRANCHES * hd + 3]
        o = g0 * oc_ref[hd] + g1 * os_ref[hd] + g2 * ow_ref[hd]
        lhs[:, hd * HEAD_DIM:(hd + 1) * HEAD_DIM] = o.astype(BF16)
    mix = _dot(lhs[...], w_ref[...])
    h = _layer_norm(DEEPNORM_ALPHA * x_ref[...] + mix, g_ref[...], b_ref[...])
    of_ref[...] = h
    ob_ref[...] = h.astype(BF16)


def _attn_out_res_ln(o_cmp, o_sel, o_win, gates, x, w, g, b, tm=256):
    T, D = x.shape
    NH = o_cmp.shape[0]
    hmap = lambda i: (0, i, 0)
    row = lambda i: (i, 0)
    fix = lambda i: (0, 0)
    return pl.pallas_call(
        _attn_out_kernel,
        grid=(T // tm,),
        in_specs=[
            pl.BlockSpec((NH, tm, HEAD_DIM), hmap), pl.BlockSpec((NH, tm, HEAD_DIM), hmap),
            pl.BlockSpec((NH, tm, HEAD_DIM), hmap),
            pl.BlockSpec((tm, LANES), row), pl.BlockSpec((tm, D), row),
            pl.BlockSpec((D, D), fix),
            pl.BlockSpec((1, D), fix), pl.BlockSpec((1, D), fix),
        ],
        out_specs=[pl.BlockSpec((tm, D), row), pl.BlockSpec((tm, D), row)],
        out_shape=[jax.ShapeDtypeStruct((T, D), F32), jax.ShapeDtypeStruct((T, D), BF16)],
        scratch_shapes=[pltpu.VMEM((tm, D), BF16)],
        compiler_params=_cparams(("parallel",), VMEM_LIMIT),
        name="attn_out_res_ln",
    )(o_cmp, o_sel, o_win, gates, x, w, g, b)


def _bf16_pieces(x, n=3):
    out, rest = [], np.asarray(x, np.float32)
    for _ in range(n):
        piece = rest.astype(BF16).astype(np.float32)
        out.append(piece)
        rest = (rest - piece).astype(np.float32)
    return out


def _nsa_tables(S):
    nhb = S // CMP_STRIDE
    n_sel = S // SEL_BLOCK
    heads = np.arange(1, N_HEADS + 1, dtype=np.float32)
    slope = (2.0 ** (-8.0 * heads / N_HEADS)).astype(np.float32)
    slopes = np.zeros((N_KV_GROUPS, 8, LANES), np.float32)
    slopes[:, :HEADS_PER_GROUP, :] = slope.reshape(N_KV_GROUPS, HEADS_PER_GROUP, 1)
    c0 = np.arange(nhb)[None, :] * CMP_STRIDE
    j0 = np.arange(SEL_ROWS)[:, None] * SEL_BLOCK
    ovt = np.maximum(np.minimum(c0 + CMP_BLOCK, j0 + SEL_BLOCK) - np.maximum(c0, j0), 0).astype(np.float32) / CMP_BLOCK
    ovt[:, nhb - 1] = 0.0
    ovt[n_sel:, :] = 0.0
    qfeat = np.zeros((N_KV_GROUPS, 8, LANES), np.float32)
    pieces = _bf16_pieces(slope)
    for t, piece in enumerate(pieces):
        qfeat[:, :HEADS_PER_GROUP, SEL_ROWS + t] = piece.reshape(N_KV_GROUPS, HEADS_PER_GROUP)
        qfeat[:, :HEADS_PER_GROUP, SEL_ROWS + 3 + t] = piece.reshape(N_KV_GROUPS, HEADS_PER_GROUP)
    kpos = np.arange(S)
    kx = np.zeros((S, LANES), np.float32)
    kx[kpos, kpos // SEL_BLOCK] = 1.0
    kx[:, SEL_ROWS:SEL_ROWS + 3] = (SEL_BLOCK * (kpos // SEL_BLOCK))[:, None]
    kx[:, SEL_ROWS + 3:SEL_ROWS + 6] = (kpos % SEL_BLOCK)[:, None]
    return jnp.asarray(slopes), jnp.asarray(ovt, dtype=BF16), jnp.asarray(qfeat), jnp.asarray(kx, dtype=BF16)


def _nsa_layer_res_ln(hf, hb, B, kv_w, cmp_pos, cmp_w1, cmp_w2, w_qg, w_o, g, b):
    T, D = hf.shape
    S = T // B
    hd = N_HEADS * HEAD_DIM
    gsz = N_KV_GROUPS * HEAD_DIM
    slopes, ovt, qfeat, kx = _nsa_tables(S)

    kvw = kv_w.astype(BF16)
    kvc = _proj_heads(hb, kvw[:, :2 * gsz], F32)
    kv_t = _proj_heads(hb, kvw[:, 2 * gsz:], BF16)
    wq = w_qg[:, :hd].astype(BF16)
    wgt = jnp.zeros((D, LANES), F32).at[:, :N_BRANCHES * N_HEADS].set(w_qg[:, hd:]).astype(BF16)
    q_t = _proj_heads(hb, wq, BF16, scale=HEAD_DIM ** -0.5)
    gates = _gates(hb, wgt)

    pos = cmp_pos.reshape(2, 2, CMP_STRIDE * HEAD_DIM)
    kvcmp = _cmp_mlp(kvc, pos, cmp_w1.astype(BF16), cmp_w2.astype(BF16), B)
    o_cmp, q_aug = _cmp_attn(q_t, kvcmp, slopes, ovt, qfeat, B)
    kx_all = jnp.broadcast_to(jnp.tile(kx, (B, 1))[None], (N_KV_GROUPS, T, LANES))
    k_aug = jnp.concatenate([kv_t[:N_KV_GROUPS], kx_all], axis=-1)
    k_aug_t = k_aug.reshape(N_KV_GROUPS, T // SEL_KEY_TILE, SEL_KEY_TILE, 2 * HEAD_DIM).transpose(0, 1, 3, 2)
    ones_col = jnp.zeros((1, 1, LANES), BF16).at[0, 0, 0].set(1.0)
    v_aug = jnp.concatenate([kv_t[N_KV_GROUPS:2 * N_KV_GROUPS],
                             jnp.broadcast_to(ones_col, (N_KV_GROUPS, T, LANES))], axis=-1)
    o_sel = _sel_attn(q_aug, k_aug_t, v_aug, B)
    o_win = _win_attn(q_t, kv_t, slopes, B)
    return _attn_out_res_ln(o_cmp, o_sel, o_win, gates, hf, w_o.astype(BF16), g, b)


def kernel(x, conv_w_pw1, conv_b_pw1, conv_w_dw, conv_b_dw, conv_ln_g, conv_ln_b, conv_w_pw2, kv_w, cmp_pos, cmp_w1, cmp_w2, nsa_w_qg, nsa_w_o, moe_wg, moe_bg, moe_we, moe_be, moe_w_gu, moe_w_down, ln_g, ln_b):
    B, S, D = x.shape
    T = B * S
    hf = x.reshape(T, D)
    hb = hf.astype(BF16)
    r = lambda v: v.reshape(1, -1)
    for l in range(DEPTH):
        if l < N_A_LAYERS:
            glu = _pw1_glu(hb, conv_w_pw1[l].astype(BF16), r(conv_b_pw1[l]))
            conv = _dwconv(glu.reshape(B, S, D), conv_w_dw[l], r(conv_b_dw[l])).reshape(T, D)
            hf, hb = _pw2_res_ln(conv, hf, conv_w_pw2[l].astype(BF16), r(conv_ln_g[l]), r(conv_ln_b[l]),
                                 r(ln_g[l, 0]), r(ln_b[l, 0]))
        else:
            j = l - N_A_LAYERS
            hf, hb = _nsa_layer_res_ln(hf, hb, B, kv_w, cmp_pos, cmp_w1, cmp_w2, nsa_w_qg[j], nsa_w_o[j],
                                       r(ln_g[l, 0]), r(ln_b[l, 0]))
        hf, hb = _hier_moe_res_ln(hf, moe_wg[l], moe_bg[l], moe_we[l], moe_be[l], moe_w_gu, moe_w_down, l,
                                  r(ln_g[l, 1]), r(ln_b[l, 1]))
    return hf.reshape(B, S, D)
```

```python
import functools

import numpy as np
import jax
import jax.numpy as jnp
from jax import lax
from jax.experimental import pallas as pl
from jax.experimental.pallas import tpu as pltpu

F32 = jnp.float32
BF16 = jnp.bfloat16
I32 = jnp.int32

DEPTH = 2
N_A_LAYERS = DEPTH // 2
DEEPNORM_ALPHA = (2.0 * DEPTH) ** 0.25
LN_EPS = 1e-5
CONV_WIDTH = 31
N_HEADS = 16
HEAD_DIM = 128
N_KV_GROUPS = 4
HEADS_PER_GROUP = N_HEADS // N_KV_GROUPS
CMP_BLOCK = 32
CMP_STRIDE = 16
SEL_BLOCK = 64
SEL_TOPK = 16
WINDOW = 512
N_BRANCHES = 3
MASK_VALUE = -1e30
N_GROUPS = 4
EXPERTS_PER_GROUP = 8
N_EXPERTS = N_GROUPS * EXPERTS_PER_GROUP
MOE_CHUNK = 128

LANES = 128
SUBLANES = 8
VMEM_LIMIT = 56 * 1024 * 1024

CONV_HALO = 32
SEL_ROWS = 64
SEL_KEY_TILE = 512


def _cparams(sem, vmem=None):
    return pltpu.CompilerParams(dimension_semantics=sem, vmem_limit_bytes=vmem)


def _sigmoid(x):
    return 1.0 / (1.0 + jnp.exp(-x))


def _layer_norm(x, g, b):
    mu = jnp.mean(x, axis=-1, keepdims=True)
    xc = x - mu
    var = jnp.mean(xc * xc, axis=-1, keepdims=True)
    return xc * lax.rsqrt(var + LN_EPS) * g + b


def _dot(a, b):
    return jnp.dot(a, b, preferred_element_type=F32)


def _dot_nt(a, b):
    return lax.dot_general(a, b, (((1,), (1,)), ((), ())), preferred_element_type=F32)


def _pw1_glu_kernel(x_ref, wa_ref, wg_ref, ba_ref, bg_ref, o_ref):
    x = x_ref[...]
    a = _dot(x, wa_ref[...]) + ba_ref[...]
    g = _dot(x, wg_ref[...]) + bg_ref[...]
    o_ref[...] = a * _sigmoid(g)


def _pw1_glu(xb, w, b, tm=512, tn=512):
    T, D = xb.shape
    nj = D // tn
    return pl.pallas_call(
        _pw1_glu_kernel,
        grid=(nj, T // tm),
        in_specs=[
            pl.BlockSpec((tm, D), lambda j, i: (i, 0)),
            pl.BlockSpec((D, tn), lambda j, i: (0, j)),
            pl.BlockSpec((D, tn), lambda j, i: (0, j + nj)),
            pl.BlockSpec((1, tn), lambda j, i: (0, j)),
            pl.BlockSpec((1, tn), lambda j, i: (0, j + nj)),
        ],
        out_specs=pl.BlockSpec((tm, tn), lambda j, i: (i, j)),
        out_shape=jax.ShapeDtypeStruct((T, D), F32),
        compiler_params=_cparams(("parallel", "parallel"), VMEM_LIMIT),
        name="pw1_glu",
    )(xb, w, w, b, b)


def _dwconv_kernel(x_ref, halo_ref, w_ref, b_ref, o_ref, xsh, *, ts, tc, rc):
    i = pl.program_id(1)
    xsh[0, 0:CONV_HALO, :] = jnp.where(i > 0, halo_ref[0], 0.0)
    xsh[0, CONV_HALO:CONV_HALO + ts, :] = x_ref[0]
    span = ts + CONV_HALO - SUBLANES
    for p in range(1, SUBLANES):
        xsh[p, 0:span, :] = xsh[0, p:p + span, :]
    off = CONV_HALO - (CONV_WIDTH - 1)
    for c0 in range(0, tc, LANES):
        for r0 in range(0, ts, rc):
            acc = jnp.broadcast_to(b_ref[:, c0:c0 + LANES], (rc, LANES))
            for k in range(CONV_WIDTH):
                p = (k + off) % SUBLANES
                base = r0 + k + off - p
                acc = acc + w_ref[k:k + 1, c0:c0 + LANES] * xsh[p, base:base + rc, c0:c0 + LANES]
            o_ref[0, r0:r0 + rc, c0:c0 + LANES] = acc


def _dwconv(x, w, b, ts=512, tc=256, rc=128):
    B, S, D = x.shape
    hb = ts // CONV_HALO
    return pl.pallas_call(
        functools.partial(_dwconv_kernel, ts=ts, tc=tc, rc=rc),
        grid=(B, S // ts, D // tc),
        in_specs=[
            pl.BlockSpec((1, ts, tc), lambda bb, i, c: (bb, i, c)),
            pl.BlockSpec((1, CONV_HALO, tc), lambda bb, i, c: (bb, jnp.maximum(i * hb - 1, 0), c)),
            pl.BlockSpec((CONV_WIDTH, tc), lambda bb, i, c: (0, c)),
            pl.BlockSpec((1, tc), lambda bb, i, c: (0, c)),
        ],
        out_specs=pl.BlockSpec((1, ts, tc), lambda bb, i, c: (bb, i, c)),
        out_shape=jax.ShapeDtypeStruct((B, S, D), F32),
        scratch_shapes=[pltpu.VMEM((SUBLANES, CONV_HALO + ts, tc), F32)],
        compiler_params=_cparams(("parallel", "parallel", "parallel")),
        name="dwconv",
    )(x, x, w, b)


def _pw2_kernel(c_ref, x_ref, w_ref, cg_ref, cb_ref, g_ref, b_ref, of_ref, ob_ref):
    y = _layer_norm(c_ref[...], cg_ref[...], cb_ref[...])
    y = y * _sigmoid(y)
    mix = _dot(y.astype(BF16), w_ref[...])
    h = _layer_norm(DEEPNORM_ALPHA * x_ref[...] + mix, g_ref[...], b_ref[...])
    of_ref[...] = h
    ob_ref[...] = h.astype(BF16)


def _pw2_res_ln(conv, x, w, cg, cb, g, b, tm=256):
    T, D = x.shape
    row = lambda i: (i, 0)
    fix = lambda i: (0, 0)
    return pl.pallas_call(
        _pw2_kernel,
        grid=(T // tm,),
        in_specs=[
            pl.BlockSpec((tm, D), row), pl.BlockSpec((tm, D), row),
            pl.BlockSpec((D, D), fix),
            pl.BlockSpec((1, D), fix), pl.BlockSpec((1, D), fix),
            pl.BlockSpec((1, D), fix), pl.BlockSpec((1, D), fix),
        ],
        out_specs=[pl.BlockSpec((tm, D), row), pl.BlockSpec((tm, D), row)],
        out_shape=[jax.ShapeDtypeStruct((T, D), F32), jax.ShapeDtypeStruct((T, D), BF16)],
        compiler_params=_cparams(("parallel",), VMEM_LIMIT),
        name="pw2_res_ln",
    )(conv, x, w, cg, cb, g, b)


def _router_kernel(h_ref, w_ref, b_ref, ri_ref, rw_ref, cnt_ref, *, tm):
    s = pl.program_id(0)

    @pl.when(s == 0)
    def _():
        cnt_ref[...] = jnp.zeros_like(cnt_ref)

    h = h_ref[...]
    w = w_ref[...]
    h_hi = h.astype(BF16)
    h_lo = (h - h_hi.astype(F32)).astype(BF16)
    w_hi = w.astype(BF16)
    w_lo = (w - w_hi.astype(F32)).astype(BF16)
    lg = _dot(h_hi, w_hi) + _dot(h_lo, w_hi) + _dot(h_hi, w_lo) + b_ref[...]

    lane = lax.broadcasted_iota(I32, (tm, LANES), 1)
    lanef = lane.astype(F32)
    neg = jnp.float32(-jnp.inf)
    big = jnp.float32(1e9)

    is_g = lane < N_GROUPS
    lgm = jnp.where(is_g, lg, neg)
    mg = jnp.max(lgm, axis=-1, keepdims=True)
    grp = jnp.min(jnp.where(lgm == mg, lanef, big), axis=-1, keepdims=True)
    sg = jnp.sum(jnp.where(is_g, jnp.exp(lg - mg), 0.0), axis=-1, keepdims=True)
    gw = 1.0 / sg

    lo = N_GROUPS + grp * EXPERTS_PER_GROUP
    in_g = (lanef >= lo) & (lanef < lo + EXPERTS_PER_GROUP)
    le = jnp.where(in_g, lg, neg)
    m1 = jnp.max(le, axis=-1, keepdims=True)
    i1 = jnp.min(jnp.where(le == m1, lanef, big), axis=-1, keepdims=True)
    le2 = jnp.where(lanef == i1, neg, le)
    m2 = jnp.max(le2, axis=-1, keepdims=True)
    i2 = jnp.min(jnp.where(le2 == m2, lanef, big), axis=-1, keepdims=True)
    t = jnp.exp(m2 - m1)
    p1 = 1.0 / (1.0 + t)
    w1 = p1 * gw
    w2 = (t * p1) * gw
    e1 = i1 - N_GROUPS
    e2 = i2 - N_GROUPS

    oh1 = (lanef == e1)
    oh2 = (lanef == e2)
    oh = (oh1 | oh2).astype(F32)
    r_i = lax.broadcasted_iota(I32, (tm, tm), 0)
    c_i = lax.broadcasted_iota(I32, (tm, tm), 1)
    tri = (c_i < r_i).astype(BF16)
    cs = _dot(tri, oh.astype(BF16)) + cnt_ref[...].astype(F32)
    rank1 = jnp.sum(jnp.where(oh1, cs, 0.0), axis=-1, keepdims=True)
    rank2 = jnp.sum(jnp.where(oh2, cs, 0.0), axis=-1, keepdims=True)
    cnt_ref[...] = cnt_ref[...] + jnp.sum(oh, axis=0, keepdims=True).astype(I32)

    ri = jnp.where(lane == 0, e1, jnp.where(lane == 1, e2, jnp.where(lane == 2, rank1, jnp.where(lane == 3, rank2, 0.0))))
    ri_ref[...] = ri.astype(I32)
    rw_ref[...] = jnp.where(lane == 0, w1, jnp.where(lane == 1, w2, 0.0))


def _router(h, wcat, bcat, tm=256):
    T, D = h.shape
    return pl.pallas_call(
        functools.partial(_router_kernel, tm=tm),
        grid=(T // tm,),
        in_specs=[
            pl.BlockSpec((tm, D), lambda i: (i, 0)),
            pl.BlockSpec((D, LANES), lambda i: (0, 0)),
            pl.BlockSpec((1, LANES), lambda i: (0, 0)),
        ],
        out_specs=[
            pl.BlockSpec((tm, LANES), lambda i: (i, 0)),
            pl.BlockSpec((tm, LANES), lambda i: (i, 0)),
            pl.BlockSpec((1, LANES), lambda i: (0, 0)),
        ],
        out_shape=[
            jax.ShapeDtypeStruct((T, LANES), I32),
            jax.ShapeDtypeStruct((T, LANES), F32),
            jax.ShapeDtypeStruct((1, LANES), I32),
        ],
        compiler_params=_cparams(("arbitrary",)),
        name="moe_router",
    )(h, wcat, bcat)


def _dispatch_kernel(dest_ref, zs_ref, zc_ref, nu_ref, x_ref, xs_hbm, zblk, sem, *, tm, n_chunks):
    s = pl.program_id(0)

    def row_copy(r, d):
        return pltpu.make_async_copy(x_ref.at[pl.ds(r, 1)], xs_hbm.at[pl.ds(d, 1)], sem)

    def zero_row_copy(d):
        return pltpu.make_async_copy(zblk.at[pl.ds(0, 1)], xs_hbm.at[pl.ds(d, 1)], sem)

    def zero_chunk_copy(c):
        return pltpu.make_async_copy(zblk, xs_hbm.at[pl.ds(pl.multiple_of(c * MOE_CHUNK, MOE_CHUNK), MOE_CHUNK)], sem)

    @pl.when(s == 0)
    def _():
        zblk[...] = jnp.zeros_like(zblk)

        def fill(start):
            def per_expert(e, c):
                def per_row(r, c2):
                    cp = zero_row_copy(zs_ref[e] + r)
                    if start:
                        cp.start()
                    else:
                        cp.wait()
                    return c2
                lax.fori_loop(0, zc_ref[e], per_row, 0)
                return c
            lax.fori_loop(0, N_EXPERTS, per_expert, 0)

            def per_chunk(c, c2):
                cp = zero_chunk_copy(c)
                if start:
                    cp.start()
                else:
                    cp.wait()
                return c2
            lax.fori_loop(nu_ref[0], n_chunks, per_chunk, 0)

        fill(True)
        fill(False)

    base = s * tm

    def issue(r, c):
        t = base + r
        row_copy(r, dest_ref[2 * t]).start(priority=0)
        row_copy(r, dest_ref[2 * t + 1]).start(priority=1)
        return c

    lax.fori_loop(0, tm, issue, 0, unroll=8)

    def drain(r, c):
        row_copy(0, 0).wait()
        row_copy(0, 0).wait()
        return c

    lax.fori_loop(0, tm, drain, 0, unroll=8)


def _dispatch(x, dest_flat, zstart, zcount, n_used, n_rows, tm=256):
    T, D = x.shape
    return pl.pallas_call(
        functools.partial(_dispatch_kernel, tm=tm, n_chunks=n_rows // MOE_CHUNK),
        grid_spec=pltpu.PrefetchScalarGridSpec(
            num_scalar_prefetch=4,
            grid=(T // tm,),
            in_specs=[pl.BlockSpec((tm, D), lambda i, *_: (i, 0))],
            out_specs=pl.BlockSpec(memory_space=pl.ANY),
            scratch_shapes=[pltpu.VMEM((MOE_CHUNK, D), F32), pltpu.SemaphoreType.DMA(())],
        ),
        out_shape=jax.ShapeDtypeStruct((n_rows, D), F32),
        compiler_params=_cparams(("arbitrary",)),
        name="moe_dispatch",
    )(dest_flat, zstart, zcount, n_used, x)


def _moe_mlp_kernel(ce_ref, nu_ref, first_ref, nxt_ref, x_ref, wgu_hbm, wd_hbm, o_ref,
                    wgu_stage, wd_stage, wgu_bf, wd_bf, sem, *, F, layer):
    c = pl.program_id(0)

    def w_copies(e):
        return (pltpu.make_async_copy(wgu_hbm.at[layer, e], wgu_stage, sem.at[0]),
                pltpu.make_async_copy(wd_hbm.at[layer, e], wd_stage, sem.at[1]))

    @pl.when(c < nu_ref[0])
    def _():
        @pl.when(first_ref[c] == 1)
        def _():
            @pl.when(c == 0)
            def _():
                for cp in w_copies(ce_ref[0]):
                    cp.start()

            for cp in w_copies(ce_ref[c]):
                cp.wait()
            wgu_bf[...] = wgu_stage[...].astype(BF16)
            wd_bf[...] = wd_stage[...].astype(BF16)

            @pl.when(nxt_ref[c] >= 0)
            def _():
                for cp in w_copies(nxt_ref[c]):
                    cp.start()

        x = x_ref[...].astype(BF16)
        gu = _dot(x, wgu_bf[...])
        g = gu[:, :F]
        u = gu[:, F:]
        hmid = (g * _sigmoid(g)) * u
        o_ref[...] = _dot(hmid.astype(BF16), wd_bf[...])

    @pl.when(c >= nu_ref[0])
    def _():
        o_ref[...] = jnp.zeros_like(o_ref)


def _moe_mlp(xs, chunk_e, n_used, first, nxt, w_gu, w_down, layer):
    n_rows, D = xs.shape
    F2 = w_gu.shape[-1]
    F = F2 // 2
    n_chunks = n_rows // MOE_CHUNK

    def x_map(c, ce, nu, fi, nx):
        return (jnp.minimum(c, nu[0] - 1), 0)

    return pl.pallas_call(
        functools.partial(_moe_mlp_kernel, F=F, layer=layer),
        grid_spec=pltpu.PrefetchScalarGridSpec(
            num_scalar_prefetch=4,
            grid=(n_chunks,),
            in_specs=[
                pl.BlockSpec((MOE_CHUNK, D), x_map),
                pl.BlockSpec(memory_space=pl.ANY),
                pl.BlockSpec(memory_space=pl.ANY),
            ],
            out_specs=pl.BlockSpec((MOE_CHUNK, D), lambda c, ce, nu, fi, nx: (c, 0)),
            scratch_shapes=[
                pltpu.VMEM((D, F2), F32), pltpu.VMEM((F, D), F32),
                pltpu.VMEM((D, F2), BF16), pltpu.VMEM((F, D), BF16),
                pltpu.SemaphoreType.DMA((2,)),
            ],
        ),
        out_shape=jax.ShapeDtypeStruct((n_rows, D), F32),
        compiler_params=_cparams(("arbitrary",), VMEM_LIMIT),
        name="moe_mlp",
    )(chunk_e, n_used, first, nxt, xs, w_gu, w_down)


def _combine_kernel(dest_ref, y_hbm, rw_ref, h_ref, g_ref, b_ref, of_ref, ob_ref, gbuf, sem, *, tm, nsteps):
    s = pl.program_id(0)

    def row_copy(src_row, slot, k, r):
        return pltpu.make_async_copy(y_hbm.at[pl.ds(src_row, 1)], gbuf.at[slot, k, pl.ds(r, 1)], sem.at[slot])

    def issue(step, slot):
        base = step * tm

        def body(r, c):
            t = base + r
            row_copy(dest_ref[2 * t], slot, 0, r).start(priority=0)
            row_copy(dest_ref[2 * t + 1], slot, 1, r).start(priority=1)
            return c

        lax.fori_loop(0, tm, body, 0, unroll=8)

    @pl.when(s == 0)
    def _():
        issue(0, 0)

    @pl.when(s + 1 < nsteps)
    def _():
        issue(s + 1, (s + 1) % 2)

    slot = s % 2

    def drain(r, c):
        row_copy(0, slot, 0, 0).wait()
        row_copy(0, slot, 1, 0).wait()
        return c

    lax.fori_loop(0, tm, drain, 0, unroll=8)

    w0 = rw_ref[:, 0:1]
    w1 = rw_ref[:, 1:2]
    y = w0 * gbuf[slot, 0] + w1 * gbuf[slot, 1]
    h = _layer_norm(DEEPNORM_ALPHA * h_ref[...] + y, g_ref[...], b_ref[...])
    of_ref[...] = h
    ob_ref[...] = h.astype(BF16)


def _combine_res_ln(y_rows, dest_flat, rw, h, g, b, tm=256):
    T, D = h.shape
    nsteps = T // tm
    row = lambda i, d: (i, 0)
    fix = lambda i, d: (0, 0)
    return pl.pallas_call(
        functools.partial(_combine_kernel, tm=tm, nsteps=nsteps),
        grid_spec=pltpu.PrefetchScalarGridSpec(
            num_scalar_prefetch=1,
            grid=(nsteps,),
            in_specs=[
                pl.BlockSpec(memory_space=pl.ANY),
                pl.BlockSpec((tm, LANES), row),
                pl.BlockSpec((tm, D), row),
                pl.BlockSpec((1, D), fix), pl.BlockSpec((1, D), fix),
            ],
            out_specs=[pl.BlockSpec((tm, D), row), pl.BlockSpec((tm, D), row)],
            scratch_shapes=[pltpu.VMEM((2, 2, tm, D), F32), pltpu.SemaphoreType.DMA((2,))],
        ),
        out_shape=[jax.ShapeDtypeStruct((T, D), F32), jax.ShapeDtypeStruct((T, D), BF16)],
        compiler_params=_cparams(("arbitrary",), VMEM_LIMIT),
        name="moe_combine",
    )(dest_flat, y_rows, rw, h, g, b)


def _hier_moe_res_ln(h, wg, bg, we, be, w_gu, w_down, layer, g, b):
    T, D = h.shape
    wcat = jnp.zeros((D, LANES), F32).at[:, :N_GROUPS].set(wg).at[:, N_GROUPS:N_GROUPS + N_EXPERTS].set(we)
    bcat = jnp.zeros((1, LANES), F32).at[0, :N_GROUPS].set(bg).at[0, N_GROUPS:N_GROUPS + N_EXPERTS].set(be)
    ri, rw, cnt = _router(h, wcat, bcat)

    counts = cnt[0, :N_EXPERTS]
    padded = (counts + MOE_CHUNK - 1) // MOE_CHUNK * MOE_CHUNK
    pad_end = jnp.cumsum(padded)
    pad_start = pad_end - padded
    A = T * 2
    n_chunks = (A + N_EXPERTS * (MOE_CHUNK - 1) + MOE_CHUNK - 1) // MOE_CHUNK
    n_rows = n_chunks * MOE_CHUNK
    dest = pad_start[ri[:, 0:2]] + ri[:, 2:4]
    dest_flat = dest.reshape(-1).astype(I32)
    n_used = (pad_end[-1] // MOE_CHUNK).astype(I32)
    cidx = jnp.minimum(jnp.arange(n_chunks, dtype=I32), n_used - 1)
    chunk_e = jnp.sum((pad_end[None, :] <= (cidx * MOE_CHUNK)[:, None]).astype(I32), axis=1)
    chunk_e = jnp.minimum(chunk_e, N_EXPERTS - 1)
    zstart = (pad_start + counts).astype(I32)
    zcount = (padded - counts).astype(I32)
    carange = jnp.arange(n_chunks, dtype=I32)
    prev_e = jnp.concatenate([jnp.full((1,), -1, I32), chunk_e[:-1]])
    first = ((chunk_e != prev_e) & (carange < n_used)).astype(I32)
    run_end = (pad_end[chunk_e] // MOE_CHUNK).astype(I32)
    nxt = jnp.where(run_end < n_used, chunk_e[jnp.minimum(run_end, n_chunks - 1)], -1).astype(I32)

    n_used = n_used.reshape(1)
    xs = _dispatch(h, dest_flat, zstart, zcount, n_used, n_rows)
    ys = _moe_mlp(xs, chunk_e, n_used, first, nxt, w_gu, w_down, layer)
    return _combine_res_ln(ys, dest_flat, rw, h, g, b)


def _proj_heads_kernel(x_ref, w_ref, o_ref, *, nh, scale):
    acc = _dot(x_ref[...], w_ref[...])
    if scale != 1.0:
        acc = acc * scale
    for c in range(nh):
        o_ref[c] = acc[:, c * HEAD_DIM:(c + 1) * HEAD_DIM].astype(o_ref.dtype)


def _proj_heads(xb, w, out_dtype, scale=1.0, tm=512, tn=512):
    T, D = xb.shape
    N = w.shape[1]
    nh = tn // HEAD_DIM
    return pl.pallas_call(
        functools.partial(_proj_heads_kernel, nh=nh, scale=scale),
        grid=(N // tn, T // tm),
        in_specs=[
            pl.BlockSpec((tm, D), lambda j, i: (i, 0)),
            pl.BlockSpec((D, tn), lambda j, i: (0, j)),
        ],
        out_specs=pl.BlockSpec((nh, tm, HEAD_DIM), lambda j, i: (j, i, 0)),
        out_shape=jax.ShapeDtypeStruct((N // HEAD_DIM, T, HEAD_DIM), out_dtype),
        compiler_params=_cparams(("parallel", "parallel"), VMEM_LIMIT),
        name="proj_heads",
    )(xb, w)


def _gates_kernel(x_ref, w_ref, o_ref):
    o_ref[...] = _sigmoid(_dot(x_ref[...], w_ref[...]))


def _gates(xb, w, tm=512):
    T, D = xb.shape
    return pl.pallas_call(
        _gates_kernel,
        grid=(T // tm,),
        in_specs=[pl.BlockSpec((tm, D), lambda i: (i, 0)), pl.BlockSpec((D, LANES), lambda i: (0, 0))],
        out_specs=pl.BlockSpec((tm, LANES), lambda i: (i, 0)),
        out_shape=jax.ShapeDtypeStruct((T, LANES), F32),
        compiler_params=_cparams(("parallel",)),
        name="nsa_gates",
    )(xb, w)


def _cmp_mlp_kernel(x_ref, pos_ref, w1_ref, w2_ref, o_ref, *, nhb):
    x = x_ref[0, 0]
    half = x.shape[1]
    lo = (x + pos_ref[0, 0:1, :]).astype(BF16)
    hi = (x + pos_ref[0, 1:2, :]).astype(BF16)
    a_lo = _dot(lo, w1_ref[0, 0:half, :])
    a_hi = _dot(hi, w1_ref[0, half:2 * half, :])
    pre = a_lo + pltpu.roll(a_hi, nhb - 1, 0)
    hid = 0.5 * pre * (1.0 + jnp.tanh(0.7978845608028654 * (pre + 0.044715 * pre * pre * pre)))
    comp = _dot(hid.astype(BF16), w2_ref[0])
    row = lax.broadcasted_iota(I32, comp.shape, 0)
    o_ref[0, 0] = jnp.where(row < nhb - 1, comp, 0.0).astype(o_ref.dtype)


def _cmp_mlp(kvc, pos, w1, w2, B):
    R, T, dh = kvc.shape
    S = T // B
    nhb = S // CMP_STRIDE
    half = CMP_STRIDE * dh
    x4 = kvc.reshape(R, B, nhb, half)
    H = w1.shape[-1]
    return pl.pallas_call(
        functools.partial(_cmp_mlp_kernel, nhb=nhb),
        grid=(R, B),
        in_specs=[
            pl.BlockSpec((1, 1, nhb, half), lambda r, bb: (r, bb, 0, 0)),
            pl.BlockSpec((1, 2, half), lambda r, bb: (r // N_KV_GROUPS, 0, 0)),
            pl.BlockSpec((1, 2 * half, H), lambda r, bb: (r // N_KV_GROUPS, 0, 0)),
            pl.BlockSpec((1, H, dh), lambda r, bb: (r // N_KV_GROUPS, 0, 0)),
        ],
        out_specs=pl.BlockSpec((1, 1, nhb, dh), lambda r, bb: (r, bb, 0, 0)),
        out_shape=jax.ShapeDtypeStruct((R, B, nhb, dh), BF16),
        compiler_params=_cparams(("parallel", "parallel"), VMEM_LIMIT),
        name="cmp_mlp",
    )(x4, pos, w1, w2)


def _cmp_attn_kernel(q_ref, k_ref, v_ref, sl_ref, ovt_ref, qf_ref, o_ref, qa_ref, *, tq, nhb, n_sel, topk):
    i = pl.program_id(2)
    q0 = i * tq
    qpos = q0 + lax.broadcasted_iota(I32, (tq, 1), 0)
    n = lax.broadcasted_iota(I32, (1, nhb), 1)
    dist = qpos - (n * CMP_STRIDE + (CMP_BLOCK - 1))
    valid = (dist >= 0) & (n < nhb - 1)
    distf = dist.astype(F32)
    k = k_ref[0, 0]
    v = v_ref[0, 0]
    psum = jnp.zeros((tq, nhb), F32)
    for h in range(HEADS_PER_GROUP):
        s = _dot_nt(q_ref[h], k) - sl_ref[0, h:h + 1, 0:1] * distf
        s = jnp.where(valid, s, MASK_VALUE)
        m = jnp.max(s, axis=-1, keepdims=True)
        e = jnp.where(valid, jnp.exp(s - m), 0.0)
        l = jnp.sum(e, axis=-1, keepdims=True)
        p = e / jnp.maximum(l, 1e-30)
        o_ref[h] = _dot(p.astype(BF16), v).astype(o_ref.dtype)
        psum = psum + p

    ovt = ovt_ref[...]
    p_hi = psum.astype(BF16)
    p_lo = (psum - p_hi.astype(F32)).astype(BF16)
    imp_t = _dot_nt(ovt, p_hi) + _dot_nt(ovt, p_lo)

    blk = lax.broadcasted_iota(I32, (SEL_ROWS, tq), 0)
    cur = lax.shift_right_logical(q0 + lax.broadcasted_iota(I32, (1, tq), 1), 6)
    in_range = (blk <= cur) & (blk < n_sel)
    forced = in_range & ((blk == 0) | (blk == cur) | (blk == cur - 1))
    k_free = (topk - 1 - (cur >= 1).astype(I32) - (cur >= 2).astype(I32)).astype(F32)
    neg = jnp.float32(-jnp.inf)
    x = jnp.where(in_range & jnp.logical_not(forced), imp_t, neg)
    n_grp = (n_sel + 7) // 8
    xg = [x[8 * r:8 * r + 8, :] for r in range(n_grp)]
    rank_g = [jnp.zeros((8, tq), F32) for _ in range(n_grp)]
    sub = lax.broadcasted_iota(I32, (8, tq), 0)
    for c in range(n_sel):
        vc = jnp.broadcast_to(x[c:c + 1, :], (8, tq))
        for r in range(n_grp):
            if 8 * r > c:
                hit = jnp.where(vc >= xg[r], 1.0, 0.0)
            elif 8 * r + 7 <= c:
                hit = jnp.where(vc > xg[r], 1.0, 0.0)
            else:
                hit = jnp.where(sub > c - 8 * r, jnp.where(vc >= xg[r], 1.0, 0.0), jnp.where(vc > xg[r], 1.0, 0.0))
            rank_g[r] = rank_g[r] + hit
    rank = jnp.concatenate(rank_g + [jnp.zeros((SEL_ROWS - 8 * n_grp, tq), F32)] * (SEL_ROWS > 8 * n_grp), axis=0)
    selected = forced | ((x > neg) & (rank < k_free))
    bias_t = jnp.where(selected, 0.0, MASK_VALUE)
    bias = jnp.concatenate([bias_t, jnp.zeros((LANES - SEL_ROWS, tq), F32)], axis=0).T
    lane = lax.broadcasted_iota(I32, (tq, LANES), 1)
    for h in range(HEADS_PER_GROUP):
        qa_ref[h, :, 0:HEAD_DIM] = q_ref[h]
        qa_ref[h, :, HEAD_DIM:2 * HEAD_DIM] = jnp.where(lane < SEL_ROWS, bias, qf_ref[0, h:h + 1, :]).astype(BF16)


def _cmp_attn(q_t, kvcmp, slopes, ovt, qfeat, B, tq=256):
    NH, T, dh = q_t.shape
    S = T // B
    nq = S // tq
    nhb = kvcmp.shape[2]
    n_sel = S // SEL_BLOCK
    assert n_sel <= SEL_ROWS
    topk = min(SEL_TOPK, n_sel)
    qmap = lambda bb, g, i: (g, bb * nq + i, 0)
    return pl.pallas_call(
        functools.partial(_cmp_attn_kernel, tq=tq, nhb=nhb, n_sel=n_sel, topk=topk),
        grid=(B, N_KV_GROUPS, nq),
        in_specs=[
            pl.BlockSpec((HEADS_PER_GROUP, tq, dh), qmap),
            pl.BlockSpec((1, 1, nhb, dh), lambda bb, g, i: (g, bb, 0, 0)),
            pl.BlockSpec((1, 1, nhb, dh), lambda bb, g, i: (N_KV_GROUPS + g, bb, 0, 0)),
            pl.BlockSpec((1, 8, LANES), lambda bb, g, i: (g, 0, 0)),
            pl.BlockSpec((SEL_ROWS, nhb), lambda bb, g, i: (0, 0)),
            pl.BlockSpec((1, 8, LANES), lambda bb, g, i: (g, 0, 0)),
        ],
        out_specs=[
            pl.BlockSpec((HEADS_PER_GROUP, tq, dh), qmap),
            pl.BlockSpec((HEADS_PER_GROUP, tq, 2 * dh), qmap),
        ],
        out_shape=[
            jax.ShapeDtypeStruct((NH, T, dh), BF16),
            jax.ShapeDtypeStruct((NH, T, 2 * dh), BF16),
        ],
        compiler_params=_cparams(("parallel", "parallel", "parallel")),
        name="cmp_attn",
    )(q_t, kvcmp, kvcmp, slopes, ovt, qfeat)


def _sel_attn_kernel(q_ref, k_ref, v_ref, o_ref, m_scr, acc_scr, s_buf, *, tq, tk):
    i = pl.program_id(2)
    q0 = i * tq
    nkt = k_ref.shape[1]
    m_scr[...] = jnp.full_like(m_scr, MASK_VALUE)
    acc_scr[...] = jnp.zeros_like(acc_scr)
    qpos = q0 + lax.broadcasted_iota(I32, (tq, 1), 0)

    def scores_to(t, slot):
        ka_t = k_ref[0, jnp.minimum(t, nkt - 1)]
        causal = (t * tk + lax.broadcasted_iota(I32, (1, tk), 1)) <= qpos
        for h in range(HEADS_PER_GROUP):
            s_buf[slot, h] = jnp.where(causal, _dot(q_ref[h], ka_t), MASK_VALUE)

    def absorb_from(t, slot):
        k0 = pl.multiple_of(jnp.minimum(t, nkt - 1) * tk, tk)
        va = v_ref[0, pl.ds(k0, tk), :]
        half = tq // 2
        for h in range(HEADS_PER_GROUP):
            m_prev = m_scr[h]
            m_new = jnp.maximum(m_prev, jnp.max(s_buf[slot, h], axis=-1, keepdims=True))
            alpha = jnp.exp(m_prev - m_new)
            p = jnp.concatenate([jnp.exp(s_buf[slot, h, 0:half, :] - m_new[0:half]).astype(BF16),
                                 jnp.exp(s_buf[slot, h, half:tq, :] - m_new[half:tq]).astype(BF16)], axis=0)
            acc_scr[h] = alpha * acc_scr[h] + _dot(p, va)
            m_scr[h] = m_new

    def tile_pair(pr, carry):
        t = 2 * pr
        scores_to(t + 1, 1)
        absorb_from(t, 0)
        scores_to(t + 2, 0)
        absorb_from(t + 1, 1)
        return carry

    n_tiles = q0 // tk + 1
    scores_to(0, 0)
    lax.fori_loop(0, (n_tiles + 1) // 2, tile_pair, 0)
    for h in range(HEADS_PER_GROUP):
        o_ref[h] = (acc_scr[h, :, 0:HEAD_DIM] / acc_scr[h, :, HEAD_DIM:HEAD_DIM + 1]).astype(o_ref.dtype)


def _sel_attn(q_aug, k_aug_t, v_aug, B, tq=256, tk=SEL_KEY_TILE):
    NH, T, dq = q_aug.shape
    dh = dq // 2
    S = T // B
    assert tk % tq == 0 and S % tk == 0
    nq = S // tq
    nkt = S // tk
    qmap = lambda bb, g, i: (g, bb * nq + i, 0)
    return pl.pallas_call(
        functools.partial(_sel_attn_kernel, tq=tq, tk=tk),
        grid=(B, N_KV_GROUPS, nq),
        in_specs=[
            pl.BlockSpec((HEADS_PER_GROUP, tq, dq), qmap),
            pl.BlockSpec((1, nkt, dq, tk), lambda bb, g, i: (g, bb, 0, 0)),
            pl.BlockSpec((1, S, dq), lambda bb, g, i: (g, bb, 0)),
        ],
        out_specs=pl.BlockSpec((HEADS_PER_GROUP, tq, dh), qmap),
        out_shape=jax.ShapeDtypeStruct((NH, T, dh), BF16),
        scratch_shapes=[
            pltpu.VMEM((HEADS_PER_GROUP, tq, 1), F32),
            pltpu.VMEM((HEADS_PER_GROUP, tq, dq), F32),
            pltpu.VMEM((2, HEADS_PER_GROUP, tq, tk), F32),
        ],
        compiler_params=_cparams(("parallel", "parallel", "parallel")),
        name="nsa_sel_attn",
    )(q_aug, k_aug_t, v_aug)


def _win_attn_kernel(q_ref, k_ref, v_ref, sl_ref, o_ref, *, tq):
    i = pl.program_id(2)
    q0 = i * tq
    kw = WINDOW + tq
    k0 = pl.multiple_of(jnp.maximum(q0 - WINDOW, 0), tq)
    k = k_ref[0, pl.ds(k0, kw), :]
    v = v_ref[0, pl.ds(k0, kw), :]
    qpos = q0 + lax.broadcasted_iota(I32, (tq, 1), 0)
    kpos = k0 + lax.broadcasted_iota(I32, (1, kw), 1)
    dist = qpos - kpos
    valid = (dist >= 0) & (dist < WINDOW)
    krel = (kpos - q0).astype(F32)
    for h in range(HEADS_PER_GROUP):
        s = _dot_nt(q_ref[h], k) + sl_ref[0, h:h + 1, 0:1] * krel
        s = jnp.where(valid, s, MASK_VALUE)
        m = jnp.max(s, axis=-1, keepdims=True)
        p = jnp.exp(s - m)
        l = jnp.sum(p, axis=-1, keepdims=True)
        o_ref[h] = (_dot(p.astype(BF16), v) / l).astype(o_ref.dtype)


def _win_attn(q_t, kv_t, slopes, B, tq=256):
    NH, T, dh = q_t.shape
    S = T // B
    assert S >= WINDOW + tq and WINDOW % tq == 0
    nq = S // tq
    qmap = lambda bb, g, i: (g, bb * nq + i, 0)
    return pl.pallas_call(
        functools.partial(_win_attn_kernel, tq=tq),
        grid=(B, N_KV_GROUPS, nq),
        in_specs=[
            pl.BlockSpec((HEADS_PER_GROUP, tq, dh), qmap),
            pl.BlockSpec((1, S, dh), lambda bb, g, i: (2 * N_KV_GROUPS + g, bb, 0)),
            pl.BlockSpec((1, S, dh), lambda bb, g, i: (3 * N_KV_GROUPS + g, bb, 0)),
            pl.BlockSpec((1, 8, LANES), lambda bb, g, i: (g, 0, 0)),
        ],
        out_specs=pl.BlockSpec((HEADS_PER_GROUP, tq, dh), qmap),
        out_shape=jax.ShapeDtypeStruct((NH, T, dh), BF16),
        compiler_params=_cparams(("parallel", "parallel", "parallel")),
        name="nsa_win_attn",
    )(q_t, kv_t, kv_t, slopes)


def _attn_out_kernel(oc_ref, os_ref, ow_ref, gt_ref, x_ref, w_ref, g_ref, b_ref, of_ref, ob_ref, lhs):
    for hd in range(N_HEADS):
        g0 = gt_ref[:, N_BRANCHES * hd:N_BRANCHES * hd + 1]
        g1 = gt_ref[:, N_BRANCHES * hd + 1:N_BRANCHES * hd + 2]
        g2 = gt_ref[:, N_BRANCHES * hd + 2:N_BRANCHES * hd + 3]
        o = g0 * oc_ref[hd].astype(F32) + g1 * os_ref[hd].astype(F32) + g2 * ow_ref[hd].astype(F32)
        lhs[:, hd * HEAD_DIM:(hd + 1) * HEAD_DIM] = o.astype(BF16)
    mix = _dot(lhs[...], w_ref[...])
    h = _layer_norm(DEEPNORM_ALPHA * x_ref[...] + mix, g_ref[...], b_ref[...])
    of_ref[...] = h
    ob_ref[...] = h.astype(BF16)


def _attn_out_res_ln(o_cmp, o_sel, o_win, gates, x, w, g, b, tm=256):
    T, D = x.shape
    NH = o_cmp.shape[0]
    hmap = lambda i: (0, i, 0)
    row = lambda i: (i, 0)
    fix = lambda i: (0, 0)
    return pl.pallas_call(
        _attn_out_kernel,
        grid=(T // tm,),
        in_specs=[
            pl.BlockSpec((NH, tm, HEAD_DIM), hmap), pl.BlockSpec((NH, tm, HEAD_DIM), hmap),
            pl.BlockSpec((NH, tm, HEAD_DIM), hmap),
            pl.BlockSpec((tm, LANES), row), pl.BlockSpec((tm, D), row),
            pl.BlockSpec((D, D), fix),
            pl.BlockSpec((1, D), fix), pl.BlockSpec((1, D), fix),
        ],
        out_specs=[pl.BlockSpec((tm, D), row), pl.BlockSpec((tm, D), row)],
        out_shape=[jax.ShapeDtypeStruct((T, D), F32), jax.ShapeDtypeStruct((T, D), BF16)],
        scratch_shapes=[pltpu.VMEM((tm, D), BF16)],
        compiler_params=_cparams(("parallel",), VMEM_LIMIT),
        name="attn_out_res_ln",
    )(o_cmp, o_sel, o_win, gates, x, w, g, b)


def _bf16_pieces(x, n=3):
    out, rest = [], np.asarray(x, np.float32)
    for _ in range(n):
        piece = rest.astype(BF16).astype(np.float32)
        out.append(piece)
        rest = (rest - piece).astype(np.float32)
    return out


def _nsa_tables(S):
    nhb = S // CMP_STRIDE
    n_sel = S // SEL_BLOCK
    heads = np.arange(1, N_HEADS + 1, dtype=np.float32)
    slope = (2.0 ** (-8.0 * heads / N_HEADS)).astype(np.float32)
    slopes = np.zeros((N_KV_GROUPS, 8, LANES), np.float32)
    slopes[:, :HEADS_PER_GROUP, :] = slope.reshape(N_KV_GROUPS, HEADS_PER_GROUP, 1)
    c0 = np.arange(nhb)[None, :] * CMP_STRIDE
    j0 = np.arange(SEL_ROWS)[:, None] * SEL_BLOCK
    ovt = np.maximum(np.minimum(c0 + CMP_BLOCK, j0 + SEL_BLOCK) - np.maximum(c0, j0), 0).astype(np.float32) / CMP_BLOCK
    ovt[:, nhb - 1] = 0.0
    ovt[n_sel:, :] = 0.0
    qfeat = np.zeros((N_KV_GROUPS, 8, LANES), np.float32)
    pieces = _bf16_pieces(slope)
    for t, piece in enumerate(pieces):
        qfeat[:, :HEADS_PER_GROUP, SEL_ROWS + t] = piece.reshape(N_KV_GROUPS, HEADS_PER_GROUP)
        qfeat[:, :HEADS_PER_GROUP, SEL_ROWS + 3 + t] = piece.reshape(N_KV_GROUPS, HEADS_PER_GROUP)
    kpos = np.arange(S)
    kx = np.zeros((S, LANES), np.float32)
    kx[kpos, kpos // SEL_BLOCK] = 1.0
    kx[:, SEL_ROWS:SEL_ROWS + 3] = (SEL_BLOCK * (kpos // SEL_BLOCK))[:, None]
    kx[:, SEL_ROWS + 3:SEL_ROWS + 6] = (kpos % SEL_BLOCK)[:, None]
    return jnp.asarray(slopes), jnp.asarray(ovt, dtype=BF16), jnp.asarray(qfeat), jnp.asarray(kx, dtype=BF16)


def _nsa_layer_res_ln(hf, hb, B, kv_w, cmp_pos, cmp_w1, cmp_w2, w_qg, w_o, g, b):
    T, D = hf.shape
    S = T // B
    hd = N_HEADS * HEAD_DIM
    gsz = N_KV_GROUPS * HEAD_DIM
    slopes, ovt, qfeat, kx = _nsa_tables(S)

    kvw = kv_w.astype(BF16)
    kvc = _proj_heads(hb, kvw[:, :2 * gsz], F32)
    kv_t = _proj_heads(hb, kvw[:, 2 * gsz:], BF16)
    wq = w_qg[:, :hd].astype(BF16)
    wgt = jnp.zeros((D, LANES), F32).at[:, :N_BRANCHES * N_HEADS].set(w_qg[:, hd:]).astype(BF16)
    q_t = _proj_heads(hb, wq, BF16, scale=HEAD_DIM ** -0.5)
    gates = _gates(hb, wgt)

    pos = cmp_pos.reshape(2, 2, CMP_STRIDE * HEAD_DIM)
    kvcmp = _cmp_mlp(kvc, pos, cmp_w1.astype(BF16), cmp_w2.astype(BF16), B)
    o_cmp, q_aug = _cmp_attn(q_t, kvcmp, slopes, ovt, qfeat, B)
    kx_all = jnp.broadcast_to(jnp.tile(kx, (B, 1))[None], (N_KV_GROUPS, T, LANES))
    k_aug = jnp.concatenate([kv_t[:N_KV_GROUPS], kx_all], axis=-1)
    k_aug_t = k_aug.reshape(N_KV_GROUPS, T // SEL_KEY_TILE, SEL_KEY_TILE, 2 * HEAD_DIM).transpose(0, 1, 3, 2)
    ones_col = jnp.zeros((1, 1, LANES), BF16).at[0, 0, 0].set(1.0)
    v_aug = jnp.concatenate([kv_t[N_KV_GROUPS:2 * N_KV_GROUPS],
                             jnp.broadcast_to(ones_col, (N_KV_GROUPS, T, LANES))], axis=-1)
    o_sel = _sel_attn(q_aug, k_aug_t, v_aug, B)
    o_win = _win_attn(q_t, kv_t, slopes, B)
    return _attn_out_res_ln(o_cmp, o_sel, o_win, gates, hf, w_o.astype(BF16), g, b)


def kernel(x, conv_w_pw1, conv_b_pw1, conv_w_dw, conv_b_dw, conv_ln_g, conv_ln_b, conv_w_pw2, kv_w, cmp_pos, cmp_w1, cmp_w2, nsa_w_qg, nsa_w_o, moe_wg, moe_bg, moe_we, moe_be, moe_w_gu, moe_w_down, ln_g, ln_b):
    B, S, D = x.shape
    T = B * S
    hf = x.reshape(T, D)
    hb = hf.astype(BF16)
    r = lambda v: v.reshape(1, -1)
    for l in range(DEPTH):
        if l < N_A_LAYERS:
            glu = _pw1_glu(hb, conv_w_pw1[l].astype(BF16), r(conv_b_pw1[l]))
            conv = _dwconv(glu.reshape(B, S, D), conv_w_dw[l], r(conv_b_dw[l])).reshape(T, D)
            hf, hb = _pw2_res_ln(conv, hf, conv_w_pw2[l].astype(BF16), r(conv_ln_g[l]), r(conv_ln_b[l]),
                                 r(ln_g[l, 0]), r(ln_b[l, 0]))
        else:
            j = l - N_A_LAYERS
            hf, hb = _nsa_layer_res_ln(hf, hb, B, kv_w, cmp_pos, cmp_w1, cmp_w2, nsa_w_qg[j], nsa_w_o[j],
                                       r(ln_g[l, 0]), r(ln_b[l, 0]))
        hf, hb = _hier_moe_res_ln(hf, moe_wg[l], moe_bg[l], moe_we[l], moe_be[l], moe_w_gu, moe_w_down, l,
                                  r(ln_g[l, 1]), r(ln_b[l, 1]))
    return hf.reshape(B, S, D)
```

```python
import functools

import numpy as np
import jax
import jax.numpy as jnp
from jax import lax
from jax.experimental import pallas as pl
from jax.experimental.pallas import tpu as pltpu

F32 = jnp.float32
BF16 = jnp.bfloat16
I32 = jnp.int32
U32 = jnp.uint32

DEPTH = 2
N_A_LAYERS = DEPTH // 2
DEEPNORM_ALPHA = (2.0 * DEPTH) ** 0.25
LN_EPS = 1e-5
CONV_WIDTH = 31
N_HEADS = 16
HEAD_DIM = 128
N_KV_GROUPS = 4
HEADS_PER_GROUP = N_HEADS // N_KV_GROUPS
CMP_BLOCK = 32
CMP_STRIDE = 16
SEL_BLOCK = 64
SEL_TOPK = 16
WINDOW = 512
N_BRANCHES = 3
MASK_VALUE = -1e30
N_GROUPS = 4
EXPERTS_PER_GROUP = 8
N_EXPERTS = N_GROUPS * EXPERTS_PER_GROUP
MOE_CHUNK = 128

LANES = 128
SUBLANES = 8
VMEM_LIMIT = 56 * 1024 * 1024

CONV_HALO = 32
SEL_ROWS = 64
SEL_KEY_TILE = 512


def _cparams(sem, vmem=None):
    return pltpu.CompilerParams(dimension_semantics=sem, vmem_limit_bytes=vmem)


def _sigmoid(x):
    return 1.0 / (1.0 + jnp.exp(-x))


def _layer_norm(x, g, b):
    mu = jnp.mean(x, axis=-1, keepdims=True)
    xc = x - mu
    var = jnp.mean(xc * xc, axis=-1, keepdims=True)
    return xc * lax.rsqrt(var + LN_EPS) * g + b


def _pack_halves(x):
    n = x.shape[-1] // 2
    lo = lax.bitcast_convert_type(x[:, :n].astype(BF16).astype(F32), U32)
    hi = lax.bitcast_convert_type(x[:, n:].astype(BF16).astype(F32), U32)
    return lax.shift_right_logical(lo, jnp.uint32(16)) | hi


def _unpack_halves(p):
    lo = lax.bitcast_convert_type(lax.shift_left(p, jnp.uint32(16)), F32)
    hi = lax.bitcast_convert_type(p & jnp.uint32(0xFFFF0000), F32)
    return lo, hi


def _dot(a, b):
    return jnp.dot(a, b, preferred_element_type=F32)


def _dot_nt(a, b):
    return lax.dot_general(a, b, (((1,), (1,)), ((), ())), preferred_element_type=F32)


def _pw1_glu_kernel(x_ref, wa_ref, wg_ref, ba_ref, bg_ref, o_ref):
    x = x_ref[...]
    a = _dot(x, wa_ref[...]) + ba_ref[...]
    g = _dot(x, wg_ref[...]) + bg_ref[...]
    o_ref[...] = a * _sigmoid(g)


def _pw1_glu(xb, w, b, tm=512, tn=512):
    T, D = xb.shape
    nj = D // tn
    return pl.pallas_call(
        _pw1_glu_kernel,
        grid=(nj, T // tm),
        in_specs=[
            pl.BlockSpec((tm, D), lambda j, i: (i, 0)),
            pl.BlockSpec((D, tn), lambda j, i: (0, j)),
            pl.BlockSpec((D, tn), lambda j, i: (0, j + nj)),
            pl.BlockSpec((1, tn), lambda j, i: (0, j)),
            pl.BlockSpec((1, tn), lambda j, i: (0, j + nj)),
        ],
        out_specs=pl.BlockSpec((tm, tn), lambda j, i: (i, j)),
        out_shape=jax.ShapeDtypeStruct((T, D), F32),
        compiler_params=_cparams(("parallel", "parallel"), VMEM_LIMIT),
        name="pw1_glu",
    )(xb, w, w, b, b)


def _dwconv_kernel(x_ref, halo_ref, w_ref, b_ref, o_ref, xsh, *, ts, tc, rc):
    i = pl.program_id(1)
    xsh[0, 0:CONV_HALO, :] = jnp.where(i > 0, halo_ref[0], 0.0)
    xsh[0, CONV_HALO:CONV_HALO + ts, :] = x_ref[0]
    span = ts + CONV_HALO - SUBLANES
    for p in range(1, SUBLANES):
        xsh[p, 0:span, :] = xsh[0, p:p + span, :]
    off = CONV_HALO - (CONV_WIDTH - 1)
    for c0 in range(0, tc, LANES):
        for r0 in range(0, ts, rc):
            acc = jnp.broadcast_to(b_ref[:, c0:c0 + LANES], (rc, LANES))
            for k in range(CONV_WIDTH):
                p = (k + off) % SUBLANES
                base = r0 + k + off - p
                acc = acc + w_ref[k:k + 1, c0:c0 + LANES] * xsh[p, base:base + rc, c0:c0 + LANES]
            o_ref[0, r0:r0 + rc, c0:c0 + LANES] = acc


def _dwconv(x, w, b, ts=512, tc=256, rc=128):
    B, S, D = x.shape
    hb = ts // CONV_HALO
    return pl.pallas_call(
        functools.partial(_dwconv_kernel, ts=ts, tc=tc, rc=rc),
        grid=(B, S // ts, D // tc),
        in_specs=[
            pl.BlockSpec((1, ts, tc), lambda bb, i, c: (bb, i, c)),
            pl.BlockSpec((1, CONV_HALO, tc), lambda bb, i, c: (bb, jnp.maximum(i * hb - 1, 0), c)),
            pl.BlockSpec((CONV_WIDTH, tc), lambda bb, i, c: (0, c)),
            pl.BlockSpec((1, tc), lambda bb, i, c: (0, c)),
        ],
        out_specs=pl.BlockSpec((1, ts, tc), lambda bb, i, c: (bb, i, c)),
        out_shape=jax.ShapeDtypeStruct((B, S, D), F32),
        scratch_shapes=[pltpu.VMEM((SUBLANES, CONV_HALO + ts, tc), F32)],
        compiler_params=_cparams(("parallel", "parallel", "parallel")),
        name="dwconv",
    )(x, x, w, b)


def _pw2_kernel(c_ref, x_ref, w_ref, cg_ref, cb_ref, g_ref, b_ref, of_ref, op_ref):
    y = _layer_norm(c_ref[...], cg_ref[...], cb_ref[...])
    y = y * _sigmoid(y)
    mix = _dot(y.astype(BF16), w_ref[...])
    h = _layer_norm(DEEPNORM_ALPHA * x_ref[...] + mix, g_ref[...], b_ref[...])
    of_ref[...] = h
    op_ref[...] = _pack_halves(h)


def _pw2_res_ln(conv, x, w, cg, cb, g, b, tm=256):
    T, D = x.shape
    row = lambda i: (i, 0)
    fix = lambda i: (0, 0)
    return pl.pallas_call(
        _pw2_kernel,
        grid=(T // tm,),
        in_specs=[
            pl.BlockSpec((tm, D), row), pl.BlockSpec((tm, D), row),
            pl.BlockSpec((D, D), fix),
            pl.BlockSpec((1, D), fix), pl.BlockSpec((1, D), fix),
            pl.BlockSpec((1, D), fix), pl.BlockSpec((1, D), fix),
        ],
        out_specs=[pl.BlockSpec((tm, D), row), pl.BlockSpec((tm, D // 2), row)],
        out_shape=[jax.ShapeDtypeStruct((T, D), F32), jax.ShapeDtypeStruct((T, D // 2), U32)],
        compiler_params=_cparams(("parallel",), VMEM_LIMIT),
        name="pw2_res_ln",
    )(conv, x, w, cg, cb, g, b)


def _router_kernel(h_ref, w_ref, b_ref, ri_ref, rw_ref, cnt_ref, *, tm):
    s = pl.program_id(0)

    @pl.when(s == 0)
    def _():
        cnt_ref[...] = jnp.zeros_like(cnt_ref)

    h = h_ref[...]
    w = w_ref[...]
    h_hi = h.astype(BF16)
    h_lo = (h - h_hi.astype(F32)).astype(BF16)
    w_hi = w.astype(BF16)
    w_lo = (w - w_hi.astype(F32)).astype(BF16)
    lg = _dot(h_hi, w_hi) + _dot(h_lo, w_hi) + _dot(h_hi, w_lo) + b_ref[...]

    lane = lax.broadcasted_iota(I32, (tm, LANES), 1)
    lanef = lane.astype(F32)
    neg = jnp.float32(-jnp.inf)
    big = jnp.float32(1e9)

    is_g = lane < N_GROUPS
    lgm = jnp.where(is_g, lg, neg)
    mg = jnp.max(lgm, axis=-1, keepdims=True)
    grp = jnp.min(jnp.where(lgm == mg, lanef, big), axis=-1, keepdims=True)
    sg = jnp.sum(jnp.where(is_g, jnp.exp(lg - mg), 0.0), axis=-1, keepdims=True)
    gw = 1.0 / sg

    lo = N_GROUPS + grp * EXPERTS_PER_GROUP
    in_g = (lanef >= lo) & (lanef < lo + EXPERTS_PER_GROUP)
    le = jnp.where(in_g, lg, neg)
    m1 = jnp.max(le, axis=-1, keepdims=True)
    i1 = jnp.min(jnp.where(le == m1, lanef, big), axis=-1, keepdims=True)
    le2 = jnp.where(lanef == i1, neg, le)
    m2 = jnp.max(le2, axis=-1, keepdims=True)
    i2 = jnp.min(jnp.where(le2 == m2, lanef, big), axis=-1, keepdims=True)
    t = jnp.exp(m2 - m1)
    p1 = 1.0 / (1.0 + t)
    w1 = p1 * gw
    w2 = (t * p1) * gw
    e1 = i1 - N_GROUPS
    e2 = i2 - N_GROUPS

    oh1 = (lanef == e1)
    oh2 = (lanef == e2)
    oh = (oh1 | oh2).astype(F32)
    r_i = lax.broadcasted_iota(I32, (tm, tm), 0)
    c_i = lax.broadcasted_iota(I32, (tm, tm), 1)
    tri = (c_i < r_i).astype(BF16)
    cs = _dot(tri, oh.astype(BF16)) + cnt_ref[...].astype(F32)
    rank1 = jnp.sum(jnp.where(oh1, cs, 0.0), axis=-1, keepdims=True)
    rank2 = jnp.sum(jnp.where(oh2, cs, 0.0), axis=-1, keepdims=True)
    cnt_ref[...] = cnt_ref[...] + jnp.sum(oh, axis=0, keepdims=True).astype(I32)

    ri = jnp.where(lane == 0, e1, jnp.where(lane == 1, e2, jnp.where(lane == 2, rank1, jnp.where(lane == 3, rank2, 0.0))))
    ri_ref[...] = ri.astype(I32)
    rw_ref[...] = jnp.where(lane == 0, w1, jnp.where(lane == 1, w2, 0.0))


def _router(h, wcat, bcat, tm=256):
    T, D = h.shape
    return pl.pallas_call(
        functools.partial(_router_kernel, tm=tm),
        grid=(T // tm,),
        in_specs=[
            pl.BlockSpec((tm, D), lambda i: (i, 0)),
            pl.BlockSpec((D, LANES), lambda i: (0, 0)),
            pl.BlockSpec((1, LANES), lambda i: (0, 0)),
        ],
        out_specs=[
            pl.BlockSpec((tm, LANES), lambda i: (i, 0)),
            pl.BlockSpec((tm, LANES), lambda i: (i, 0)),
            pl.BlockSpec((1, LANES), lambda i: (0, 0)),
        ],
        out_shape=[
            jax.ShapeDtypeStruct((T, LANES), I32),
            jax.ShapeDtypeStruct((T, LANES), F32),
            jax.ShapeDtypeStruct((1, LANES), I32),
        ],
        compiler_params=_cparams(("arbitrary",)),
        name="moe_router",
    )(h, wcat, bcat)


def _dispatch_kernel(dest_ref, zs_ref, zc_ref, nu_ref, x_ref, xs_hbm, zblk, sem, *, tm, n_tok, n_chunks):
    s = pl.program_id(0)

    def row_copy(r, d):
        return pltpu.make_async_copy(x_ref.at[pl.ds(r, 1)], xs_hbm.at[pl.ds(d, 1)], sem)

    def zero_row_copy(d):
        return pltpu.make_async_copy(zblk.at[pl.ds(0, 1)], xs_hbm.at[pl.ds(d, 1)], sem)

    def zero_chunk_copy(c):
        return pltpu.make_async_copy(zblk, xs_hbm.at[pl.ds(pl.multiple_of(c * MOE_CHUNK, MOE_CHUNK), MOE_CHUNK)], sem)

    @pl.when(s == 0)
    def _():
        zblk[...] = jnp.zeros_like(zblk)

        def fill(start):
            def per_expert(e, c):
                def per_row(r, c2):
                    cp = zero_row_copy(zs_ref[e] + r)
                    if start:
                        cp.start()
                    else:
                        cp.wait()
                    return c2
                lax.fori_loop(0, zc_ref[e], per_row, 0)
                return c
            lax.fori_loop(0, N_EXPERTS, per_expert, 0)

            def per_chunk(c, c2):
                cp = zero_chunk_copy(c)
                if start:
                    cp.start()
                else:
                    cp.wait()
                return c2
            lax.fori_loop(nu_ref[0], n_chunks, per_chunk, 0)

        fill(True)
        fill(False)

    base = s * tm

    def issue(r, c):
        t = base + r
        row_copy(r, dest_ref[t]).start(priority=0)
        row_copy(r, dest_ref[n_tok + t]).start(priority=1)
        return c

    lax.fori_loop(0, tm, issue, 0, unroll=8)

    def drain(r, c):
        row_copy(0, 0).wait()
        row_copy(0, 0).wait()
        return c

    lax.fori_loop(0, tm, drain, 0, unroll=8)


def _dispatch(x, dest_flat, zstart, zcount, n_used, n_rows, tm=256):
    T, W = x.shape
    return pl.pallas_call(
        functools.partial(_dispatch_kernel, tm=tm, n_tok=T, n_chunks=n_rows // MOE_CHUNK),
        grid_spec=pltpu.PrefetchScalarGridSpec(
            num_scalar_prefetch=4,
            grid=(T // tm,),
            in_specs=[pl.BlockSpec((tm, W), lambda i, *_: (i, 0))],
            out_specs=pl.BlockSpec(memory_space=pl.ANY),
            scratch_shapes=[pltpu.VMEM((MOE_CHUNK, W), x.dtype), pltpu.SemaphoreType.DMA(())],
        ),
        out_shape=jax.ShapeDtypeStruct((n_rows, W), x.dtype),
        compiler_params=_cparams(("arbitrary",)),
        name="moe_dispatch",
    )(dest_flat, zstart, zcount, n_used, x)


def _moe_mlp_kernel(ce_ref, nu_ref, first_ref, nxt_ref, x_ref, wgu_hbm, wd_hbm, o_ref,
                    wgu_stage, wd_stage, wgu_bf, wd_bf, sem, *, F, layer):
    c = pl.program_id(0)

    def w_copies(e):
        return (pltpu.make_async_copy(wgu_hbm.at[layer, e], wgu_stage, sem.at[0]),
                pltpu.make_async_copy(wd_hbm.at[layer, e], wd_stage, sem.at[1]))

    @pl.when(c < nu_ref[0])
    def _():
        @pl.when(first_ref[c] == 1)
        def _():
            @pl.when(c == 0)
            def _():
                for cp in w_copies(ce_ref[0]):
                    cp.start()

            for cp in w_copies(ce_ref[c]):
                cp.wait()
            wgu_bf[...] = wgu_stage[...].astype(BF16)
            wd_bf[...] = wd_stage[...].astype(BF16)

            @pl.when(nxt_ref[c] >= 0)
            def _():
                for cp in w_copies(nxt_ref[c]):
                    cp.start()

        x_lo, x_hi = _unpack_halves(x_ref[...])
        half = x_lo.shape[-1]
        gu = _dot(x_lo.astype(BF16), wgu_bf[0:half, :]) + _dot(x_hi.astype(BF16), wgu_bf[half:2 * half, :])
        g = gu[:, :F]
        u = gu[:, F:]
        hmid = (g * _sigmoid(g)) * u
        o_ref[...] = _pack_halves(_dot(hmid.astype(BF16), wd_bf[...]))

    @pl.when(c >= nu_ref[0])
    def _():
        o_ref[...] = jnp.zeros_like(o_ref)


def _moe_mlp(xs, chunk_e, n_used, first, nxt, w_gu, w_down, layer):
    n_rows, W = xs.shape
    D = 2 * W
    F2 = w_gu.shape[-1]
    F = F2 // 2
    n_chunks = n_rows // MOE_CHUNK

    def x_map(c, ce, nu, fi, nx):
        return (jnp.minimum(c, nu[0] - 1), 0)

    return pl.pallas_call(
        functools.partial(_moe_mlp_kernel, F=F, layer=layer),
        grid_spec=pltpu.PrefetchScalarGridSpec(
            num_scalar_prefetch=4,
            grid=(n_chunks,),
            in_specs=[
                pl.BlockSpec((MOE_CHUNK, W), x_map),
                pl.BlockSpec(memory_space=pl.ANY),
                pl.BlockSpec(memory_space=pl.ANY),
            ],
            out_specs=pl.BlockSpec((MOE_CHUNK, W), lambda c, ce, nu, fi, nx: (c, 0)),
            scratch_shapes=[
                pltpu.VMEM((D, F2), F32), pltpu.VMEM((F, D), F32),
                pltpu.VMEM((D, F2), BF16), pltpu.VMEM((F, D), BF16),
                pltpu.SemaphoreType.DMA((2,)),
            ],
        ),
        out_shape=jax.ShapeDtypeStruct((n_rows, W), U32),
        compiler_params=_cparams(("arbitrary",), VMEM_LIMIT),
        name="moe_mlp",
    )(chunk_e, n_used, first, nxt, xs, w_gu, w_down)


def _combine_kernel(dest_ref, y_hbm, rw_ref, h_ref, g_ref, b_ref, of_ref, ob_ref, gbuf, sem, *, tm, nsteps):
    s = pl.program_id(0)

    def row_copy(src_row, slot, k, r):
        return pltpu.make_async_copy(y_hbm.at[pl.ds(src_row, 1)], gbuf.at[slot, k, pl.ds(r, 1)], sem.at[slot])

    def issue(step, slot):
        base = step * tm

        def body(r, c):
            t = base + r
            row_copy(dest_ref[t], slot, 0, r).start(priority=0)
            row_copy(dest_ref[tm * nsteps + t], slot, 1, r).start(priority=1)
            return c

        lax.fori_loop(0, tm, body, 0, unroll=8)

    @pl.when(s == 0)
    def _():
        issue(0, 0)

    @pl.when(s + 1 < nsteps)
    def _():
        issue(s + 1, (s + 1) % 2)

    slot = s % 2

    def drain(r, c):
        row_copy(0, slot, 0, 0).wait()
        row_copy(0, slot, 1, 0).wait()
        return c

    lax.fori_loop(0, tm, drain, 0, unroll=8)

    w0 = rw_ref[:, 0:1]
    w1 = rw_ref[:, 1:2]
    lo0, hi0 = _unpack_halves(gbuf[slot, 0])
    lo1, hi1 = _unpack_halves(gbuf[slot, 1])
    y = jnp.concatenate([w0 * lo0 + w1 * lo1, w0 * hi0 + w1 * hi1], axis=-1)
    h = _layer_norm(DEEPNORM_ALPHA * h_ref[...] + y, g_ref[...], b_ref[...])
    of_ref[...] = h
    ob_ref[...] = h.astype(BF16)


def _combine_res_ln(y_rows, dest_flat, rw, h, g, b, tm=256):
    T, D = h.shape
    nsteps = T // tm
    row = lambda i, d: (i, 0)
    fix = lambda i, d: (0, 0)
    return pl.pallas_call(
        functools.partial(_combine_kernel, tm=tm, nsteps=nsteps),
        grid_spec=pltpu.PrefetchScalarGridSpec(
            num_scalar_prefetch=1,
            grid=(nsteps,),
            in_specs=[
                pl.BlockSpec(memory_space=pl.ANY),
                pl.BlockSpec((tm, LANES), row),
                pl.BlockSpec((tm, D), row),
                pl.BlockSpec((1, D), fix), pl.BlockSpec((1, D), fix),
            ],
            out_specs=[pl.BlockSpec((tm, D), row), pl.BlockSpec((tm, D), row)],
            scratch_shapes=[pltpu.VMEM((2, 2, tm, D // 2), U32), pltpu.SemaphoreType.DMA((2,))],
        ),
        out_shape=[jax.ShapeDtypeStruct((T, D), F32), jax.ShapeDtypeStruct((T, D), BF16)],
        compiler_params=_cparams(("arbitrary",), VMEM_LIMIT),
        name="moe_combine",
    )(dest_flat, y_rows, rw, h, g, b)


def _hier_moe_res_ln(h, hp, wg, bg, we, be, w_gu, w_down, layer, g, b):
    T, D = h.shape
    wcat = jnp.zeros((D, LANES), F32).at[:, :N_GROUPS].set(wg).at[:, N_GROUPS:N_GROUPS + N_EXPERTS].set(we)
    bcat = jnp.zeros((1, LANES), F32).at[0, :N_GROUPS].set(bg).at[0, N_GROUPS:N_GROUPS + N_EXPERTS].set(be)
    ri, rw, cnt = _router(h, wcat, bcat)

    counts = cnt[0, :N_EXPERTS]
    padded = (counts + MOE_CHUNK - 1) // MOE_CHUNK * MOE_CHUNK
    pad_end = jnp.cumsum(padded)
    pad_start = pad_end - padded
    A = T * 2
    n_chunks = (A + N_EXPERTS * (MOE_CHUNK - 1) + MOE_CHUNK - 1) // MOE_CHUNK
    n_rows = n_chunks * MOE_CHUNK
    ri_t = ri[:, 0:4].T
    dest_flat = (pad_start[ri_t[0:2]] + ri_t[2:4]).reshape(-1).astype(I32)
    n_used = (pad_end[-1] // MOE_CHUNK).astype(I32)
    cidx = jnp.minimum(jnp.arange(n_chunks, dtype=I32), n_used - 1)
    chunk_e = jnp.sum((pad_end[None, :] <= (cidx * MOE_CHUNK)[:, None]).astype(I32), axis=1)
    chunk_e = jnp.minimum(chunk_e, N_EXPERTS - 1)
    zstart = (pad_start + counts).astype(I32)
    zcount = (padded - counts).astype(I32)
    carange = jnp.arange(n_chunks, dtype=I32)
    prev_e = jnp.concatenate([jnp.full((1,), -1, I32), chunk_e[:-1]])
    first = ((chunk_e != prev_e) & (carange < n_used)).astype(I32)
    run_end = (pad_end[chunk_e] // MOE_CHUNK).astype(I32)
    nxt = jnp.where(run_end < n_used, chunk_e[jnp.minimum(run_end, n_chunks - 1)], -1).astype(I32)

    n_used = n_used.reshape(1)
    xs = _dispatch(hp, dest_flat, zstart, zcount, n_used, n_rows)
    ys = _moe_mlp(xs, chunk_e, n_used, first, nxt, w_gu, w_down, layer)
    return _combine_res_ln(ys, dest_flat, rw, h, g, b)


def _proj_heads_kernel(x_ref, w_ref, o_ref, *, nh, scale):
    acc = _dot(x_ref[...], w_ref[...])
    if scale != 1.0:
        acc = acc * scale
    for c in range(nh):
        o_ref[c] = acc[:, c * HEAD_DIM:(c + 1) * HEAD_DIM].astype(o_ref.dtype)


def _proj_heads(xb, w, out_dtype, scale=1.0, tm=512, tn=512):
    T, D = xb.shape
    N = w.shape[1]
    nh = tn // HEAD_DIM
    return pl.pallas_call(
        functools.partial(_proj_heads_kernel, nh=nh, scale=scale),
        grid=(N // tn, T // tm),
        in_specs=[
            pl.BlockSpec((tm, D), lambda j, i: (i, 0)),
            pl.BlockSpec((D, tn), lambda j, i: (0, j)),
        ],
        out_specs=pl.BlockSpec((nh, tm, HEAD_DIM), lambda j, i: (j, i, 0)),
        out_shape=jax.ShapeDtypeStruct((N // HEAD_DIM, T, HEAD_DIM), out_dtype),
        compiler_params=_cparams(("parallel", "parallel"), VMEM_LIMIT),
        name="proj_heads",
    )(xb, w)


def _gates_kernel(x_ref, w_ref, o_ref):
    o_ref[...] = _sigmoid(_dot(x_ref[...], w_ref[...]))


def _gates(xb, w, tm=512):
    T, D = xb.shape
    return pl.pallas_call(
        _gates_kernel,
        grid=(T // tm,),
        in_specs=[pl.BlockSpec((tm, D), lambda i: (i, 0)), pl.BlockSpec((D, LANES), lambda i: (0, 0))],
        out_specs=pl.BlockSpec((tm, LANES), lambda i: (i, 0)),
        out_shape=jax.ShapeDtypeStruct((T, LANES), F32),
        compiler_params=_cparams(("parallel",)),
        name="nsa_gates",
    )(xb, w)


def _cmp_mlp_kernel(x_ref, pos_ref, w1_ref, w2_ref, o_ref, *, nhb):
    x = x_ref[0, 0]
    half = x.shape[1]
    lo = (x + pos_ref[0, 0:1, :]).astype(BF16)
    hi = (x + pos_ref[0, 1:2, :]).astype(BF16)
    a_lo = _dot(lo, w1_ref[0, 0:half, :])
    a_hi = _dot(hi, w1_ref[0, half:2 * half, :])
    pre = a_lo + pltpu.roll(a_hi, nhb - 1, 0)
    hid = 0.5 * pre * (1.0 + jnp.tanh(0.7978845608028654 * (pre + 0.044715 * pre * pre * pre)))
    comp = _dot(hid.astype(BF16), w2_ref[0])
    row = lax.broadcasted_iota(I32, comp.shape, 0)
    o_ref[0, 0] = jnp.where(row < nhb - 1, comp, 0.0).astype(o_ref.dtype)


def _cmp_mlp(kvc, pos, w1, w2, B):
    R, T, dh = kvc.shape
    S = T // B
    nhb = S // CMP_STRIDE
    half = CMP_STRIDE * dh
    x4 = kvc.reshape(R, B, nhb, half)
    H = w1.shape[-1]
    return pl.pallas_call(
        functools.partial(_cmp_mlp_kernel, nhb=nhb),
        grid=(R, B),
        in_specs=[
            pl.BlockSpec((1, 1, nhb, half), lambda r, bb: (r, bb, 0, 0)),
            pl.BlockSpec((1, 2, half), lambda r, bb: (r // N_KV_GROUPS, 0, 0)),
            pl.BlockSpec((1, 2 * half, H), lambda r, bb: (r // N_KV_GROUPS, 0, 0)),
            pl.BlockSpec((1, H, dh), lambda r, bb: (r // N_KV_GROUPS, 0, 0)),
        ],
        out_specs=pl.BlockSpec((1, 1, nhb, dh), lambda r, bb: (r, bb, 0, 0)),
        out_shape=jax.ShapeDtypeStruct((R, B, nhb, dh), BF16),
        compiler_params=_cparams(("parallel", "parallel"), VMEM_LIMIT),
        name="cmp_mlp",
    )(x4, pos, w1, w2)


def _cmp_attn_kernel(q_ref, k_ref, v_ref, sl_ref, ovt_ref, qf_ref, o_ref, qa_ref, *, tq, nhb, n_sel, topk):
    i = pl.program_id(2)
    q0 = i * tq
    qpos = q0 + lax.broadcasted_iota(I32, (tq, 1), 0)
    n = lax.broadcasted_iota(I32, (1, nhb), 1)
    dist = qpos - (n * CMP_STRIDE + (CMP_BLOCK - 1))
    valid = (dist >= 0) & (n < nhb - 1)
    distf = dist.astype(F32)
    k = k_ref[0, 0]
    v = v_ref[0, 0]
    psum = jnp.zeros((tq, nhb), F32)
    for h in range(HEADS_PER_GROUP):
        s = _dot_nt(q_ref[h], k) - sl_ref[0, h:h + 1, 0:1] * distf
        s = jnp.where(valid, s, MASK_VALUE)
        m = jnp.max(s, axis=-1, keepdims=True)
        e = jnp.where(valid, jnp.exp(s - m), 0.0)
        l = jnp.sum(e, axis=-1, keepdims=True)
        p = e / jnp.maximum(l, 1e-30)
        o_ref[h] = _dot(p.astype(BF16), v).astype(o_ref.dtype)
        psum = psum + p

    ovt = ovt_ref[...]
    p_hi = psum.astype(BF16)
    p_lo = (psum - p_hi.astype(F32)).astype(BF16)
    imp_t = _dot_nt(ovt, p_hi) + _dot_nt(ovt, p_lo)

    blk = lax.broadcasted_iota(I32, (SEL_ROWS, tq), 0)
    cur = lax.shift_right_logical(q0 + lax.broadcasted_iota(I32, (1, tq), 1), 6)
    in_range = (blk <= cur) & (blk < n_sel)
    forced = in_range & ((blk == 0) | (blk == cur) | (blk == cur - 1))
    k_free = (topk - 1 - (cur >= 1).astype(I32) - (cur >= 2).astype(I32)).astype(F32)
    neg = jnp.float32(-jnp.inf)
    x = jnp.where(in_range & jnp.logical_not(forced), imp_t, neg)
    n_grp = (n_sel + 7) // 8
    xg = [x[8 * r:8 * r + 8, :] for r in range(n_grp)]
    rank_g = [jnp.zeros((8, tq), F32) for _ in range(n_grp)]
    sub = lax.broadcasted_iota(I32, (8, tq), 0)
    for c in range(n_sel):
        vc = jnp.broadcast_to(x[c:c + 1, :], (8, tq))
        for r in range(n_grp):
            if 8 * r > c:
                hit = jnp.where(vc >= xg[r], 1.0, 0.0)
            elif 8 * r + 7 <= c:
                hit = jnp.where(vc > xg[r], 1.0, 0.0)
            else:
                hit = jnp.where(sub > c - 8 * r, jnp.where(vc >= xg[r], 1.0, 0.0), jnp.where(vc > xg[r], 1.0, 0.0))
            rank_g[r] = rank_g[r] + hit
    rank = jnp.concatenate(rank_g + [jnp.zeros((SEL_ROWS - 8 * n_grp, tq), F32)] * (SEL_ROWS > 8 * n_grp), axis=0)
    selected = forced | ((x > neg) & (rank < k_free))
    bias_t = jnp.where(selected, 0.0, MASK_VALUE)
    bias = jnp.concatenate([bias_t, jnp.zeros((LANES - SEL_ROWS, tq), F32)], axis=0).T
    lane = lax.broadcasted_iota(I32, (tq, LANES), 1)
    for h in range(HEADS_PER_GROUP):
        qa_ref[h, :, 0:HEAD_DIM] = q_ref[h]
        qa_ref[h, :, HEAD_DIM:2 * HEAD_DIM] = jnp.where(lane < SEL_ROWS, bias, qf_ref[0, h:h + 1, :]).astype(BF16)


def _cmp_attn(q_t, kvcmp, slopes, ovt, qfeat, B, tq=512):
    NH, T, dh = q_t.shape
    S = T // B
    nq = S // tq
    nhb = kvcmp.shape[2]
    n_sel = S // SEL_BLOCK
    assert n_sel <= SEL_ROWS
    topk = min(SEL_TOPK, n_sel)
    qmap = lambda bb, g, i: (g, bb * nq + i, 0)
    return pl.pallas_call(
        functools.partial(_cmp_attn_kernel, tq=tq, nhb=nhb, n_sel=n_sel, topk=topk),
        grid=(B, N_KV_GROUPS, nq),
        in_specs=[
            pl.BlockSpec((HEADS_PER_GROUP, tq, dh), qmap),
            pl.BlockSpec((1, 1, nhb, dh), lambda bb, g, i: (g, bb, 0, 0)),
            pl.BlockSpec((1, 1, nhb, dh), lambda bb, g, i: (N_KV_GROUPS + g, bb, 0, 0)),
            pl.BlockSpec((1, 8, LANES), lambda bb, g, i: (g, 0, 0)),
            pl.BlockSpec((SEL_ROWS, nhb), lambda bb, g, i: (0, 0)),
            pl.BlockSpec((1, 8, LANES), lambda bb, g, i: (g, 0, 0)),
        ],
        out_specs=[
            pl.BlockSpec((HEADS_PER_GROUP, tq, dh), qmap),
            pl.BlockSpec((HEADS_PER_GROUP, tq, 2 * dh), qmap),
        ],
        out_shape=[
            jax.ShapeDtypeStruct((NH, T, dh), BF16),
            jax.ShapeDtypeStruct((NH, T, 2 * dh), BF16),
        ],
        compiler_params=_cparams(("parallel", "parallel", "parallel")),
        name="cmp_attn",
    )(q_t, kvcmp, kvcmp, slopes, ovt, qfeat)


def _sel_attn_kernel(q_ref, k_ref, v_ref, o_ref, m_scr, acc_scr, s_buf, *, tq, tk):
    i = pl.program_id(2)
    q0 = i * tq
    nkt = k_ref.shape[1]
    m_scr[...] = jnp.full_like(m_scr, MASK_VALUE)
    acc_scr[...] = jnp.zeros_like(acc_scr)
    qpos = q0 + lax.broadcasted_iota(I32, (tq, 1), 0)

    def scores_to(t, slot):
        ka_t = k_ref[0, jnp.minimum(t, nkt - 1)]
        causal = (t * tk + lax.broadcasted_iota(I32, (1, tk), 1)) <= qpos
        for h in range(HEADS_PER_GROUP):
            s_buf[slot, h] = jnp.where(causal, _dot(q_ref[h], ka_t), MASK_VALUE)

    def absorb_from(t, slot):
        k0 = pl.multiple_of(jnp.minimum(t, nkt - 1) * tk, tk)
        va = v_ref[0, pl.ds(k0, tk), :]
        half = tq // 2
        for h in range(HEADS_PER_GROUP):
            m_prev = m_scr[h]
            m_new = jnp.maximum(m_prev, jnp.max(s_buf[slot, h], axis=-1, keepdims=True))
            alpha = jnp.exp(m_prev - m_new)
            p = jnp.concatenate([jnp.exp(s_buf[slot, h, 0:half, :] - m_new[0:half]).astype(BF16),
                                 jnp.exp(s_buf[slot, h, half:tq, :] - m_new[half:tq]).astype(BF16)], axis=0)
            acc_scr[h] = alpha * acc_scr[h] + _dot(p, va)
            m_scr[h] = m_new

    def tile_pair(pr, carry):
        t = 2 * pr
        scores_to(t + 1, 1)
        absorb_from(t, 0)
        scores_to(t + 2, 0)
        absorb_from(t + 1, 1)
        return carry

    n_tiles = q0 // tk + 1
    scores_to(0, 0)
    lax.fori_loop(0, (n_tiles + 1) // 2, tile_pair, 0)
    for h in range(HEADS_PER_GROUP):
        o_ref[h] = (acc_scr[h, :, 0:HEAD_DIM] / acc_scr[h, :, HEAD_DIM:HEAD_DIM + 1]).astype(o_ref.dtype)


def _sel_attn(q_aug, k_aug_t, v_aug, B, tq=256, tk=SEL_KEY_TILE):
    NH, T, dq = q_aug.shape
    dh = dq // 2
    S = T // B
    assert tk % tq == 0 and S % tk == 0
    nq = S // tq
    nkt = S // tk
    qmap = lambda bb, g, i: (g, bb * nq + i, 0)
    return pl.pallas_call(
        functools.partial(_sel_attn_kernel, tq=tq, tk=tk),
        grid=(B, N_KV_GROUPS, nq),
        in_specs=[
            pl.BlockSpec((HEADS_PER_GROUP, tq, dq), qmap),
            pl.BlockSpec((1, nkt, dq, tk), lambda bb, g, i: (g, bb, 0, 0)),
            pl.BlockSpec((1, S, dq), lambda bb, g, i: (g, bb, 0)),
        ],
        out_specs=pl.BlockSpec((HEADS_PER_GROUP, tq, dh), qmap),
        out_shape=jax.ShapeDtypeStruct((NH, T, dh), BF16),
        scratch_shapes=[
            pltpu.VMEM((HEADS_PER_GROUP, tq, 1), F32),
            pltpu.VMEM((HEADS_PER_GROUP, tq, dq), F32),
            pltpu.VMEM((2, HEADS_PER_GROUP, tq, tk), F32),
        ],
        compiler_params=_cparams(("parallel", "parallel", "parallel")),
        name="nsa_sel_attn",
    )(q_aug, k_aug_t, v_aug)


def _win_attn_kernel(q_ref, k_ref, v_ref, o_ref, *, tq):
    i = pl.program_id(2)
    q0 = i * tq
    back = WINDOW // tq
    qpos = q0 + lax.broadcasted_iota(I32, (tq, 1), 0)
    tiles = []
    for d in range(back, -1, -1):
        kt = i - d
        ktc = jnp.maximum(kt, 0)
        kpos = kt * tq + lax.broadcasted_iota(I32, (1, tq), 1)
        if d == back:
            valid = ((qpos - kpos) < WINDOW) & (kt >= 0)
        elif d == 0:
            valid = kpos <= qpos
        else:
            valid = kt >= 0
        tiles.append((k_ref[0, ktc], v_ref[0, pl.ds(pl.multiple_of(ktc * tq, tq), tq), :], valid))
    for h in range(HEADS_PER_GROUP):
        q = q_ref[h]
        s_all = [jnp.where(valid, _dot(q, k_t), MASK_VALUE) for k_t, _, valid in tiles]
        m = functools.reduce(jnp.maximum, [jnp.max(s, axis=-1, keepdims=True) for s in s_all])
        acc = functools.reduce(lambda a, b: a + b,
                               [_dot(jnp.exp(s - m).astype(BF16), va) for s, (_, va, _) in zip(s_all, tiles)])
        o_ref[h] = (acc[:, 0:HEAD_DIM] / acc[:, HEAD_DIM:HEAD_DIM + 1]).astype(o_ref.dtype)


def _win_attn(q_aug, k_aug_t, v_aug, B, tq=256):
    NH, T, dq = q_aug.shape
    dh = dq // 2
    S = T // B
    assert WINDOW % tq == 0 and S % tq == 0
    nq = S // tq
    qmap = lambda bb, g, i: (g, bb * nq + i, 0)
    return pl.pallas_call(
        functools.partial(_win_attn_kernel, tq=tq),
        grid=(B, N_KV_GROUPS, nq),
        in_specs=[
            pl.BlockSpec((HEADS_PER_GROUP, tq, dq), qmap),
            pl.BlockSpec((1, nq, dq, tq), lambda bb, g, i: (g, bb, 0, 0)),
            pl.BlockSpec((1, S, dq), lambda bb, g, i: (g, bb, 0)),
        ],
        out_specs=pl.BlockSpec((HEADS_PER_GROUP, tq, dh), qmap),
        out_shape=jax.ShapeDtypeStruct((NH, T, dh), BF16),
        compiler_params=_cparams(("parallel", "parallel", "parallel")),
        name="nsa_win_attn",
    )(q_aug, k_aug_t, v_aug)


def _attn_out_kernel(oc_ref, os_ref, ow_ref, gt_ref, x_ref, w_ref, g_ref, b_ref, of_ref, op_ref, lhs):
    for hd in range(N_HEADS):
        g0 = gt_ref[:, N_BRANCHES * hd:N_BRANCHES * hd + 1]
        g1 = gt_ref[:, N_BRANCHES * hd + 1:N_BRANCHES * hd + 2]
        g2 = gt_ref[:, N_BRANCHES * hd + 2:N_BRANCHES * hd + 3]
        o = g0 * oc_ref[hd].astype(F32) + g1 * os_ref[hd].astype(F32) + g2 * ow_ref[hd].astype(F32)
        lhs[:, hd * HEAD_DIM:(hd + 1) * HEAD_DIM] = o.astype(BF16)
    mix = _dot(lhs[...], w_ref[...])
    h = _layer_norm(DEEPNORM_ALPHA * x_ref[...] + mix, g_ref[...], b_ref[...])
    of_ref[...] = h
    op_ref[...] = _pack_halves(h)


def _attn_out_res_ln(o_cmp, o_sel, o_win, gates, x, w, g, b, tm=256):
    T, D = x.shape
    NH = o_cmp.shape[0]
    hmap = lambda i: (0, i, 0)
    row = lambda i: (i, 0)
    fix = lambda i: (0, 0)
    return pl.pallas_call(
        _attn_out_kernel,
        grid=(T // tm,),
        in_specs=[
            pl.BlockSpec((NH, tm, HEAD_DIM), hmap), pl.BlockSpec((NH, tm, HEAD_DIM), hmap),
            pl.BlockSpec((NH, tm, HEAD_DIM), hmap),
            pl.BlockSpec((tm, LANES), row), pl.BlockSpec((tm, D), row),
            pl.BlockSpec((D, D), fix),
            pl.BlockSpec((1, D), fix), pl.BlockSpec((1, D), fix),
        ],
        out_specs=[pl.BlockSpec((tm, D), row), pl.BlockSpec((tm, D // 2), row)],
        out_shape=[jax.ShapeDtypeStruct((T, D), F32), jax.ShapeDtypeStruct((T, D // 2), U32)],
        scratch_shapes=[pltpu.VMEM((tm, D), BF16)],
        compiler_params=_cparams(("parallel",), VMEM_LIMIT),
        name="attn_out_res_ln",
    )(o_cmp, o_sel, o_win, gates, x, w, g, b)


def _bf16_pieces(x, n=3):
    out, rest = [], np.asarray(x, np.float32)
    for _ in range(n):
        piece = rest.astype(BF16).astype(np.float32)
        out.append(piece)
        rest = (rest - piece).astype(np.float32)
    return out


def _nsa_tables(S):
    nhb = S // CMP_STRIDE
    n_sel = S // SEL_BLOCK
    heads = np.arange(1, N_HEADS + 1, dtype=np.float32)
    slope = (2.0 ** (-8.0 * heads / N_HEADS)).astype(np.float32)
    slopes = np.zeros((N_KV_GROUPS, 8, LANES), np.float32)
    slopes[:, :HEADS_PER_GROUP, :] = slope.reshape(N_KV_GROUPS, HEADS_PER_GROUP, 1)
    c0 = np.arange(nhb)[None, :] * CMP_STRIDE
    j0 = np.arange(SEL_ROWS)[:, None] * SEL_BLOCK
    ovt = np.maximum(np.minimum(c0 + CMP_BLOCK, j0 + SEL_BLOCK) - np.maximum(c0, j0), 0).astype(np.float32) / CMP_BLOCK
    ovt[:, nhb - 1] = 0.0
    ovt[n_sel:, :] = 0.0
    qfeat = np.zeros((N_KV_GROUPS, 8, LANES), np.float32)
    pieces = _bf16_pieces(slope)
    for t, piece in enumerate(pieces):
        qfeat[:, :HEADS_PER_GROUP, SEL_ROWS + t] = piece.reshape(N_KV_GROUPS, HEADS_PER_GROUP)
        qfeat[:, :HEADS_PER_GROUP, SEL_ROWS + 3 + t] = piece.reshape(N_KV_GROUPS, HEADS_PER_GROUP)
    kpos = np.arange(S)
    kx = np.zeros((S, LANES), np.float32)
    kx[kpos, kpos // SEL_BLOCK] = 1.0
    kx[:, SEL_ROWS:SEL_ROWS + 3] = (SEL_BLOCK * (kpos // SEL_BLOCK))[:, None]
    kx[:, SEL_ROWS + 3:SEL_ROWS + 6] = (kpos % SEL_BLOCK)[:, None]
    return jnp.asarray(slopes), jnp.asarray(ovt, dtype=BF16), jnp.asarray(qfeat), jnp.asarray(kx, dtype=BF16)


def _nsa_layer_res_ln(hf, hb, B, kv_w, cmp_pos, cmp_w1, cmp_w2, w_qg, w_o, g, b):
    T, D = hf.shape
    S = T // B
    hd = N_HEADS * HEAD_DIM
    gsz = N_KV_GROUPS * HEAD_DIM
    slopes, ovt, qfeat, kx = _nsa_tables(S)

    kvw = kv_w.astype(BF16)
    kvc = _proj_heads(hb, kvw[:, :2 * gsz], F32)
    kv_t = _proj_heads(hb, kvw[:, 2 * gsz:], BF16)
    wq = w_qg[:, :hd].astype(BF16)
    wgt = jnp.zeros((D, LANES), F32).at[:, :N_BRANCHES * N_HEADS].set(w_qg[:, hd:]).astype(BF16)
    q_t = _proj_heads(hb, wq, BF16, scale=HEAD_DIM ** -0.5)
    gates = _gates(hb, wgt)

    pos = cmp_pos.reshape(2, 2, CMP_STRIDE * HEAD_DIM)
    kvcmp = _cmp_mlp(kvc, pos, cmp_w1.astype(BF16), cmp_w2.astype(BF16), B)
    o_cmp, q_aug = _cmp_attn(q_t, kvcmp, slopes, ovt, qfeat, B)
    kx_all = jnp.broadcast_to(jnp.tile(kx, (B, 1))[None], (N_KV_GROUPS, T, LANES))
    k_aug = jnp.concatenate([kv_t[:N_KV_GROUPS], kx_all], axis=-1)
    k_aug_t = k_aug.reshape(N_KV_GROUPS, T // SEL_KEY_TILE, SEL_KEY_TILE, 2 * HEAD_DIM).transpose(0, 1, 3, 2)
    ones_col = jnp.zeros((1, 1, LANES), BF16).at[0, 0, 0].set(1.0)
    v_aug = jnp.concatenate([kv_t[N_KV_GROUPS:2 * N_KV_GROUPS],
                             jnp.broadcast_to(ones_col, (N_KV_GROUPS, T, LANES))], axis=-1)
    o_sel = _sel_attn(q_aug, k_aug_t, v_aug, B)
    win_tq = 256
    lane_id = jnp.arange(LANES)[None, None, :]
    kw_aug = jnp.concatenate([kv_t[2 * N_KV_GROUPS:3 * N_KV_GROUPS],
                              jnp.where(lane_id < SEL_ROWS, jnp.zeros((), BF16), kx_all)], axis=-1)
    kw_aug_t = kw_aug.reshape(N_KV_GROUPS, T // win_tq, win_tq, 2 * HEAD_DIM).transpose(0, 1, 3, 2)
    vw_aug = jnp.concatenate([kv_t[3 * N_KV_GROUPS:],
                              jnp.broadcast_to(ones_col, (N_KV_GROUPS, T, LANES))], axis=-1)
    o_win = _win_attn(q_aug, kw_aug_t, vw_aug, B, tq=win_tq)
    return _attn_out_res_ln(o_cmp, o_sel, o_win, gates, hf, w_o.astype(BF16), g, b)


def kernel(x, conv_w_pw1, conv_b_pw1, conv_w_dw, conv_b_dw, conv_ln_g, conv_ln_b, conv_w_pw2, kv_w, cmp_pos, cmp_w1, cmp_w2, nsa_w_qg, nsa_w_o, moe_wg, moe_bg, moe_we, moe_be, moe_w_gu, moe_w_down, ln_g, ln_b):
    B, S, D = x.shape
    T = B * S
    hf = x.reshape(T, D)
    hb = hf.astype(BF16)
    r = lambda v: v.reshape(1, -1)
    for l in range(DEPTH):
        if l < N_A_LAYERS:
            glu = _pw1_glu(hb, conv_w_pw1[l].astype(BF16), r(conv_b_pw1[l]))
            conv = _dwconv(glu.reshape(B, S, D), conv_w_dw[l], r(conv_b_dw[l])).reshape(T, D)
            hf, hp = _pw2_res_ln(conv, hf, conv_w_pw2[l].astype(BF16), r(conv_ln_g[l]), r(conv_ln_b[l]),
                                 r(ln_g[l, 0]), r(ln_b[l, 0]))
        else:
            j = l - N_A_LAYERS
            hf, hp = _nsa_layer_res_ln(hf, hb, B, kv_w, cmp_pos, cmp_w1, cmp_w2, nsa_w_qg[j], nsa_w_o[j],
                                       r(ln_g[l, 0]), r(ln_b[l, 0]))
        hf, hb = _hier_moe_res_ln(hf, hp, moe_wg[l], moe_bg[l], moe_we[l], moe_be[l], moe_w_gu, moe_w_down, l,
                                  r(ln_g[l, 1]), r(ln_b[l, 1]))
    return hf.reshape(B, S, D)
```

```python
import functools

import numpy as np
import jax
import jax.numpy as jnp
from jax import lax
from jax.experimental import pallas as pl
from jax.experimental.pallas import tpu as pltpu

F32 = jnp.float32
BF16 = jnp.bfloat16
I32 = jnp.int32
U32 = jnp.uint32

DEPTH = 2
N_A_LAYERS = DEPTH // 2
DEEPNORM_ALPHA = (2.0 * DEPTH) ** 0.25
LN_EPS = 1e-5
CONV_WIDTH = 31
N_HEADS = 16
HEAD_DIM = 128
N_KV_GROUPS = 4
HEADS_PER_GROUP = N_HEADS // N_KV_GROUPS
CMP_BLOCK = 32
CMP_STRIDE = 16
SEL_BLOCK = 64
SEL_TOPK = 16
WINDOW = 512
N_BRANCHES = 3
MASK_VALUE = -1e30
N_GROUPS = 4
EXPERTS_PER_GROUP = 8
N_EXPERTS = N_GROUPS * EXPERTS_PER_GROUP
MOE_CHUNK = 128

LANES = 128
SUBLANES = 8
VMEM_LIMIT = 56 * 1024 * 1024

CONV_HALO = 32
SEL_ROWS = 64
SEL_KEY_TILE = 512


def _cparams(sem, vmem=None):
    return pltpu.CompilerParams(dimension_semantics=sem, vmem_limit_bytes=vmem)


def _sigmoid(x):
    return 1.0 / (1.0 + jnp.exp(-x))


def _layer_norm(x, g, b):
    mu = jnp.mean(x, axis=-1, keepdims=True)
    xc = x - mu
    var = jnp.mean(xc * xc, axis=-1, keepdims=True)
    return xc * lax.rsqrt(var + LN_EPS) * g + b


def _pack_halves(x):
    n = x.shape[-1] // 2
    lo = lax.bitcast_convert_type(x[:, :n].astype(BF16).astype(F32), U32)
    hi = lax.bitcast_convert_type(x[:, n:].astype(BF16).astype(F32), U32)
    return lax.shift_right_logical(lo, jnp.uint32(16)) | hi


def _unpack_halves(p):
    lo = lax.bitcast_convert_type(lax.shift_left(p, jnp.uint32(16)), F32)
    hi = lax.bitcast_convert_type(p & jnp.uint32(0xFFFF0000), F32)
    return lo, hi


def _dot(a, b):
    return jnp.dot(a, b, preferred_element_type=F32)


def _dot_nt(a, b):
    return lax.dot_general(a, b, (((1,), (1,)), ((), ())), preferred_element_type=F32)


def _pw1_glu_kernel(x_ref, wa_ref, wg_ref, ba_ref, bg_ref, o_ref):
    x = x_ref[...]
    a = _dot(x, wa_ref[...]) + ba_ref[...]
    g = _dot(x, wg_ref[...]) + bg_ref[...]
    o_ref[...] = a * _sigmoid(g)


def _pw1_glu(xb, w, b, tm=512, tn=512):
    T, D = xb.shape
    nj = D // tn
    return pl.pallas_call(
        _pw1_glu_kernel,
        grid=(nj, T // tm),
        in_specs=[
            pl.BlockSpec((tm, D), lambda j, i: (i, 0)),
            pl.BlockSpec((D, tn), lambda j, i: (0, j)),
            pl.BlockSpec((D, tn), lambda j, i: (0, j + nj)),
            pl.BlockSpec((1, tn), lambda j, i: (0, j)),
            pl.BlockSpec((1, tn), lambda j, i: (0, j + nj)),
        ],
        out_specs=pl.BlockSpec((tm, tn), lambda j, i: (i, j)),
        out_shape=jax.ShapeDtypeStruct((T, D), F32),
        compiler_params=_cparams(("parallel", "parallel"), VMEM_LIMIT),
        name="pw1_glu",
    )(xb, w, w, b, b)


def _dwconv_kernel(x_ref, halo_ref, w_ref, b_ref, o_ref, xsh, *, ts, tc, rc):
    i = pl.program_id(1)
    xsh[0, 0:CONV_HALO, :] = jnp.where(i > 0, halo_ref[0], 0.0)
    xsh[0, CONV_HALO:CONV_HALO + ts, :] = x_ref[0]
    span = ts + CONV_HALO - SUBLANES
    for p in range(1, SUBLANES):
        xsh[p, 0:span, :] = xsh[0, p:p + span, :]
    off = CONV_HALO - (CONV_WIDTH - 1)
    for c0 in range(0, tc, LANES):
        for r0 in range(0, ts, rc):
            acc = jnp.broadcast_to(b_ref[:, c0:c0 + LANES], (rc, LANES))
            for k in range(CONV_WIDTH):
                p = (k + off) % SUBLANES
                base = r0 + k + off - p
                acc = acc + w_ref[k:k + 1, c0:c0 + LANES] * xsh[p, base:base + rc, c0:c0 + LANES]
            o_ref[0, r0:r0 + rc, c0:c0 + LANES] = acc


def _dwconv(x, w, b, ts=512, tc=256, rc=128):
    B, S, D = x.shape
    hb = ts // CONV_HALO
    return pl.pallas_call(
        functools.partial(_dwconv_kernel, ts=ts, tc=tc, rc=rc),
        grid=(B, S // ts, D // tc),
        in_specs=[
            pl.BlockSpec((1, ts, tc), lambda bb, i, c: (bb, i, c)),
            pl.BlockSpec((1, CONV_HALO, tc), lambda bb, i, c: (bb, jnp.maximum(i * hb - 1, 0), c)),
            pl.BlockSpec((CONV_WIDTH, tc), lambda bb, i, c: (0, c)),
            pl.BlockSpec((1, tc), lambda bb, i, c: (0, c)),
        ],
        out_specs=pl.BlockSpec((1, ts, tc), lambda bb, i, c: (bb, i, c)),
        out_shape=jax.ShapeDtypeStruct((B, S, D), F32),
        scratch_shapes=[pltpu.VMEM((SUBLANES, CONV_HALO + ts, tc), F32)],
        compiler_params=_cparams(("parallel", "parallel", "parallel")),
        name="dwconv",
    )(x, x, w, b)


def _pw2_kernel(c_ref, x_ref, w_ref, cg_ref, cb_ref, g_ref, b_ref, of_ref, op_ref):
    y = _layer_norm(c_ref[...], cg_ref[...], cb_ref[...])
    y = y * _sigmoid(y)
    mix = _dot(y.astype(BF16), w_ref[...])
    h = _layer_norm(DEEPNORM_ALPHA * x_ref[...] + mix, g_ref[...], b_ref[...])
    of_ref[...] = h
    op_ref[...] = _pack_halves(h)


def _pw2_res_ln(conv, x, w, cg, cb, g, b, tm=256):
    T, D = x.shape
    row = lambda i: (i, 0)
    fix = lambda i: (0, 0)
    return pl.pallas_call(
        _pw2_kernel,
        grid=(T // tm,),
        in_specs=[
            pl.BlockSpec((tm, D), row), pl.BlockSpec((tm, D), row),
            pl.BlockSpec((D, D), fix),
            pl.BlockSpec((1, D), fix), pl.BlockSpec((1, D), fix),
            pl.BlockSpec((1, D), fix), pl.BlockSpec((1, D), fix),
        ],
        out_specs=[pl.BlockSpec((tm, D), row), pl.BlockSpec((tm, D // 2), row)],
        out_shape=[jax.ShapeDtypeStruct((T, D), F32), jax.ShapeDtypeStruct((T, D // 2), U32)],
        compiler_params=_cparams(("parallel",), VMEM_LIMIT),
        name="pw2_res_ln",
    )(conv, x, w, cg, cb, g, b)


def _router_kernel(h_ref, w_ref, b_ref, ri_ref, rw_ref, cnt_ref, *, tm):
    s = pl.program_id(0)

    @pl.when(s == 0)
    def _():
        cnt_ref[...] = jnp.zeros_like(cnt_ref)

    h = h_ref[...]
    w = w_ref[...]
    h_hi = h.astype(BF16)
    h_lo = (h - h_hi.astype(F32)).astype(BF16)
    w_hi = w.astype(BF16)
    w_lo = (w - w_hi.astype(F32)).astype(BF16)
    lg = _dot(h_hi, w_hi) + _dot(h_lo, w_hi) + _dot(h_hi, w_lo) + b_ref[...]

    lane = lax.broadcasted_iota(I32, (tm, LANES), 1)
    lanef = lane.astype(F32)
    neg = jnp.float32(-jnp.inf)
    big = jnp.float32(1e9)

    is_g = lane < N_GROUPS
    lgm = jnp.where(is_g, lg, neg)
    mg = jnp.max(lgm, axis=-1, keepdims=True)
    grp = jnp.min(jnp.where(lgm == mg, lanef, big), axis=-1, keepdims=True)
    sg = jnp.sum(jnp.where(is_g, jnp.exp(lg - mg), 0.0), axis=-1, keepdims=True)
    gw = 1.0 / sg

    lo = N_GROUPS + grp * EXPERTS_PER_GROUP
    in_g = (lanef >= lo) & (lanef < lo + EXPERTS_PER_GROUP)
    le = jnp.where(in_g, lg, neg)
    m1 = jnp.max(le, axis=-1, keepdims=True)
    i1 = jnp.min(jnp.where(le == m1, lanef, big), axis=-1, keepdims=True)
    le2 = jnp.where(lanef == i1, neg, le)
    m2 = jnp.max(le2, axis=-1, keepdims=True)
    i2 = jnp.min(jnp.where(le2 == m2, lanef, big), axis=-1, keepdims=True)
    t = jnp.exp(m2 - m1)
    p1 = 1.0 / (1.0 + t)
    w1 = p1 * gw
    w2 = (t * p1) * gw
    e1 = i1 - N_GROUPS
    e2 = i2 - N_GROUPS

    oh1 = (lanef == e1)
    oh2 = (lanef == e2)
    oh = (oh1 | oh2).astype(F32)
    r_i = lax.broadcasted_iota(I32, (tm, tm), 0)
    c_i = lax.broadcasted_iota(I32, (tm, tm), 1)
    tri = (c_i < r_i).astype(BF16)
    cs = _dot(tri, oh.astype(BF16)) + cnt_ref[...].astype(F32)
    rank1 = jnp.sum(jnp.where(oh1, cs, 0.0), axis=-1, keepdims=True)
    rank2 = jnp.sum(jnp.where(oh2, cs, 0.0), axis=-1, keepdims=True)
    cnt_ref[...] = cnt_ref[...] + jnp.sum(oh, axis=0, keepdims=True).astype(I32)

    ri = jnp.where(lane == 0, e1, jnp.where(lane == 1, e2, jnp.where(lane == 2, rank1, jnp.where(lane == 3, rank2, 0.0))))
    ri_ref[...] = ri.astype(I32)
    rw_ref[...] = jnp.where(lane == 0, w1, jnp.where(lane == 1, w2, 0.0))


def _router(h, wcat, bcat, tm=256):
    T, D = h.shape
    return pl.pallas_call(
        functools.partial(_router_kernel, tm=tm),
        grid=(T // tm,),
        in_specs=[
            pl.BlockSpec((tm, D), lambda i: (i, 0)),
            pl.BlockSpec((D, LANES), lambda i: (0, 0)),
            pl.BlockSpec((1, LANES), lambda i: (0, 0)),
        ],
        out_specs=[
            pl.BlockSpec((tm, LANES), lambda i: (i, 0)),
            pl.BlockSpec((tm, LANES), lambda i: (i, 0)),
            pl.BlockSpec((1, LANES), lambda i: (0, 0)),
        ],
        out_shape=[
            jax.ShapeDtypeStruct((T, LANES), I32),
            jax.ShapeDtypeStruct((T, LANES), F32),
            jax.ShapeDtypeStruct((1, LANES), I32),
        ],
        compiler_params=_cparams(("arbitrary",)),
        name="moe_router",
    )(h, wcat, bcat)


def _dispatch_kernel(dest_ref, zs_ref, zc_ref, nu_ref, x_ref, xs_hbm, zblk, sem, *, tm, n_tok, n_chunks):
    s = pl.program_id(0)

    def row_copy(r, d):
        return pltpu.make_async_copy(x_ref.at[pl.ds(r, 1)], xs_hbm.at[pl.ds(d, 1)], sem)

    def zero_row_copy(d):
        return pltpu.make_async_copy(zblk.at[pl.ds(0, 1)], xs_hbm.at[pl.ds(d, 1)], sem)

    def zero_chunk_copy(c):
        return pltpu.make_async_copy(zblk, xs_hbm.at[pl.ds(pl.multiple_of(c * MOE_CHUNK, MOE_CHUNK), MOE_CHUNK)], sem)

    @pl.when(s == 0)
    def _():
        zblk[...] = jnp.zeros_like(zblk)

        def fill(start):
            def per_expert(e, c):
                def per_row(r, c2):
                    cp = zero_row_copy(zs_ref[e] + r)
                    if start:
                        cp.start()
                    else:
                        cp.wait()
                    return c2
                lax.fori_loop(0, zc_ref[e], per_row, 0)
                return c
            lax.fori_loop(0, N_EXPERTS, per_expert, 0)

            def per_chunk(c, c2):
                cp = zero_chunk_copy(c)
                if start:
                    cp.start()
                else:
                    cp.wait()
                return c2
            lax.fori_loop(nu_ref[0], n_chunks, per_chunk, 0)

        fill(True)
        fill(False)

    base = s * tm

    def issue(r, c):
        t = base + r
        row_copy(r, dest_ref[t]).start(priority=0)
        row_copy(r, dest_ref[n_tok + t]).start(priority=1)
        return c

    lax.fori_loop(0, tm, issue, 0, unroll=8)

    def drain(r, c):
        row_copy(0, 0).wait()
        row_copy(0, 0).wait()
        return c

    lax.fori_loop(0, tm, drain, 0, unroll=8)


def _dispatch(x, dest_flat, zstart, zcount, n_used, n_rows, tm=256):
    T, W = x.shape
    return pl.pallas_call(
        functools.partial(_dispatch_kernel, tm=tm, n_tok=T, n_chunks=n_rows // MOE_CHUNK),
        grid_spec=pltpu.PrefetchScalarGridSpec(
            num_scalar_prefetch=4,
            grid=(T // tm,),
            in_specs=[pl.BlockSpec((tm, W), lambda i, *_: (i, 0))],
            out_specs=pl.BlockSpec(memory_space=pl.ANY),
            scratch_shapes=[pltpu.VMEM((MOE_CHUNK, W), x.dtype), pltpu.SemaphoreType.DMA(())],
        ),
        out_shape=jax.ShapeDtypeStruct((n_rows, W), x.dtype),
        compiler_params=_cparams(("arbitrary",)),
        name="moe_dispatch",
    )(dest_flat, zstart, zcount, n_used, x)


def _moe_mlp_kernel(ce_ref, nu_ref, first_ref, nxt_ref, x_ref, wgu_hbm, wd_hbm, o_ref,
                    wgu_stage, wd_stage, wgu_bf, wd_bf, sem, *, F, layer):
    c = pl.program_id(0)

    def w_copies(e):
        return (pltpu.make_async_copy(wgu_hbm.at[layer, e], wgu_stage, sem.at[0]),
                pltpu.make_async_copy(wd_hbm.at[layer, e], wd_stage, sem.at[1]))

    @pl.when(c < nu_ref[0])
    def _():
        @pl.when(first_ref[c] == 1)
        def _():
            @pl.when(c == 0)
            def _():
                for cp in w_copies(ce_ref[0]):
                    cp.start()

            for cp in w_copies(ce_ref[c]):
                cp.wait()
            wgu_bf[...] = wgu_stage[...].astype(BF16)
            wd_bf[...] = wd_stage[...].astype(BF16)

            @pl.when(nxt_ref[c] >= 0)
            def _():
                for cp in w_copies(nxt_ref[c]):
                    cp.start()

        x_lo, x_hi = _unpack_halves(x_ref[...])
        half = x_lo.shape[-1]
        gu = _dot(x_lo.astype(BF16), wgu_bf[0:half, :]) + _dot(x_hi.astype(BF16), wgu_bf[half:2 * half, :])
        g = gu[:, :F]
        u = gu[:, F:]
        hmid = (g * _sigmoid(g)) * u
        o_ref[...] = _pack_halves(_dot(hmid.astype(BF16), wd_bf[...]))

    @pl.when(c >= nu_ref[0])
    def _():
        o_ref[...] = jnp.zeros_like(o_ref)


def _moe_mlp(xs, chunk_e, n_used, first, nxt, w_gu, w_down, layer):
    n_rows, W = xs.shape
    D = 2 * W
    F2 = w_gu.shape[-1]
    F = F2 // 2
    n_chunks = n_rows // MOE_CHUNK

    def x_map(c, ce, nu, fi, nx):
        return (jnp.minimum(c, nu[0] - 1), 0)

    return pl.pallas_call(
        functools.partial(_moe_mlp_kernel, F=F, layer=layer),
        grid_spec=pltpu.PrefetchScalarGridSpec(
            num_scalar_prefetch=4,
            grid=(n_chunks,),
            in_specs=[
                pl.BlockSpec((MOE_CHUNK, W), x_map),
                pl.BlockSpec(memory_space=pl.ANY),
                pl.BlockSpec(memory_space=pl.ANY),
            ],
            out_specs=pl.BlockSpec((MOE_CHUNK, W), lambda c, ce, nu, fi, nx: (c, 0)),
            scratch_shapes=[
                pltpu.VMEM((D, F2), F32), pltpu.VMEM((F, D), F32),
                pltpu.VMEM((D, F2), BF16), pltpu.VMEM((F, D), BF16),
                pltpu.SemaphoreType.DMA((2,)),
            ],
        ),
        out_shape=jax.ShapeDtypeStruct((n_rows, W), U32),
        compiler_params=_cparams(("arbitrary",), VMEM_LIMIT),
        name="moe_mlp",
    )(chunk_e, n_used, first, nxt, xs, w_gu, w_down)


def _combine_kernel(dest_ref, y_hbm, rw_ref, h_ref, g_ref, b_ref, of_ref, ob_ref, gbuf, sem, *, tm, nsteps):
    s = pl.program_id(0)

    def row_copy(src_row, slot, k, r):
        return pltpu.make_async_copy(y_hbm.at[pl.ds(src_row, 1)], gbuf.at[slot, k, pl.ds(r, 1)], sem.at[slot])

    def issue(step, slot):
        base = step * tm

        def body(r, c):
            t = base + r
            row_copy(dest_ref[t], slot, 0, r).start(priority=0)
            row_copy(dest_ref[tm * nsteps + t], slot, 1, r).start(priority=1)
            return c

        lax.fori_loop(0, tm, body, 0, unroll=8)

    @pl.when(s == 0)
    def _():
        issue(0, 0)

    @pl.when(s + 1 < nsteps)
    def _():
        issue(s + 1, (s + 1) % 2)

    slot = s % 2

    def drain(r, c):
        row_copy(0, slot, 0, 0).wait()
        row_copy(0, slot, 1, 0).wait()
        return c

    lax.fori_loop(0, tm, drain, 0, unroll=8)

    w0 = rw_ref[:, 0:1]
    w1 = rw_ref[:, 1:2]
    lo0, hi0 = _unpack_halves(gbuf[slot, 0])
    lo1, hi1 = _unpack_halves(gbuf[slot, 1])
    y = jnp.concatenate([w0 * lo0 + w1 * lo1, w0 * hi0 + w1 * hi1], axis=-1)
    h = _layer_norm(DEEPNORM_ALPHA * h_ref[...] + y, g_ref[...], b_ref[...])
    of_ref[...] = h
    ob_ref[...] = h.astype(BF16)


def _combine_res_ln(y_rows, dest_flat, rw, h, g, b, tm=256):
    T, D = h.shape
    nsteps = T // tm
    row = lambda i, d: (i, 0)
    fix = lambda i, d: (0, 0)
    return pl.pallas_call(
        functools.partial(_combine_kernel, tm=tm, nsteps=nsteps),
        grid_spec=pltpu.PrefetchScalarGridSpec(
            num_scalar_prefetch=1,
            grid=(nsteps,),
            in_specs=[
                pl.BlockSpec(memory_space=pl.ANY),
                pl.BlockSpec((tm, LANES), row),
                pl.BlockSpec((tm, D), row),
                pl.BlockSpec((1, D), fix), pl.BlockSpec((1, D), fix),
            ],
            out_specs=[pl.BlockSpec((tm, D), row), pl.BlockSpec((tm, D), row)],
            scratch_shapes=[pltpu.VMEM((2, 2, tm, D // 2), U32), pltpu.SemaphoreType.DMA((2,))],
        ),
        out_shape=[jax.ShapeDtypeStruct((T, D), F32), jax.ShapeDtypeStruct((T, D), BF16)],
        compiler_params=_cparams(("arbitrary",), VMEM_LIMIT),
        name="moe_combine",
    )(dest_flat, y_rows, rw, h, g, b)


def _hier_moe_res_ln(h, hp, wg, bg, we, be, w_gu, w_down, layer, g, b):
    T, D = h.shape
    wcat = jnp.zeros((D, LANES), F32).at[:, :N_GROUPS].set(wg).at[:, N_GROUPS:N_GROUPS + N_EXPERTS].set(we)
    bcat = jnp.zeros((1, LANES), F32).at[0, :N_GROUPS].set(bg).at[0, N_GROUPS:N_GROUPS + N_EXPERTS].set(be)
    ri, rw, cnt = _router(h, wcat, bcat)

    counts = cnt[0, :N_EXPERTS]
    padded = (counts + MOE_CHUNK - 1) // MOE_CHUNK * MOE_CHUNK
    pad_end = jnp.cumsum(padded)
    pad_start = pad_end - padded
    A = T * 2
    n_chunks = (A + N_EXPERTS * (MOE_CHUNK - 1) + MOE_CHUNK - 1) // MOE_CHUNK
    n_rows = n_chunks * MOE_CHUNK
    ri_t = ri[:, 0:4].T
    seg = jnp.zeros((2, T), I32)
    for e in range(N_EXPERTS):
        seg = jnp.where(ri_t[0:2] == e, pad_start[e], seg)
    dest_flat = (seg + ri_t[2:4]).reshape(-1).astype(I32)
    n_used = (pad_end[-1] // MOE_CHUNK).astype(I32)
    cidx = jnp.minimum(jnp.arange(n_chunks, dtype=I32), n_used - 1)
    chunk_e = jnp.sum((pad_end[None, :] <= (cidx * MOE_CHUNK)[:, None]).astype(I32), axis=1)
    chunk_e = jnp.minimum(chunk_e, N_EXPERTS - 1)
    zstart = (pad_start + counts).astype(I32)
    zcount = (padded - counts).astype(I32)
    carange = jnp.arange(n_chunks, dtype=I32)
    prev_e = jnp.concatenate([jnp.full((1,), -1, I32), chunk_e[:-1]])
    first = ((chunk_e != prev_e) & (carange < n_used)).astype(I32)
    run_end = (pad_end[chunk_e] // MOE_CHUNK).astype(I32)
    nxt = jnp.where(run_end < n_used, chunk_e[jnp.minimum(run_end, n_chunks - 1)], -1).astype(I32)

    n_used = n_used.reshape(1)
    xs = _dispatch(hp, dest_flat, zstart, zcount, n_used, n_rows)
    ys = _moe_mlp(xs, chunk_e, n_used, first, nxt, w_gu, w_down, layer)
    return _combine_res_ln(ys, dest_flat, rw, h, g, b)


def _proj_heads_kernel(x_ref, w_ref, o_ref, *, nh, scale):
    acc = _dot(x_ref[...], w_ref[...])
    if scale != 1.0:
        acc = acc * scale
    for c in range(nh):
        o_ref[c] = acc[:, c * HEAD_DIM:(c + 1) * HEAD_DIM].astype(o_ref.dtype)


def _proj_heads(xb, w, out_dtype, scale=1.0, tm=512, tn=512):
    T, D = xb.shape
    N = w.shape[1]
    nh = tn // HEAD_DIM
    return pl.pallas_call(
        functools.partial(_proj_heads_kernel, nh=nh, scale=scale),
        grid=(N // tn, T // tm),
        in_specs=[
            pl.BlockSpec((tm, D), lambda j, i: (i, 0)),
            pl.BlockSpec((D, tn), lambda j, i: (0, j)),
        ],
        out_specs=pl.BlockSpec((nh, tm, HEAD_DIM), lambda j, i: (j, i, 0)),
        out_shape=jax.ShapeDtypeStruct((N // HEAD_DIM, T, HEAD_DIM), out_dtype),
        compiler_params=_cparams(("parallel", "parallel"), VMEM_LIMIT),
        name="proj_heads",
    )(xb, w)


def _gates_kernel(x_ref, w_ref, o_ref):
    o_ref[...] = _sigmoid(_dot(x_ref[...], w_ref[...]))


def _gates(xb, w, tm=512):
    T, D = xb.shape
    return pl.pallas_call(
        _gates_kernel,
        grid=(T // tm,),
        in_specs=[pl.BlockSpec((tm, D), lambda i: (i, 0)), pl.BlockSpec((D, LANES), lambda i: (0, 0))],
        out_specs=pl.BlockSpec((tm, LANES), lambda i: (i, 0)),
        out_shape=jax.ShapeDtypeStruct((T, LANES), F32),
        compiler_params=_cparams(("parallel",)),
        name="nsa_gates",
    )(xb, w)


def _cmp_mlp_kernel(x_ref, pos_ref, w1_ref, w2_ref, o_ref, *, nhb):
    x = x_ref[0, 0]
    half = x.shape[1]
    lo = (x + pos_ref[0, 0:1, :]).astype(BF16)
    hi = (x + pos_ref[0, 1:2, :]).astype(BF16)
    a_lo = _dot(lo, w1_ref[0, 0:half, :])
    a_hi = _dot(hi, w1_ref[0, half:2 * half, :])
    pre = a_lo + pltpu.roll(a_hi, nhb - 1, 0)
    hid = 0.5 * pre * (1.0 + jnp.tanh(0.7978845608028654 * (pre + 0.044715 * pre * pre * pre)))
    comp = _dot(hid.astype(BF16), w2_ref[0])
    row = lax.broadcasted_iota(I32, comp.shape, 0)
    o_ref[0, 0] = jnp.where(row < nhb - 1, comp, 0.0).astype(o_ref.dtype)


def _cmp_mlp(kvc, pos, w1, w2, B):
    R, T, dh = kvc.shape
    S = T // B
    nhb = S // CMP_STRIDE
    half = CMP_STRIDE * dh
    x4 = kvc.reshape(R, B, nhb, half)
    H = w1.shape[-1]
    return pl.pallas_call(
        functools.partial(_cmp_mlp_kernel, nhb=nhb),
        grid=(R, B),
        in_specs=[
            pl.BlockSpec((1, 1, nhb, half), lambda r, bb: (r, bb, 0, 0)),
            pl.BlockSpec((1, 2, half), lambda r, bb: (r // N_KV_GROUPS, 0, 0)),
            pl.BlockSpec((1, 2 * half, H), lambda r, bb: (r // N_KV_GROUPS, 0, 0)),
            pl.BlockSpec((1, H, dh), lambda r, bb: (r // N_KV_GROUPS, 0, 0)),
        ],
        out_specs=pl.BlockSpec((1, 1, nhb, dh), lambda r, bb: (r, bb, 0, 0)),
        out_shape=jax.ShapeDtypeStruct((R, B, nhb, dh), BF16),
        compiler_params=_cparams(("parallel", "parallel"), VMEM_LIMIT),
        name="cmp_mlp",
    )(x4, pos, w1, w2)


def _cmp_attn_kernel(q_ref, k_ref, v_ref, sl_ref, ovt_ref, qf_ref, o_ref, qa_ref, *, tq, nhb, n_sel, topk):
    i = pl.program_id(2)
    q0 = i * tq
    qpos = q0 + lax.broadcasted_iota(I32, (tq, 1), 0)
    n = lax.broadcasted_iota(I32, (1, nhb), 1)
    dist = qpos - (n * CMP_STRIDE + (CMP_BLOCK - 1))
    valid = (dist >= 0) & (n < nhb - 1)
    distf = dist.astype(F32)
    k = k_ref[0, 0]
    v = v_ref[0, 0]
    psum = jnp.zeros((tq, nhb), F32)
    for h in range(HEADS_PER_GROUP):
        s = _dot_nt(q_ref[h], k) - sl_ref[0, h:h + 1, 0:1] * distf
        s = jnp.where(valid, s, MASK_VALUE)
        m = jnp.max(s, axis=-1, keepdims=True)
        e = jnp.where(valid, jnp.exp(s - m), 0.0)
        l = jnp.sum(e, axis=-1, keepdims=True)
        p = e / jnp.maximum(l, 1e-30)
        o_ref[h] = _dot(p.astype(BF16), v).astype(o_ref.dtype)
        psum = psum + p

    ovt = ovt_ref[...]
    p_hi = psum.astype(BF16)
    p_lo = (psum - p_hi.astype(F32)).astype(BF16)
    imp_t = _dot_nt(ovt, p_hi) + _dot_nt(ovt, p_lo)

    blk = lax.broadcasted_iota(I32, (SEL_ROWS, tq), 0)
    cur = lax.shift_right_logical(q0 + lax.broadcasted_iota(I32, (1, tq), 1), 6)
    in_range = (blk <= cur) & (blk < n_sel)
    forced = in_range & ((blk == 0) | (blk == cur) | (blk == cur - 1))
    k_free = (topk - 1 - (cur >= 1).astype(I32) - (cur >= 2).astype(I32)).astype(F32)
    neg = jnp.float32(-jnp.inf)
    x = jnp.where(in_range & jnp.logical_not(forced), imp_t, neg)
    n_grp = (n_sel + 7) // 8
    xg = [x[8 * r:8 * r + 8, :] for r in range(n_grp)]
    rank_g = [jnp.zeros((8, tq), F32) for _ in range(n_grp)]
    sub = lax.broadcasted_iota(I32, (8, tq), 0)
    for c in range(n_sel):
        vc = jnp.broadcast_to(x[c:c + 1, :], (8, tq))
        for r in range(n_grp):
            if 8 * r > c:
                hit = jnp.where(vc >= xg[r], 1.0, 0.0)
            elif 8 * r + 7 <= c:
                hit = jnp.where(vc > xg[r], 1.0, 0.0)
            else:
                hit = jnp.where(sub > c - 8 * r, jnp.where(vc >= xg[r], 1.0, 0.0), jnp.where(vc > xg[r], 1.0, 0.0))
            rank_g[r] = rank_g[r] + hit
    rank = jnp.concatenate(rank_g + [jnp.zeros((SEL_ROWS - 8 * n_grp, tq), F32)] * (SEL_ROWS > 8 * n_grp), axis=0)
    selected = forced | ((x > neg) & (rank < k_free))
    bias_t = jnp.where(selected, 0.0, MASK_VALUE)
    bias = jnp.concatenate([bias_t, jnp.zeros((LANES - SEL_ROWS, tq), F32)], axis=0).T
    lane = lax.broadcasted_iota(I32, (tq, LANES), 1)
    for h in range(HEADS_PER_GROUP):
        qa_ref[h, :, 0:HEAD_DIM] = q_ref[h]
        qa_ref[h, :, HEAD_DIM:2 * HEAD_DIM] = jnp.where(lane < SEL_ROWS, bias, qf_ref[0, h:h + 1, :]).astype(BF16)


def _cmp_attn(q_t, kvcmp, slopes, ovt, qfeat, B, tq=512):
    NH, T, dh = q_t.shape
    S = T // B
    nq = S // tq
    nhb = kvcmp.shape[2]
    n_sel = S // SEL_BLOCK
    assert n_sel <= SEL_ROWS
    topk = min(SEL_TOPK, n_sel)
    qmap = lambda bb, g, i: (g, bb * nq + i, 0)
    return pl.pallas_call(
        functools.partial(_cmp_attn_kernel, tq=tq, nhb=nhb, n_sel=n_sel, topk=topk),
        grid=(B, N_KV_GROUPS, nq),
        in_specs=[
            pl.BlockSpec((HEADS_PER_GROUP, tq, dh), qmap),
            pl.BlockSpec((1, 1, nhb, dh), lambda bb, g, i: (g, bb, 0, 0)),
            pl.BlockSpec((1, 1, nhb, dh), lambda bb, g, i: (N_KV_GROUPS + g, bb, 0, 0)),
            pl.BlockSpec((1, 8, LANES), lambda bb, g, i: (g, 0, 0)),
            pl.BlockSpec((SEL_ROWS, nhb), lambda bb, g, i: (0, 0)),
            pl.BlockSpec((1, 8, LANES), lambda bb, g, i: (g, 0, 0)),
        ],
        out_specs=[
            pl.BlockSpec((HEADS_PER_GROUP, tq, dh), qmap),
            pl.BlockSpec((HEADS_PER_GROUP, tq, 2 * dh), qmap),
        ],
        out_shape=[
            jax.ShapeDtypeStruct((NH, T, dh), BF16),
            jax.ShapeDtypeStruct((NH, T, 2 * dh), BF16),
        ],
        compiler_params=_cparams(("parallel", "parallel", "parallel")),
        name="cmp_attn",
    )(q_t, kvcmp, kvcmp, slopes, ovt, qfeat)


def _sel_attn_kernel(q_ref, k_ref, v_ref, o_ref, m_scr, acc_scr, s_buf, *, tq, tk):
    i = pl.program_id(2)
    q0 = i * tq
    nkt = k_ref.shape[1]
    m_scr[...] = jnp.full_like(m_scr, MASK_VALUE)
    acc_scr[...] = jnp.zeros_like(acc_scr)
    qpos = q0 + lax.broadcasted_iota(I32, (tq, 1), 0)

    def scores_to(t, slot):
        ka_t = k_ref[0, jnp.minimum(t, nkt - 1)]
        causal = (t * tk + lax.broadcasted_iota(I32, (1, tk), 1)) <= qpos
        for h in range(HEADS_PER_GROUP):
            s_buf[slot, h] = jnp.where(causal, _dot(q_ref[h], ka_t), MASK_VALUE)

    def absorb_from(t, slot):
        k0 = pl.multiple_of(jnp.minimum(t, nkt - 1) * tk, tk)
        va = v_ref[0, pl.ds(k0, tk), :]
        half = tq // 2
        for h in range(HEADS_PER_GROUP):
            m_prev = m_scr[h]
            m_new = jnp.maximum(m_prev, jnp.max(s_buf[slot, h], axis=-1, keepdims=True))
            alpha = jnp.exp(m_prev - m_new)
            p = jnp.concatenate([jnp.exp(s_buf[slot, h, 0:half, :] - m_new[0:half]).astype(BF16),
                                 jnp.exp(s_buf[slot, h, half:tq, :] - m_new[half:tq]).astype(BF16)], axis=0)
            acc_scr[h] = alpha * acc_scr[h] + _dot(p, va)
            m_scr[h] = m_new

    n_tiles = q0 // tk + 1

    def tile_pair(pr, carry):
        t = 2 * pr
        scores_to(t + 1, 1)
        absorb_from(t, 0)

        @pl.when(t + 1 < n_tiles)
        def _():
            scores_to(t + 2, 0)
            absorb_from(t + 1, 1)

        return carry

    scores_to(0, 0)
    lax.fori_loop(0, (n_tiles + 1) // 2, tile_pair, 0)
    for h in range(HEADS_PER_GROUP):
        o_ref[h] = (acc_scr[h, :, 0:HEAD_DIM] / acc_scr[h, :, HEAD_DIM:HEAD_DIM + 1]).astype(o_ref.dtype)


def _sel_attn(q_aug, k_aug_t, v_aug, B, tq=256, tk=SEL_KEY_TILE):
    NH, T, dq = q_aug.shape
    dh = dq // 2
    S = T // B
    assert tk % tq == 0 and S % tk == 0
    nq = S // tq
    nkt = S // tk
    qmap = lambda bb, g, i: (g, bb * nq + i, 0)
    return pl.pallas_call(
        functools.partial(_sel_attn_kernel, tq=tq, tk=tk),
        grid=(B, N_KV_GROUPS, nq),
        in_specs=[
            pl.BlockSpec((HEADS_PER_GROUP, tq, dq), qmap),
            pl.BlockSpec((1, nkt, dq, tk), lambda bb, g, i: (g, bb, 0, 0)),
            pl.BlockSpec((1, S, dq), lambda bb, g, i: (g, bb, 0)),
        ],
        out_specs=pl.BlockSpec((HEADS_PER_GROUP, tq, dh), qmap),
        out_shape=jax.ShapeDtypeStruct((NH, T, dh), BF16),
        scratch_shapes=[
            pltpu.VMEM((HEADS_PER_GROUP, tq, 1), F32),
            pltpu.VMEM((HEADS_PER_GROUP, tq, dq), F32),
            pltpu.VMEM((2, HEADS_PER_GROUP, tq, tk), F32),
        ],
        compiler_params=_cparams(("parallel", "parallel", "parallel")),
        name="nsa_sel_attn",
    )(q_aug, k_aug_t, v_aug)


def _win_attn_kernel(q_ref, k_ref, v_ref, o_ref, *, tq):
    i = pl.program_id(2)
    q0 = i * tq
    back = WINDOW // tq
    qpos = q0 + lax.broadcasted_iota(I32, (tq, 1), 0)
    tiles = []
    for d in range(back, -1, -1):
        kt = i - d
        ktc = jnp.maximum(kt, 0)
        kpos = kt * tq + lax.broadcasted_iota(I32, (1, tq), 1)
        if d == back:
            valid = ((qpos - kpos) < WINDOW) & (kt >= 0)
        elif d == 0:
            valid = kpos <= qpos
        else:
            valid = kt >= 0
        tiles.append((k_ref[0, ktc], v_ref[0, pl.ds(pl.multiple_of(ktc * tq, tq), tq), :], valid))
    for h in range(HEADS_PER_GROUP):
        q = q_ref[h]
        s_all = [jnp.where(valid, _dot(q, k_t), MASK_VALUE) for k_t, _, valid in tiles]
        m = functools.reduce(jnp.maximum, [jnp.max(s, axis=-1, keepdims=True) for s in s_all])
        acc = functools.reduce(lambda a, b: a + b,
                               [_dot(jnp.exp(s - m).astype(BF16), va) for s, (_, va, _) in zip(s_all, tiles)])
        o_ref[h] = (acc[:, 0:HEAD_DIM] / acc[:, HEAD_DIM:HEAD_DIM + 1]).astype(o_ref.dtype)


def _win_attn(q_aug, k_aug_t, v_aug, B, tq=256):
    NH, T, dq = q_aug.shape
    dh = dq // 2
    S = T // B
    assert WINDOW % tq == 0 and S % tq == 0
    nq = S // tq
    qmap = lambda bb, g, i: (g, bb * nq + i, 0)
    return pl.pallas_call(
        functools.partial(_win_attn_kernel, tq=tq),
        grid=(B, N_KV_GROUPS, nq),
        in_specs=[
            pl.BlockSpec((HEADS_PER_GROUP, tq, dq), qmap),
            pl.BlockSpec((1, nq, dq, tq), lambda bb, g, i: (g, bb, 0, 0)),
            pl.BlockSpec((1, S, dq), lambda bb, g, i: (g, bb, 0)),
        ],
        out_specs=pl.BlockSpec((HEADS_PER_GROUP, tq, dh), qmap),
        out_shape=jax.ShapeDtypeStruct((NH, T, dh), BF16),
        compiler_params=_cparams(("parallel", "parallel", "parallel")),
        name="nsa_win_attn",
    )(q_aug, k_aug_t, v_aug)


def _attn_out_kernel(oc_ref, os_ref, ow_ref, gt_ref, ex_ref, x_ref, w_ref, g_ref, b_ref, of_ref, op_ref, lhs):
    gt = gt_ref[...]
    gt_hi = gt.astype(BF16)
    gt_lo = (gt - gt_hi.astype(F32)).astype(BF16)
    gt2 = jnp.concatenate([gt_hi, gt_lo], axis=-1)
    ge = [_dot(gt2, ex_ref[br]) for br in range(N_BRANCHES)]
    for hd in range(N_HEADS):
        cols = slice(hd * HEAD_DIM, (hd + 1) * HEAD_DIM)
        o = (ge[0][:, cols] * oc_ref[hd].astype(F32) + ge[1][:, cols] * os_ref[hd].astype(F32)
             + ge[2][:, cols] * ow_ref[hd].astype(F32))
        lhs[:, cols] = o.astype(BF16)
    mix = _dot(lhs[...], w_ref[...])
    h = _layer_norm(DEEPNORM_ALPHA * x_ref[...] + mix, g_ref[...], b_ref[...])
    of_ref[...] = h
    op_ref[...] = _pack_halves(h)


def _attn_out_res_ln(o_cmp, o_sel, o_win, gates, x, w, g, b, tm=256):
    T, D = x.shape
    NH = o_cmp.shape[0]
    hmap = lambda i: (0, i, 0)
    row = lambda i: (i, 0)
    fix = lambda i: (0, 0)
    expand = np.zeros((N_BRANCHES, 2 * LANES, NH * HEAD_DIM), np.float32)
    for hd in range(NH):
        for br in range(N_BRANCHES):
            for piece in range(2):
                expand[br, piece * LANES + N_BRANCHES * hd + br, hd * HEAD_DIM:(hd + 1) * HEAD_DIM] = 1.0
    expand = jnp.asarray(expand, dtype=BF16)
    return pl.pallas_call(
        _attn_out_kernel,
        grid=(T // tm,),
        in_specs=[
            pl.BlockSpec((NH, tm, HEAD_DIM), hmap), pl.BlockSpec((NH, tm, HEAD_DIM), hmap),
            pl.BlockSpec((NH, tm, HEAD_DIM), hmap),
            pl.BlockSpec((tm, LANES), row),
            pl.BlockSpec((N_BRANCHES, 2 * LANES, NH * HEAD_DIM), lambda i: (0, 0, 0)),
            pl.BlockSpec((tm, D), row),
            pl.BlockSpec((D, D), fix),
            pl.BlockSpec((1, D), fix), pl.BlockSpec((1, D), fix),
        ],
        out_specs=[pl.BlockSpec((tm, D), row), pl.BlockSpec((tm, D // 2), row)],
        out_shape=[jax.ShapeDtypeStruct((T, D), F32), jax.ShapeDtypeStruct((T, D // 2), U32)],
        scratch_shapes=[pltpu.VMEM((tm, D), BF16)],
        compiler_params=_cparams(("parallel",), VMEM_LIMIT),
        name="attn_out_res_ln",
    )(o_cmp, o_sel, o_win, gates, expand, x, w, g, b)


def _bf16_pieces(x, n=3):
    out, rest = [], np.asarray(x, np.float32)
    for _ in range(n):
        piece = rest.astype(BF16).astype(np.float32)
        out.append(piece)
        rest = (rest - piece).astype(np.float32)
    return out


def _nsa_tables(S):
    nhb = S // CMP_STRIDE
    n_sel = S // SEL_BLOCK
    heads = np.arange(1, N_HEADS + 1, dtype=np.float32)
    slope = (2.0 ** (-8.0 * heads / N_HEADS)).astype(np.float32)
    slopes = np.zeros((N_KV_GROUPS, 8, LANES), np.float32)
    slopes[:, :HEADS_PER_GROUP, :] = slope.reshape(N_KV_GROUPS, HEADS_PER_GROUP, 1)
    c0 = np.arange(nhb)[None, :] * CMP_STRIDE
    j0 = np.arange(SEL_ROWS)[:, None] * SEL_BLOCK
    ovt = np.maximum(np.minimum(c0 + CMP_BLOCK, j0 + SEL_BLOCK) - np.maximum(c0, j0), 0).astype(np.float32) / CMP_BLOCK
    ovt[:, nhb - 1] = 0.0
    ovt[n_sel:, :] = 0.0
    qfeat = np.zeros((N_KV_GROUPS, 8, LANES), np.float32)
    pieces = _bf16_pieces(slope)
    for t, piece in enumerate(pieces):
        qfeat[:, :HEADS_PER_GROUP, SEL_ROWS + t] = piece.reshape(N_KV_GROUPS, HEADS_PER_GROUP)
        qfeat[:, :HEADS_PER_GROUP, SEL_ROWS + 3 + t] = piece.reshape(N_KV_GROUPS, HEADS_PER_GROUP)
    kpos = np.arange(S)
    kx = np.zeros((S, LANES), np.float32)
    kx[kpos, kpos // SEL_BLOCK] = 1.0
    kx[:, SEL_ROWS:SEL_ROWS + 3] = (SEL_BLOCK * (kpos // SEL_BLOCK))[:, None]
    kx[:, SEL_ROWS + 3:SEL_ROWS + 6] = (kpos % SEL_BLOCK)[:, None]
    return jnp.asarray(slopes), jnp.asarray(ovt, dtype=BF16), jnp.asarray(qfeat), jnp.asarray(kx, dtype=BF16)


def _nsa_layer_res_ln(hf, hb, B, kv_w, cmp_pos, cmp_w1, cmp_w2, w_qg, w_o, g, b):
    T, D = hf.shape
    S = T // B
    hd = N_HEADS * HEAD_DIM
    gsz = N_KV_GROUPS * HEAD_DIM
    slopes, ovt, qfeat, kx = _nsa_tables(S)

    kvw = kv_w.astype(BF16)
    kvc = _proj_heads(hb, kvw[:, :2 * gsz], F32)
    kv_t = _proj_heads(hb, kvw[:, 2 * gsz:], BF16)
    wq = w_qg[:, :hd].astype(BF16)
    wgt = jnp.zeros((D, LANES), F32).at[:, :N_BRANCHES * N_HEADS].set(w_qg[:, hd:]).astype(BF16)
    q_t = _proj_heads(hb, wq, BF16, scale=HEAD_DIM ** -0.5)
    gates = _gates(hb, wgt)

    pos = cmp_pos.reshape(2, 2, CMP_STRIDE * HEAD_DIM)
    kvcmp = _cmp_mlp(kvc, pos, cmp_w1.astype(BF16), cmp_w2.astype(BF16), B)
    o_cmp, q_aug = _cmp_attn(q_t, kvcmp, slopes, ovt, qfeat, B)
    kx_all = jnp.broadcast_to(jnp.tile(kx, (B, 1))[None], (N_KV_GROUPS, T, LANES))
    k_aug = jnp.concatenate([kv_t[:N_KV_GROUPS], kx_all], axis=-1)
    k_aug_t = k_aug.reshape(N_KV_GROUPS, T // SEL_KEY_TILE, SEL_KEY_TILE, 2 * HEAD_DIM).transpose(0, 1, 3, 2)
    ones_col = jnp.zeros((1, 1, LANES), BF16).at[0, 0, 0].set(1.0)
    v_aug = jnp.concatenate([kv_t[N_KV_GROUPS:2 * N_KV_GROUPS],
                             jnp.broadcast_to(ones_col, (N_KV_GROUPS, T, LANES))], axis=-1)
    o_sel = _sel_attn(q_aug, k_aug_t, v_aug, B)
    win_tq = 256
    lane_id = jnp.arange(LANES)[None, None, :]
    kw_aug = jnp.concatenate([kv_t[2 * N_KV_GROUPS:3 * N_KV_GROUPS],
                              jnp.where(lane_id < SEL_ROWS, jnp.zeros((), BF16), kx_all)], axis=-1)
    kw_aug_t = kw_aug.reshape(N_KV_GROUPS, T // win_tq, win_tq, 2 * HEAD_DIM).transpose(0, 1, 3, 2)
    vw_aug = jnp.concatenate([kv_t[3 * N_KV_GROUPS:],
                              jnp.broadcast_to(ones_col, (N_KV_GROUPS, T, LANES))], axis=-1)
    o_win = _win_attn(q_aug, kw_aug_t, vw_aug, B, tq=win_tq)
    return _attn_out_res_ln(o_cmp, o_sel, o_win, gates, hf, w_o.astype(BF16), g, b)


def kernel(x, conv_w_pw1, conv_b_pw1, conv_w_dw, conv_b_dw, conv_ln_g, conv_ln_b, conv_w_pw2, kv_w, cmp_pos, cmp_w1, cmp_w2, nsa_w_qg, nsa_w_o, moe_wg, moe_bg, moe_we, moe_be, moe_w_gu, moe_w_down, ln_g, ln_b):
    B, S, D = x.shape
    T = B * S
    hf = x.reshape(T, D)
    hb = hf.astype(BF16)
    r = lambda v: v.reshape(1, -1)
    for l in range(DEPTH):
        if l < N_A_LAYERS:
            glu = _pw1_glu(hb, conv_w_pw1[l].astype(BF16), r(conv_b_pw1[l]))
            conv = _dwconv(glu.reshape(B, S, D), conv_w_dw[l], r(conv_b_dw[l])).reshape(T, D)
            hf, hp = _pw2_res_ln(conv, hf, conv_w_pw2[l].astype(BF16), r(conv_ln_g[l]), r(conv_ln_b[l]),
                                 r(ln_g[l, 0]), r(ln_b[l, 0]))
        else:
            j = l - N_A_LAYERS
            hf, hp = _nsa_layer_res_ln(hf, hb, B, kv_w, cmp_pos, cmp_w1, cmp_w2, nsa_w_qg[j], nsa_w_o[j],
                                       r(ln_g[l, 0]), r(ln_b[l, 0]))
        hf, hb = _hier_moe_res_ln(hf, hp, moe_wg[l], moe_bg[l], moe_we[l], moe_be[l], moe_w_gu, moe_w_down, l,
                                  r(ln_g[l, 1]), r(ln_b[l, 1]))
    return hf.reshape(B, S, D)
```

```python
import functools

import numpy as np
import jax
import jax.numpy as jnp
from jax import lax
from jax.experimental import pallas as pl
from jax.experimental.pallas import tpu as pltpu

F32 = jnp.float32
BF16 = jnp.bfloat16
I32 = jnp.int32
U32 = jnp.uint32

DEPTH = 2
N_A_LAYERS = DEPTH // 2
DEEPNORM_ALPHA = (2.0 * DEPTH) ** 0.25
LN_EPS = 1e-5
CONV_WIDTH = 31
N_HEADS = 16
HEAD_DIM = 128
N_KV_GROUPS = 4
HEADS_PER_GROUP = N_HEADS // N_KV_GROUPS
CMP_BLOCK = 32
CMP_STRIDE = 16
SEL_BLOCK = 64
SEL_TOPK = 16
WINDOW = 512
N_BRANCHES = 3
MASK_VALUE = -1e30
N_GROUPS = 4
EXPERTS_PER_GROUP = 8
N_EXPERTS = N_GROUPS * EXPERTS_PER_GROUP
MOE_CHUNK = 128

LANES = 128
SUBLANES = 8
VMEM_LIMIT = 56 * 1024 * 1024

CONV_HALO = 32
SEL_ROWS = 64
SEL_KEY_TILE = 512
W_DMA_SPLIT = 4


def _cparams(sem, vmem=None):
    return pltpu.CompilerParams(dimension_semantics=sem, vmem_limit_bytes=vmem)


def _sigmoid(x):
    return 1.0 / (1.0 + jnp.exp(-x))


def _layer_norm(x, g, b):
    mu = jnp.mean(x, axis=-1, keepdims=True)
    xc = x - mu
    var = jnp.mean(xc * xc, axis=-1, keepdims=True)
    return xc * lax.rsqrt(var + LN_EPS) * g + b


def _pack_halves(x):
    n = x.shape[-1] // 2
    lo = lax.bitcast_convert_type(x[:, :n].astype(BF16).astype(F32), U32)
    hi = lax.bitcast_convert_type(x[:, n:].astype(BF16).astype(F32), U32)
    return lax.shift_right_logical(lo, jnp.uint32(16)) | hi


def _unpack_halves(p):
    lo = lax.bitcast_convert_type(lax.shift_left(p, jnp.uint32(16)), F32)
    hi = lax.bitcast_convert_type(p & jnp.uint32(0xFFFF0000), F32)
    return lo, hi


def _dot(a, b):
    return jnp.dot(a, b, preferred_element_type=F32)


def _dot_nt(a, b):
    return lax.dot_general(a, b, (((1,), (1,)), ((), ())), preferred_element_type=F32)


def _pw1_glu_kernel(x_ref, wa_ref, wg_ref, ba_ref, bg_ref, o_ref):
    x = x_ref[...]
    a = _dot(x, wa_ref[...]) + ba_ref[...]
    g = _dot(x, wg_ref[...]) + bg_ref[...]
    o_ref[...] = a * _sigmoid(g)


def _pw1_glu(xb, w, b, tm=1024, tn=512):
    T, D = xb.shape
    nj = D // tn
    return pl.pallas_call(
        _pw1_glu_kernel,
        grid=(nj, T // tm),
        in_specs=[
            pl.BlockSpec((tm, D), lambda j, i: (i, 0)),
            pl.BlockSpec((D, tn), lambda j, i: (0, j)),
            pl.BlockSpec((D, tn), lambda j, i: (0, j + nj)),
            pl.BlockSpec((1, tn), lambda j, i: (0, j)),
            pl.BlockSpec((1, tn), lambda j, i: (0, j + nj)),
        ],
        out_specs=pl.BlockSpec((tm, tn), lambda j, i: (i, j)),
        out_shape=jax.ShapeDtypeStruct((T, D), F32),
        compiler_params=_cparams(("parallel", "parallel"), VMEM_LIMIT),
        name="pw1_glu",
    )(xb, w, w, b, b)


def _dwconv_kernel(x_ref, halo_ref, w_ref, b_ref, o_ref, xsh, *, ts, tc, rc):
    i = pl.program_id(1)
    xsh[0, 0:CONV_HALO, :] = jnp.where(i > 0, halo_ref[0], 0.0)
    xsh[0, CONV_HALO:CONV_HALO + ts, :] = x_ref[0]
    span = ts + CONV_HALO - SUBLANES
    for p in range(1, SUBLANES):
        xsh[p, 0:span, :] = xsh[0, p:p + span, :]
    off = CONV_HALO - (CONV_WIDTH - 1)
    for c0 in range(0, tc, LANES):
        for r0 in range(0, ts, rc):
            acc = jnp.broadcast_to(b_ref[:, c0:c0 + LANES], (rc, LANES))
            for k in range(CONV_WIDTH):
                p = (k + off) % SUBLANES
                base = r0 + k + off - p
                acc = acc + w_ref[k:k + 1, c0:c0 + LANES] * xsh[p, base:base + rc, c0:c0 + LANES]
            o_ref[0, r0:r0 + rc, c0:c0 + LANES] = acc


def _dwconv(x, w, b, ts=512, tc=256, rc=128):
    B, S, D = x.shape
    hb = ts // CONV_HALO
    return pl.pallas_call(
        functools.partial(_dwconv_kernel, ts=ts, tc=tc, rc=rc),
        grid=(B, S // ts, D // tc),
        in_specs=[
            pl.BlockSpec((1, ts, tc), lambda bb, i, c: (bb, i, c)),
            pl.BlockSpec((1, CONV_HALO, tc), lambda bb, i, c: (bb, jnp.maximum(i * hb - 1, 0), c)),
            pl.BlockSpec((CONV_WIDTH, tc), lambda bb, i, c: (0, c)),
            pl.BlockSpec((1, tc), lambda bb, i, c: (0, c)),
        ],
        out_specs=pl.BlockSpec((1, ts, tc), lambda bb, i, c: (bb, i, c)),
        out_shape=jax.ShapeDtypeStruct((B, S, D), F32),
        scratch_shapes=[pltpu.VMEM((SUBLANES, CONV_HALO + ts, tc), F32)],
        compiler_params=_cparams(("parallel", "parallel", "parallel")),
        name="dwconv",
    )(x, x, w, b)


def _pw2_kernel(c_ref, x_ref, w_ref, cg_ref, cb_ref, g_ref, b_ref, of_ref, op_ref):
    y = _layer_norm(c_ref[...], cg_ref[...], cb_ref[...])
    y = y * _sigmoid(y)
    mix = _dot(y.astype(BF16), w_ref[...])
    h = _layer_norm(DEEPNORM_ALPHA * x_ref[...] + mix, g_ref[...], b_ref[...])
    of_ref[...] = h
    op_ref[...] = _pack_halves(h)


def _pw2_res_ln(conv, x, w, cg, cb, g, b, tm=256):
    T, D = x.shape
    row = lambda i: (i, 0)
    fix = lambda i: (0, 0)
    return pl.pallas_call(
        _pw2_kernel,
        grid=(T // tm,),
        in_specs=[
            pl.BlockSpec((tm, D), row), pl.BlockSpec((tm, D), row),
            pl.BlockSpec((D, D), fix),
            pl.BlockSpec((1, D), fix), pl.BlockSpec((1, D), fix),
            pl.BlockSpec((1, D), fix), pl.BlockSpec((1, D), fix),
        ],
        out_specs=[pl.BlockSpec((tm, D), row), pl.BlockSpec((tm, D // 2), row)],
        out_shape=[jax.ShapeDtypeStruct((T, D), F32), jax.ShapeDtypeStruct((T, D // 2), U32)],
        compiler_params=_cparams(("parallel",), VMEM_LIMIT),
        name="pw2_res_ln",
    )(conv, x, w, cg, cb, g, b)


def _router_kernel(h_ref, w_ref, b_ref, ri_ref, rw_ref, cnt_ref, *, tm):
    s = pl.program_id(0)

    @pl.when(s == 0)
    def _():
        cnt_ref[...] = jnp.zeros_like(cnt_ref)

    h = h_ref[...]
    w = w_ref[...]
    h_hi = h.astype(BF16)
    h_lo = (h - h_hi.astype(F32)).astype(BF16)
    w_hi = w.astype(BF16)
    w_lo = (w - w_hi.astype(F32)).astype(BF16)
    lg = _dot(h_hi, w_hi) + _dot(h_lo, w_hi) + _dot(h_hi, w_lo) + b_ref[...]

    lane = lax.broadcasted_iota(I32, (tm, LANES), 1)
    lanef = lane.astype(F32)
    neg = jnp.float32(-jnp.inf)
    big = jnp.float32(1e9)

    is_g = lane < N_GROUPS
    lgm = jnp.where(is_g, lg, neg)
    mg = jnp.max(lgm, axis=-1, keepdims=True)
    grp = jnp.min(jnp.where(lgm == mg, lanef, big), axis=-1, keepdims=True)
    sg = jnp.sum(jnp.where(is_g, jnp.exp(lg - mg), 0.0), axis=-1, keepdims=True)
    gw = 1.0 / sg

    lo = N_GROUPS + grp * EXPERTS_PER_GROUP
    in_g = (lanef >= lo) & (lanef < lo + EXPERTS_PER_GROUP)
    le = jnp.where(in_g, lg, neg)
    m1 = jnp.max(le, axis=-1, keepdims=True)
    i1 = jnp.min(jnp.where(le == m1, lanef, big), axis=-1, keepdims=True)
    le2 = jnp.where(lanef == i1, neg, le)
    m2 = jnp.max(le2, axis=-1, keepdims=True)
    i2 = jnp.min(jnp.where(le2 == m2, lanef, big), axis=-1, keepdims=True)
    t = jnp.exp(m2 - m1)
    p1 = 1.0 / (1.0 + t)
    w1 = p1 * gw
    w2 = (t * p1) * gw
    e1 = i1 - N_GROUPS
    e2 = i2 - N_GROUPS

    oh1 = (lanef == e1)
    oh2 = (lanef == e2)
    oh = (oh1 | oh2).astype(F32)
    r_i = lax.broadcasted_iota(I32, (tm, tm), 0)
    c_i = lax.broadcasted_iota(I32, (tm, tm), 1)
    tri = (c_i < r_i).astype(BF16)
    cs = _dot(tri, oh.astype(BF16)) + cnt_ref[...].astype(F32)
    rank1 = jnp.sum(jnp.where(oh1, cs, 0.0), axis=-1, keepdims=True)
    rank2 = jnp.sum(jnp.where(oh2, cs, 0.0), axis=-1, keepdims=True)
    cnt_ref[...] = cnt_ref[...] + jnp.sum(oh, axis=0, keepdims=True).astype(I32)

    ri = jnp.where(lane == 0, e1, jnp.where(lane == 1, e2, jnp.where(lane == 2, rank1, jnp.where(lane == 3, rank2, 0.0))))
    ri_ref[...] = ri.astype(I32)
    rw_ref[...] = jnp.where(lane == 0, w1, jnp.where(lane == 1, w2, 0.0))


def _router(h, wcat, bcat, tm=256):
    T, D = h.shape
    return pl.pallas_call(
        functools.partial(_router_kernel, tm=tm),
        grid=(T // tm,),
        in_specs=[
            pl.BlockSpec((tm, D), lambda i: (i, 0)),
            pl.BlockSpec((D, LANES), lambda i: (0, 0)),
            pl.BlockSpec((1, LANES), lambda i: (0, 0)),
        ],
        out_specs=[
            pl.BlockSpec((tm, LANES), lambda i: (i, 0)),
            pl.BlockSpec((tm, LANES), lambda i: (i, 0)),
            pl.BlockSpec((1, LANES), lambda i: (0, 0)),
        ],
        out_shape=[
            jax.ShapeDtypeStruct((T, LANES), I32),
            jax.ShapeDtypeStruct((T, LANES), F32),
            jax.ShapeDtypeStruct((1, LANES), I32),
        ],
        compiler_params=_cparams(("arbitrary",)),
        name="moe_router",
    )(h, wcat, bcat)


def _dispatch_kernel(dest_ref, zs_ref, zc_ref, nu_ref, x_ref, xs_hbm, zblk, sem, *, tm, n_tok, n_chunks):
    s = pl.program_id(0)

    def row_copy(r, d):
        return pltpu.make_async_copy(x_ref.at[pl.ds(r, 1)], xs_hbm.at[pl.ds(d, 1)], sem)

    def zero_row_copy(d):
        return pltpu.make_async_copy(zblk.at[pl.ds(0, 1)], xs_hbm.at[pl.ds(d, 1)], sem)

    def zero_chunk_copy(c):
        return pltpu.make_async_copy(zblk, xs_hbm.at[pl.ds(pl.multiple_of(c * MOE_CHUNK, MOE_CHUNK), MOE_CHUNK)], sem)

    @pl.when(s == 0)
    def _():
        zblk[...] = jnp.zeros_like(zblk)

        def fill(start):
            def per_expert(e, c):
                def per_row(r, c2):
                    cp = zero_row_copy(zs_ref[e] + r)
                    if start:
                        cp.start()
                    else:
                        cp.wait()
                    return c2
                lax.fori_loop(0, zc_ref[e], per_row, 0)
                return c
            lax.fori_loop(0, N_EXPERTS, per_expert, 0)

            def per_chunk(c, c2):
                cp = zero_chunk_copy(c)
                if start:
                    cp.start()
                else:
                    cp.wait()
                return c2
            lax.fori_loop(nu_ref[0], n_chunks, per_chunk, 0)

        fill(True)
        fill(False)

    base = s * tm

    def issue(r, c):
        t = base + r
        row_copy(r, dest_ref[t]).start(priority=0)
        row_copy(r, dest_ref[n_tok + t]).start(priority=1)
        return c

    lax.fori_loop(0, tm, issue, 0, unroll=8)

    def drain(r, c):
        row_copy(0, 0).wait()
        row_copy(0, 0).wait()
        return c

    lax.fori_loop(0, tm, drain, 0, unroll=8)


def _dispatch(x, dest_flat, zstart, zcount, n_used, n_rows, tm=256):
    T, W = x.shape
    return pl.pallas_call(
        functools.partial(_dispatch_kernel, tm=tm, n_tok=T, n_chunks=n_rows // MOE_CHUNK),
        grid_spec=pltpu.PrefetchScalarGridSpec(
            num_scalar_prefetch=4,
            grid=(T // tm,),
            in_specs=[pl.BlockSpec((tm, W), lambda i, *_: (i, 0))],
            out_specs=pl.BlockSpec(memory_space=pl.ANY),
            scratch_shapes=[pltpu.VMEM((MOE_CHUNK, W), x.dtype), pltpu.SemaphoreType.DMA(())],
        ),
        out_shape=jax.ShapeDtypeStruct((n_rows, W), x.dtype),
        compiler_params=_cparams(("arbitrary",)),
        name="moe_dispatch",
    )(dest_flat, zstart, zcount, n_used, x)


def _moe_mlp_kernel(ce_ref, nu_ref, first_ref, nxt_ref, x_ref, wgu_hbm, wd_hbm, o_ref,
                    wgu_stage, wd_stage, wgu_bf, wd_bf, sem, *, F, layer):
    c = pl.program_id(0)

    def w_copies(e):
        copies = []
        for ref, stage, s_idx in ((wgu_hbm, wgu_stage, 0), (wd_hbm, wd_stage, 1)):
            rows = stage.shape[0] // W_DMA_SPLIT
            for part in range(W_DMA_SPLIT):
                rs = pl.ds(part * rows, rows)
                copies.append(pltpu.make_async_copy(ref.at[layer, e, rs], stage.at[rs], sem.at[s_idx]))
        return copies

    def start_all(e):
        for n, cp in enumerate(w_copies(e)):
            cp.start(priority=n % 2)

    @pl.when(c < nu_ref[0])
    def _():
        @pl.when(first_ref[c] == 1)
        def _():
            @pl.when(c == 0)
            def _():
                start_all(ce_ref[0])

            for cp in w_copies(ce_ref[c]):
                cp.wait()
            wgu_bf[...] = wgu_stage[...].astype(BF16)
            wd_bf[...] = wd_stage[...].astype(BF16)

            @pl.when(nxt_ref[c] >= 0)
            def _():
                start_all(nxt_ref[c])

        x_lo, x_hi = _unpack_halves(x_ref[...])
        half = x_lo.shape[-1]
        gu = _dot(x_lo.astype(BF16), wgu_bf[0:half, :]) + _dot(x_hi.astype(BF16), wgu_bf[half:2 * half, :])
        g = gu[:, :F]
        u = gu[:, F:]
        hmid = (g * _sigmoid(g)) * u
        o_ref[...] = _pack_halves(_dot(hmid.astype(BF16), wd_bf[...]))

    @pl.when(c >= nu_ref[0])
    def _():
        o_ref[...] = jnp.zeros_like(o_ref)


def _moe_mlp(xs, chunk_e, n_used, first, nxt, w_gu, w_down, layer):
    n_rows, W = xs.shape
    D = 2 * W
    F2 = w_gu.shape[-1]
    F = F2 // 2
    n_chunks = n_rows // MOE_CHUNK

    def x_map(c, ce, nu, fi, nx):
        return (jnp.minimum(c, nu[0] - 1), 0)

    return pl.pallas_call(
        functools.partial(_moe_mlp_kernel, F=F, layer=layer),
        grid_spec=pltpu.PrefetchScalarGridSpec(
            num_scalar_prefetch=4,
            grid=(n_chunks,),
            in_specs=[
                pl.BlockSpec((MOE_CHUNK, W), x_map),
                pl.BlockSpec(memory_space=pl.ANY),
                pl.BlockSpec(memory_space=pl.ANY),
            ],
            out_specs=pl.BlockSpec((MOE_CHUNK, W), lambda c, ce, nu, fi, nx: (c, 0)),
            scratch_shapes=[
                pltpu.VMEM((D, F2), F32), pltpu.VMEM((F, D), F32),
                pltpu.VMEM((D, F2), BF16), pltpu.VMEM((F, D), BF16),
                pltpu.SemaphoreType.DMA((2,)),
            ],
        ),
        out_shape=jax.ShapeDtypeStruct((n_rows, W), U32),
        compiler_params=_cparams(("arbitrary",), VMEM_LIMIT),
        name="moe_mlp",
    )(chunk_e, n_used, first, nxt, xs, w_gu, w_down)


def _combine_kernel(dest_ref, y_hbm, rw_ref, h_ref, g_ref, b_ref, of_ref, ob_ref, gbuf, sem, *, tm, nsteps):
    s = pl.program_id(0)

    def row_copy(src_row, slot, k, r):
        return pltpu.make_async_copy(y_hbm.at[pl.ds(src_row, 1)], gbuf.at[slot, k, pl.ds(r, 1)], sem.at[slot])

    def issue(step, slot):
        base = step * tm

        def body(r, c):
            t = base + r
            row_copy(dest_ref[t], slot, 0, r).start(priority=0)
            row_copy(dest_ref[tm * nsteps + t], slot, 1, r).start(priority=1)
            return c

        lax.fori_loop(0, tm, body, 0, unroll=8)

    @pl.when(s == 0)
    def _():
        issue(0, 0)

    @pl.when(s + 1 < nsteps)
    def _():
        issue(s + 1, (s + 1) % 2)

    slot = s % 2

    def drain(r, c):
        row_copy(0, slot, 0, 0).wait()
        row_copy(0, slot, 1, 0).wait()
        return c

    lax.fori_loop(0, tm, drain, 0, unroll=8)

    w0 = rw_ref[:, 0:1]
    w1 = rw_ref[:, 1:2]
    lo0, hi0 = _unpack_halves(gbuf[slot, 0])
    lo1, hi1 = _unpack_halves(gbuf[slot, 1])
    y = jnp.concatenate([w0 * lo0 + w1 * lo1, w0 * hi0 + w1 * hi1], axis=-1)
    h = _layer_norm(DEEPNORM_ALPHA * h_ref[...] + y, g_ref[...], b_ref[...])
    of_ref[...] = h
    ob_ref[...] = h.astype(BF16)


def _combine_res_ln(y_rows, dest_flat, rw, h, g, b, tm=256):
    T, D = h.shape
    nsteps = T // tm
    row = lambda i, d: (i, 0)
    fix = lambda i, d: (0, 0)
    return pl.pallas_call(
        functools.partial(_combine_kernel, tm=tm, nsteps=nsteps),
        grid_spec=pltpu.PrefetchScalarGridSpec(
            num_scalar_prefetch=1,
            grid=(nsteps,),
            in_specs=[
                pl.BlockSpec(memory_space=pl.ANY),
                pl.BlockSpec((tm, LANES), row),
                pl.BlockSpec((tm, D), row),
                pl.BlockSpec((1, D), fix), pl.BlockSpec((1, D), fix),
            ],
            out_specs=[pl.BlockSpec((tm, D), row), pl.BlockSpec((tm, D), row)],
            scratch_shapes=[pltpu.VMEM((2, 2, tm, D // 2), U32), pltpu.SemaphoreType.DMA((2,))],
        ),
        out_shape=[jax.ShapeDtypeStruct((T, D), F32), jax.ShapeDtypeStruct((T, D), BF16)],
        compiler_params=_cparams(("arbitrary",), VMEM_LIMIT),
        name="moe_combine",
    )(dest_flat, y_rows, rw, h, g, b)


def _hier_moe_res_ln(h, hp, wg, bg, we, be, w_gu, w_down, layer, g, b):
    T, D = h.shape
    wcat = jnp.zeros((D, LANES), F32).at[:, :N_GROUPS].set(wg).at[:, N_GROUPS:N_GROUPS + N_EXPERTS].set(we)
    bcat = jnp.zeros((1, LANES), F32).at[0, :N_GROUPS].set(bg).at[0, N_GROUPS:N_GROUPS + N_EXPERTS].set(be)
    ri, rw, cnt = _router(h, wcat, bcat)

    counts = cnt[0, :N_EXPERTS]
    padded = (counts + MOE_CHUNK - 1) // MOE_CHUNK * MOE_CHUNK
    pad_end = jnp.cumsum(padded)
    pad_start = pad_end - padded
    A = T * 2
    n_chunks = (A + N_EXPERTS * (MOE_CHUNK - 1) + MOE_CHUNK - 1) // MOE_CHUNK
    n_rows = n_chunks * MOE_CHUNK
    ri_t = ri[:, 0:4].T
    seg = jnp.zeros((2, T), I32)
    for e in range(N_EXPERTS):
        seg = jnp.where(ri_t[0:2] == e, pad_start[e], seg)
    dest_flat = (seg + ri_t[2:4]).reshape(-1).astype(I32)
    n_used = (pad_end[-1] // MOE_CHUNK).astype(I32)
    cidx = jnp.minimum(jnp.arange(n_chunks, dtype=I32), n_used - 1)
    chunk_e = jnp.sum((pad_end[None, :] <= (cidx * MOE_CHUNK)[:, None]).astype(I32), axis=1)
    chunk_e = jnp.minimum(chunk_e, N_EXPERTS - 1)
    zstart = (pad_start + counts).astype(I32)
    zcount = (padded - counts).astype(I32)
    carange = jnp.arange(n_chunks, dtype=I32)
    prev_e = jnp.concatenate([jnp.full((1,), -1, I32), chunk_e[:-1]])
    first = ((chunk_e != prev_e) & (carange < n_used)).astype(I32)
    run_end = (pad_end[chunk_e] // MOE_CHUNK).astype(I32)
    nxt = jnp.where(run_end < n_used, chunk_e[jnp.minimum(run_end, n_chunks - 1)], -1).astype(I32)

    n_used = n_used.reshape(1)
    xs = _dispatch(hp, dest_flat, zstart, zcount, n_used, n_rows)
    ys = _moe_mlp(xs, chunk_e, n_used, first, nxt, w_gu, w_down, layer)
    return _combine_res_ln(ys, dest_flat, rw, h, g, b)


def _proj_heads_kernel(x_ref, w_ref, o_ref, *, nh, scale):
    acc = _dot(x_ref[...], w_ref[...])
    if scale != 1.0:
        acc = acc * scale
    for c in range(nh):
        o_ref[c] = acc[:, c * HEAD_DIM:(c + 1) * HEAD_DIM].astype(o_ref.dtype)


def _proj_heads(xb, w, out_dtype, scale=1.0, tm=1024, tn=512):
    T, D = xb.shape
    N = w.shape[1]
    nh = tn // HEAD_DIM
    return pl.pallas_call(
        functools.partial(_proj_heads_kernel, nh=nh, scale=scale),
        grid=(N // tn, T // tm),
        in_specs=[
            pl.BlockSpec((tm, D), lambda j, i: (i, 0)),
            pl.BlockSpec((D, tn), lambda j, i: (0, j)),
        ],
        out_specs=pl.BlockSpec((nh, tm, HEAD_DIM), lambda j, i: (j, i, 0)),
        out_shape=jax.ShapeDtypeStruct((N // HEAD_DIM, T, HEAD_DIM), out_dtype),
        compiler_params=_cparams(("parallel", "parallel"), VMEM_LIMIT),
        name="proj_heads",
    )(xb, w)


def _gates_kernel(x_ref, w_ref, o_ref):
    o_ref[...] = _sigmoid(_dot(x_ref[...], w_ref[...]))


def _gates(xb, w, tm=512):
    T, D = xb.shape
    return pl.pallas_call(
        _gates_kernel,
        grid=(T // tm,),
        in_specs=[pl.BlockSpec((tm, D), lambda i: (i, 0)), pl.BlockSpec((D, LANES), lambda i: (0, 0))],
        out_specs=pl.BlockSpec((tm, LANES), lambda i: (i, 0)),
        out_shape=jax.ShapeDtypeStruct((T, LANES), F32),
        compiler_params=_cparams(("parallel",)),
        name="nsa_gates",
    )(xb, w)


def _cmp_mlp_kernel(x_ref, pos_ref, w1_ref, w2_ref, o_ref, *, nhb):
    x = x_ref[0, 0]
    half = x.shape[1]
    lo = (x + pos_ref[0, 0:1, :]).astype(BF16)
    hi = (x + pos_ref[0, 1:2, :]).astype(BF16)
    a_lo = _dot(lo, w1_ref[0, 0:half, :])
    a_hi = _dot(hi, w1_ref[0, half:2 * half, :])
    pre = a_lo + pltpu.roll(a_hi, nhb - 1, 0)
    hid = 0.5 * pre * (1.0 + jnp.tanh(0.7978845608028654 * (pre + 0.044715 * pre * pre * pre)))
    comp = _dot(hid.astype(BF16), w2_ref[0])
    row = lax.broadcasted_iota(I32, comp.shape, 0)
    o_ref[0, 0] = jnp.where(row < nhb - 1, comp, 0.0).astype(o_ref.dtype)


def _cmp_mlp(kvc, pos, w1, w2, B):
    R, T, dh = kvc.shape
    S = T // B
    nhb = S // CMP_STRIDE
    half = CMP_STRIDE * dh
    x4 = kvc.reshape(R, B, nhb, half)
    H = w1.shape[-1]
    return pl.pallas_call(
        functools.partial(_cmp_mlp_kernel, nhb=nhb),
        grid=(R, B),
        in_specs=[
            pl.BlockSpec((1, 1, nhb, half), lambda r, bb: (r, bb, 0, 0)),
            pl.BlockSpec((1, 2, half), lambda r, bb: (r // N_KV_GROUPS, 0, 0)),
            pl.BlockSpec((1, 2 * half, H), lambda r, bb: (r // N_KV_GROUPS, 0, 0)),
            pl.BlockSpec((1, H, dh), lambda r, bb: (r // N_KV_GROUPS, 0, 0)),
        ],
        out_specs=pl.BlockSpec((1, 1, nhb, dh), lambda r, bb: (r, bb, 0, 0)),
        out_shape=jax.ShapeDtypeStruct((R, B, nhb, dh), BF16),
        compiler_params=_cparams(("parallel", "parallel"), VMEM_LIMIT),
        name="cmp_mlp",
    )(x4, pos, w1, w2)


def _cmp_attn_kernel(q_ref, k_ref, v_ref, sl_ref, ovt_ref, qf_ref, o_ref, qa_ref, *, tq, nhb, n_sel, topk):
    i = pl.program_id(2)
    q0 = i * tq
    qpos = q0 + lax.broadcasted_iota(I32, (tq, 1), 0)
    n = lax.broadcasted_iota(I32, (1, nhb), 1)
    dist = qpos - (n * CMP_STRIDE + (CMP_BLOCK - 1))
    valid = (dist >= 0) & (n < nhb - 1)
    distf = dist.astype(F32)
    k = k_ref[0, 0]
    v = v_ref[0, 0]
    psum = jnp.zeros((tq, nhb), F32)
    for h in range(HEADS_PER_GROUP):
        s = _dot_nt(q_ref[h], k) - sl_ref[0, h:h + 1, 0:1] * distf
        s = jnp.where(valid, s, MASK_VALUE)
        m = jnp.max(s, axis=-1, keepdims=True)
        e = jnp.where(valid, jnp.exp(s - m), 0.0)
        l = jnp.sum(e, axis=-1, keepdims=True)
        p = e / jnp.maximum(l, 1e-30)
        o_ref[h] = _dot(p.astype(BF16), v).astype(o_ref.dtype)
        psum = psum + p

    ovt = ovt_ref[...]
    p_hi = psum.astype(BF16)
    p_lo = (psum - p_hi.astype(F32)).astype(BF16)
    imp_t = _dot_nt(ovt, p_hi) + _dot_nt(ovt, p_lo)

    blk = lax.broadcasted_iota(I32, (SEL_ROWS, tq), 0)
    cur = lax.shift_right_logical(q0 + lax.broadcasted_iota(I32, (1, tq), 1), 6)
    in_range = (blk <= cur) & (blk < n_sel)
    forced = in_range & ((blk == 0) | (blk == cur) | (blk == cur - 1))
    k_free = (topk - 1 - (cur >= 1).astype(I32) - (cur >= 2).astype(I32)).astype(F32)
    neg = jnp.float32(-jnp.inf)
    x = jnp.where(in_range & jnp.logical_not(forced), imp_t, neg)
    n_grp = (n_sel + 7) // 8
    xg = [x[8 * r:8 * r + 8, :] for r in range(n_grp)]
    rank_g = [jnp.zeros((8, tq), F32) for _ in range(n_grp)]
    sub = lax.broadcasted_iota(I32, (8, tq), 0)
    for c in range(n_sel):
        vc = jnp.broadcast_to(x[c:c + 1, :], (8, tq))
        for r in range(n_grp):
            if 8 * r > c:
                hit = jnp.where(vc >= xg[r], 1.0, 0.0)
            elif 8 * r + 7 <= c:
                hit = jnp.where(vc > xg[r], 1.0, 0.0)
            else:
                hit = jnp.where(sub > c - 8 * r, jnp.where(vc >= xg[r], 1.0, 0.0), jnp.where(vc > xg[r], 1.0, 0.0))
            rank_g[r] = rank_g[r] + hit
    rank = jnp.concatenate(rank_g + [jnp.zeros((SEL_ROWS - 8 * n_grp, tq), F32)] * (SEL_ROWS > 8 * n_grp), axis=0)
    selected = forced | ((x > neg) & (rank < k_free))
    bias_t = jnp.where(selected, 0.0, MASK_VALUE)
    bias = jnp.concatenate([bias_t, jnp.zeros((LANES - SEL_ROWS, tq), F32)], axis=0).T
    lane = lax.broadcasted_iota(I32, (tq, LANES), 1)
    for h in range(HEADS_PER_GROUP):
        qa_ref[h, :, 0:HEAD_DIM] = q_ref[h]
        qa_ref[h, :, HEAD_DIM:2 * HEAD_DIM] = jnp.where(lane < SEL_ROWS, bias, qf_ref[0, h:h + 1, :]).astype(BF16)


def _cmp_attn(q_t, kvcmp, slopes, ovt, qfeat, B, tq=512):
    NH, T, dh = q_t.shape
    S = T // B
    nq = S // tq
    nhb = kvcmp.shape[2]
    n_sel = S // SEL_BLOCK
    assert n_sel <= SEL_ROWS
    topk = min(SEL_TOPK, n_sel)
    qmap = lambda bb, g, i: (g, bb * nq + i, 0)
    return pl.pallas_call(
        functools.partial(_cmp_attn_kernel, tq=tq, nhb=nhb, n_sel=n_sel, topk=topk),
        grid=(B, N_KV_GROUPS, nq),
        in_specs=[
            pl.BlockSpec((HEADS_PER_GROUP, tq, dh), qmap),
            pl.BlockSpec((1, 1, nhb, dh), lambda bb, g, i: (g, bb, 0, 0)),
            pl.BlockSpec((1, 1, nhb, dh), lambda bb, g, i: (N_KV_GROUPS + g, bb, 0, 0)),
            pl.BlockSpec((1, 8, LANES), lambda bb, g, i: (g, 0, 0)),
            pl.BlockSpec((SEL_ROWS, nhb), lambda bb, g, i: (0, 0)),
            pl.BlockSpec((1, 8, LANES), lambda bb, g, i: (g, 0, 0)),
        ],
        out_specs=[
            pl.BlockSpec((HEADS_PER_GROUP, tq, dh), qmap),
            pl.BlockSpec((HEADS_PER_GROUP, tq, 2 * dh), qmap),
        ],
        out_shape=[
            jax.ShapeDtypeStruct((NH, T, dh), BF16),
            jax.ShapeDtypeStruct((NH, T, 2 * dh), BF16),
        ],
        compiler_params=_cparams(("parallel", "parallel", "parallel")),
        name="cmp_attn",
    )(q_t, kvcmp, kvcmp, slopes, ovt, qfeat)


def _sel_attn_kernel(q_ref, k_ref, v_ref, o_ref, m_scr, acc_scr, s_buf, *, tq, tk):
    i = pl.program_id(2)
    q0 = i * tq
    nkt = k_ref.shape[1]
    m_scr[...] = jnp.full_like(m_scr, MASK_VALUE)
    acc_scr[...] = jnp.zeros_like(acc_scr)
    qpos = q0 + lax.broadcasted_iota(I32, (tq, 1), 0)

    def scores_to(t, slot):
        ka_t = k_ref[0, jnp.minimum(t, nkt - 1)]
        causal = (t * tk + lax.broadcasted_iota(I32, (1, tk), 1)) <= qpos
        for h in range(HEADS_PER_GROUP):
            s_buf[slot, h] = jnp.where(causal, _dot(q_ref[h], ka_t), MASK_VALUE)

    def absorb_from(t, slot):
        k0 = pl.multiple_of(jnp.minimum(t, nkt - 1) * tk, tk)
        va = v_ref[0, pl.ds(k0, tk), :]
        half = tq // 2
        for h in range(HEADS_PER_GROUP):
            m_prev = m_scr[h]
            m_new = jnp.maximum(m_prev, jnp.max(s_buf[slot, h], axis=-1, keepdims=True))
            alpha = jnp.exp(m_prev - m_new)
            p = jnp.concatenate([jnp.exp(s_buf[slot, h, 0:half, :] - m_new[0:half]).astype(BF16),
                                 jnp.exp(s_buf[slot, h, half:tq, :] - m_new[half:tq]).astype(BF16)], axis=0)
            acc_scr[h] = alpha * acc_scr[h] + _dot(p, va)
            m_scr[h] = m_new

    n_tiles = q0 // tk + 1

    def tile_pair(pr, carry):
        t = 2 * pr
        scores_to(t + 1, 1)
        absorb_from(t, 0)

        @pl.when(t + 1 < n_tiles)
        def _():
            scores_to(t + 2, 0)
            absorb_from(t + 1, 1)

        return carry

    scores_to(0, 0)
    lax.fori_loop(0, (n_tiles + 1) // 2, tile_pair, 0)
    for h in range(HEADS_PER_GROUP):
        o_ref[h] = (acc_scr[h, :, 0:HEAD_DIM] / acc_scr[h, :, HEAD_DIM:HEAD_DIM + 1]).astype(o_ref.dtype)


def _sel_attn(q_aug, k_aug_t, v_aug, B, tq=256, tk=SEL_KEY_TILE):
    NH, T, dq = q_aug.shape
    dh = dq // 2
    S = T // B
    assert tk % tq == 0 and S % tk == 0
    nq = S // tq
    nkt = S // tk
    qmap = lambda bb, g, i: (g, bb * nq + i, 0)
    return pl.pallas_call(
        functools.partial(_sel_attn_kernel, tq=tq, tk=tk),
        grid=(B, N_KV_GROUPS, nq),
        in_specs=[
            pl.BlockSpec((HEADS_PER_GROUP, tq, dq), qmap),
            pl.BlockSpec((1, nkt, dq, tk), lambda bb, g, i: (g, bb, 0, 0)),
            pl.BlockSpec((1, S, dq), lambda bb, g, i: (g, bb, 0)),
        ],
        out_specs=pl.BlockSpec((HEADS_PER_GROUP, tq, dh), qmap),
        out_shape=jax.ShapeDtypeStruct((NH, T, dh), BF16),
        scratch_shapes=[
            pltpu.VMEM((HEADS_PER_GROUP, tq, 1), F32),
            pltpu.VMEM((HEADS_PER_GROUP, tq, dq), F32),
            pltpu.VMEM((2, HEADS_PER_GROUP, tq, tk), F32),
        ],
        compiler_params=_cparams(("parallel", "parallel", "parallel")),
        name="nsa_sel_attn",
    )(q_aug, k_aug_t, v_aug)


def _win_attn_kernel(q_ref, k_ref, v_ref, o_ref, *, tq):
    i = pl.program_id(2)
    q0 = i * tq
    back = WINDOW // tq
    qpos = q0 + lax.broadcasted_iota(I32, (tq, 1), 0)
    tiles = []
    for d in range(back, -1, -1):
        kt = i - d
        ktc = jnp.maximum(kt, 0)
        kpos = kt * tq + lax.broadcasted_iota(I32, (1, tq), 1)
        if d == back:
            valid = ((qpos - kpos) < WINDOW) & (kt >= 0)
        elif d == 0:
            valid = kpos <= qpos
        else:
            valid = kt >= 0
        tiles.append((k_ref[0, ktc], v_ref[0, pl.ds(pl.multiple_of(ktc * tq, tq), tq), :], valid))
    for h in range(HEADS_PER_GROUP):
        q = q_ref[h]
        s_all = [jnp.where(valid, _dot(q, k_t), MASK_VALUE) for k_t, _, valid in tiles]
        m = functools.reduce(jnp.maximum, [jnp.max(s, axis=-1, keepdims=True) for s in s_all])
        acc = functools.reduce(lambda a, b: a + b,
                               [_dot(jnp.exp(s - m).astype(BF16), va) for s, (_, va, _) in zip(s_all, tiles)])
        o_ref[h] = (acc[:, 0:HEAD_DIM] / acc[:, HEAD_DIM:HEAD_DIM + 1]).astype(o_ref.dtype)


def _win_attn(q_aug, k_aug_t, v_aug, B, tq=256):
    NH, T, dq = q_aug.shape
    dh = dq // 2
    S = T // B
    assert WINDOW % tq == 0 and S % tq == 0
    nq = S // tq
    qmap = lambda bb, g, i: (g, bb * nq + i, 0)
    return pl.pallas_call(
        functools.partial(_win_attn_kernel, tq=tq),
        grid=(B, N_KV_GROUPS, nq),
        in_specs=[
            pl.BlockSpec((HEADS_PER_GROUP, tq, dq), qmap),
            pl.BlockSpec((1, nq, dq, tq), lambda bb, g, i: (g, bb, 0, 0)),
            pl.BlockSpec((1, S, dq), lambda bb, g, i: (g, bb, 0)),
        ],
        out_specs=pl.BlockSpec((HEADS_PER_GROUP, tq, dh), qmap),
        out_shape=jax.ShapeDtypeStruct((NH, T, dh), BF16),
        compiler_params=_cparams(("parallel", "parallel", "parallel")),
        name="nsa_win_attn",
    )(q_aug, k_aug_t, v_aug)


def _attn_out_kernel(oc_ref, os_ref, ow_ref, gt_ref, ex_ref, x_ref, w_ref, g_ref, b_ref, of_ref, op_ref, lhs):
    gt = gt_ref[...]
    gt_hi = gt.astype(BF16)
    gt_lo = (gt - gt_hi.astype(F32)).astype(BF16)
    gt2 = jnp.concatenate([gt_hi, gt_lo], axis=-1)
    ge = [_dot(gt2, ex_ref[br]) for br in range(N_BRANCHES)]
    for hd in range(N_HEADS):
        cols = slice(hd * HEAD_DIM, (hd + 1) * HEAD_DIM)
        o = (ge[0][:, cols] * oc_ref[hd].astype(F32) + ge[1][:, cols] * os_ref[hd].astype(F32)
             + ge[2][:, cols] * ow_ref[hd].astype(F32))
        lhs[:, cols] = o.astype(BF16)
    mix = _dot(lhs[...], w_ref[...])
    h = _layer_norm(DEEPNORM_ALPHA * x_ref[...] + mix, g_ref[...], b_ref[...])
    of_ref[...] = h
    op_ref[...] = _pack_halves(h)


def _attn_out_res_ln(o_cmp, o_sel, o_win, gates, x, w, g, b, tm=256):
    T, D = x.shape
    NH = o_cmp.shape[0]
    hmap = lambda i: (0, i, 0)
    row = lambda i: (i, 0)
    fix = lambda i: (0, 0)
    expand = np.zeros((N_BRANCHES, 2 * LANES, NH * HEAD_DIM), np.float32)
    for hd in range(NH):
        for br in range(N_BRANCHES):
            for piece in range(2):
                expand[br, piece * LANES + N_BRANCHES * hd + br, hd * HEAD_DIM:(hd + 1) * HEAD_DIM] = 1.0
    expand = jnp.asarray(expand, dtype=BF16)
    return pl.pallas_call(
        _attn_out_kernel,
        grid=(T // tm,),
        in_specs=[
            pl.BlockSpec((NH, tm, HEAD_DIM), hmap), pl.BlockSpec((NH, tm, HEAD_DIM), hmap),
            pl.BlockSpec((NH, tm, HEAD_DIM), hmap),
            pl.BlockSpec((tm, LANES), row),
            pl.BlockSpec((N_BRANCHES, 2 * LANES, NH * HEAD_DIM), lambda i: (0, 0, 0)),
            pl.BlockSpec((tm, D), row),
            pl.BlockSpec((D, D), fix),
            pl.BlockSpec((1, D), fix), pl.BlockSpec((1, D), fix),
        ],
        out_specs=[pl.BlockSpec((tm, D), row), pl.BlockSpec((tm, D // 2), row)],
        out_shape=[jax.ShapeDtypeStruct((T, D), F32), jax.ShapeDtypeStruct((T, D // 2), U32)],
        scratch_shapes=[pltpu.VMEM((tm, D), BF16)],
        compiler_params=_cparams(("parallel",), VMEM_LIMIT),
        name="attn_out_res_ln",
    )(o_cmp, o_sel, o_win, gates, expand, x, w, g, b)


def _bf16_pieces(x, n=3):
    out, rest = [], np.asarray(x, np.float32)
    for _ in range(n):
        piece = rest.astype(BF16).astype(np.float32)
        out.append(piece)
        rest = (rest - piece).astype(np.float32)
    return out


def _nsa_tables(S):
    nhb = S // CMP_STRIDE
    n_sel = S // SEL_BLOCK
    heads = np.arange(1, N_HEADS + 1, dtype=np.float32)
    slope = (2.0 ** (-8.0 * heads / N_HEADS)).astype(np.float32)
    slopes = np.zeros((N_KV_GROUPS, 8, LANES), np.float32)
    slopes[:, :HEADS_PER_GROUP, :] = slope.reshape(N_KV_GROUPS, HEADS_PER_GROUP, 1)
    c0 = np.arange(nhb)[None, :] * CMP_STRIDE
    j0 = np.arange(SEL_ROWS)[:, None] * SEL_BLOCK
    ovt = np.maximum(np.minimum(c0 + CMP_BLOCK, j0 + SEL_BLOCK) - np.maximum(c0, j0), 0).astype(np.float32) / CMP_BLOCK
    ovt[:, nhb - 1] = 0.0
    ovt[n_sel:, :] = 0.0
    qfeat = np.zeros((N_KV_GROUPS, 8, LANES), np.float32)
    pieces = _bf16_pieces(slope)
    for t, piece in enumerate(pieces):
        qfeat[:, :HEADS_PER_GROUP, SEL_ROWS + t] = piece.reshape(N_KV_GROUPS, HEADS_PER_GROUP)
        qfeat[:, :HEADS_PER_GROUP, SEL_ROWS + 3 + t] = piece.reshape(N_KV_GROUPS, HEADS_PER_GROUP)
    kpos = np.arange(S)
    kx = np.zeros((S, LANES), np.float32)
    kx[kpos, kpos // SEL_BLOCK] = 1.0
    kx[:, SEL_ROWS:SEL_ROWS + 3] = (SEL_BLOCK * (kpos // SEL_BLOCK))[:, None]
    kx[:, SEL_ROWS + 3:SEL_ROWS + 6] = (kpos % SEL_BLOCK)[:, None]
    return jnp.asarray(slopes), jnp.asarray(ovt, dtype=BF16), jnp.asarray(qfeat), jnp.asarray(kx, dtype=BF16)


def _nsa_layer_res_ln(hf, hb, B, kv_w, cmp_pos, cmp_w1, cmp_w2, w_qg, w_o, g, b):
    T, D = hf.shape
    S = T // B
    hd = N_HEADS * HEAD_DIM
    gsz = N_KV_GROUPS * HEAD_DIM
    slopes, ovt, qfeat, kx = _nsa_tables(S)

    kvw = kv_w.astype(BF16)
    kvc = _proj_heads(hb, kvw[:, :2 * gsz], F32)
    kv_t = _proj_heads(hb, kvw[:, 2 * gsz:], BF16)
    wq = w_qg[:, :hd].astype(BF16)
    wgt = jnp.zeros((D, LANES), F32).at[:, :N_BRANCHES * N_HEADS].set(w_qg[:, hd:]).astype(BF16)
    q_t = _proj_heads(hb, wq, BF16, scale=HEAD_DIM ** -0.5)
    gates = _gates(hb, wgt)

    pos = cmp_pos.reshape(2, 2, CMP_STRIDE * HEAD_DIM)
    kvcmp = _cmp_mlp(kvc, pos, cmp_w1.astype(BF16), cmp_w2.astype(BF16), B)
    o_cmp, q_aug = _cmp_attn(q_t, kvcmp, slopes, ovt, qfeat, B)
    kx_all = jnp.broadcast_to(jnp.tile(kx, (B, 1))[None], (N_KV_GROUPS, T, LANES))
    k_aug = jnp.concatenate([kv_t[:N_KV_GROUPS], kx_all], axis=-1)
    k_aug_t = k_aug.reshape(N_KV_GROUPS, T // SEL_KEY_TILE, SEL_KEY_TILE, 2 * HEAD_DIM).transpose(0, 1, 3, 2)
    ones_col = jnp.zeros((1, 1, LANES), BF16).at[0, 0, 0].set(1.0)
    v_aug = jnp.concatenate([kv_t[N_KV_GROUPS:2 * N_KV_GROUPS],
                             jnp.broadcast_to(ones_col, (N_KV_GROUPS, T, LANES))], axis=-1)
    o_sel = _sel_attn(q_aug, k_aug_t, v_aug, B)
    win_tq = 256
    lane_id = jnp.arange(LANES)[None, None, :]
    kw_aug = jnp.concatenate([kv_t[2 * N_KV_GROUPS:3 * N_KV_GROUPS],
                              jnp.where(lane_id < SEL_ROWS, jnp.zeros((), BF16), kx_all)], axis=-1)
    kw_aug_t = kw_aug.reshape(N_KV_GROUPS, T // win_tq, win_tq, 2 * HEAD_DIM).transpose(0, 1, 3, 2)
    vw_aug = jnp.concatenate([kv_t[3 * N_KV_GROUPS:],
                              jnp.broadcast_to(ones_col, (N_KV_GROUPS, T, LANES))], axis=-1)
    o_win = _win_attn(q_aug, kw_aug_t, vw_aug, B, tq=win_tq)
    return _attn_out_res_ln(o_cmp, o_sel, o_win, gates, hf, w_o.astype(BF16), g, b)


def kernel(x, conv_w_pw1, conv_b_pw1, conv_w_dw, conv_b_dw, conv_ln_g, conv_ln_b, conv_w_pw2, kv_w, cmp_pos, cmp_w1, cmp_w2, nsa_w_qg, nsa_w_o, moe_wg, moe_bg, moe_we, moe_be, moe_w_gu, moe_w_down, ln_g, ln_b):
    B, S, D = x.shape
    T = B * S
    hf = x.reshape(T, D)
    hb = hf.astype(BF16)
    r = lambda v: v.reshape(1, -1)
    for l in range(DEPTH):
        if l < N_A_LAYERS:
            glu = _pw1_glu(hb, conv_w_pw1[l].astype(BF16), r(conv_b_pw1[l]))
            conv = _dwconv(glu.reshape(B, S, D), conv_w_dw[l], r(conv_b_dw[l])).reshape(T, D)
            hf, hp = _pw2_res_ln(conv, hf, conv_w_pw2[l].astype(BF16), r(conv_ln_g[l]), r(conv_ln_b[l]),
                                 r(ln_g[l, 0]), r(ln_b[l, 0]))
        else:
            j = l - N_A_LAYERS
            hf, hp = _nsa_layer_res_ln(hf, hb, B, kv_w, cmp_pos, cmp_w1, cmp_w2, nsa_w_qg[j], nsa_w_o[j],
                                       r(ln_g[l, 0]), r(ln_b[l, 0]))
        hf, hb = _hier_moe_res_ln(hf, hp, moe_wg[l], moe_bg[l], moe_we[l], moe_be[l], moe_w_gu, moe_w_down, l,
                                  r(ln_g[l, 1]), r(ln_b[l, 1]))
    return hf.reshape(B, S, D)
```

```python
import functools

import numpy as np
import jax
import jax.numpy as jnp
from jax import lax
from jax.experimental import pallas as pl
from jax.experimental.pallas import tpu as pltpu

F32 = jnp.float32
BF16 = jnp.bfloat16
I32 = jnp.int32
U32 = jnp.uint32

DEPTH = 2
N_A_LAYERS = DEPTH // 2
DEEPNORM_ALPHA = (2.0 * DEPTH) ** 0.25
LN_EPS = 1e-5
CONV_WIDTH = 31
N_HEADS = 16
HEAD_DIM = 128
N_KV_GROUPS = 4
HEADS_PER_GROUP = N_HEADS // N_KV_GROUPS
CMP_BLOCK = 32
CMP_STRIDE = 16
SEL_BLOCK = 64
SEL_TOPK = 16
WINDOW = 512
N_BRANCHES = 3
MASK_VALUE = -1e30
N_GROUPS = 4
EXPERTS_PER_GROUP = 8
N_EXPERTS = N_GROUPS * EXPERTS_PER_GROUP
MOE_CHUNK = 128

LANES = 128
SUBLANES = 8
VMEM_LIMIT = 56 * 1024 * 1024

CONV_HALO = 32
SEL_ROWS = 64
SEL_KEY_TILE = 512
W_DMA_SPLIT = 4


def _cparams(sem, vmem=None):
    return pltpu.CompilerParams(dimension_semantics=sem, vmem_limit_bytes=vmem)


def _sigmoid(x):
    return 1.0 / (1.0 + jnp.exp(-x))


def _layer_norm(x, g, b):
    mu = jnp.mean(x, axis=-1, keepdims=True)
    xc = x - mu
    var = jnp.mean(xc * xc, axis=-1, keepdims=True)
    return xc * lax.rsqrt(var + LN_EPS) * g + b


def _pack_halves(x):
    n = x.shape[-1] // 2
    lo = lax.bitcast_convert_type(x[:, :n].astype(BF16).astype(F32), U32)
    hi = lax.bitcast_convert_type(x[:, n:].astype(BF16).astype(F32), U32)
    return lax.shift_right_logical(lo, jnp.uint32(16)) | hi


def _unpack_halves(p):
    lo = lax.bitcast_convert_type(lax.shift_left(p, jnp.uint32(16)), F32)
    hi = lax.bitcast_convert_type(p & jnp.uint32(0xFFFF0000), F32)
    return lo, hi


def _dot(a, b):
    return jnp.dot(a, b, preferred_element_type=F32)


def _dot_nt(a, b):
    return lax.dot_general(a, b, (((1,), (1,)), ((), ())), preferred_element_type=F32)


def _pw1_glu_kernel(x_ref, wa_ref, wg_ref, ba_ref, bg_ref, o_ref):
    x = x_ref[...]
    a = _dot(x, wa_ref[...]) + ba_ref[...]
    g = _dot(x, wg_ref[...]) + bg_ref[...]
    o_ref[...] = a * _sigmoid(g)


def _pw1_glu(xb, w, b, tm=1024, tn=512):
    T, D = xb.shape
    nj = D // tn
    return pl.pallas_call(
        _pw1_glu_kernel,
        grid=(nj, T // tm),
        in_specs=[
            pl.BlockSpec((tm, D), lambda j, i: (i, 0)),
            pl.BlockSpec((D, tn), lambda j, i: (0, j)),
            pl.BlockSpec((D, tn), lambda j, i: (0, j + nj)),
            pl.BlockSpec((1, tn), lambda j, i: (0, j)),
            pl.BlockSpec((1, tn), lambda j, i: (0, j + nj)),
        ],
        out_specs=pl.BlockSpec((tm, tn), lambda j, i: (i, j)),
        out_shape=jax.ShapeDtypeStruct((T, D), F32),
        compiler_params=_cparams(("parallel", "parallel"), VMEM_LIMIT),
        name="pw1_glu",
    )(xb, w, w, b, b)


def _dwconv_kernel(x_ref, halo_ref, w_ref, b_ref, o_ref, xsh, *, ts, tc, rc):
    i = pl.program_id(1)
    xsh[0, 0:CONV_HALO, :] = jnp.where(i > 0, halo_ref[0], 0.0)
    xsh[0, CONV_HALO:CONV_HALO + ts, :] = x_ref[0]
    span = ts + CONV_HALO - SUBLANES
    for p in range(1, SUBLANES):
        xsh[p, 0:span, :] = xsh[0, p:p + span, :]
    off = CONV_HALO - (CONV_WIDTH - 1)
    for c0 in range(0, tc, LANES):
        for r0 in range(0, ts, rc):
            acc = jnp.broadcast_to(b_ref[:, c0:c0 + LANES], (rc, LANES))
            for k in range(CONV_WIDTH):
                p = (k + off) % SUBLANES
                base = r0 + k + off - p
                acc = acc + w_ref[k:k + 1, c0:c0 + LANES] * xsh[p, base:base + rc, c0:c0 + LANES]
            o_ref[0, r0:r0 + rc, c0:c0 + LANES] = acc


def _dwconv(x, w, b, ts=512, tc=256, rc=128):
    B, S, D = x.shape
    hb = ts // CONV_HALO
    return pl.pallas_call(
        functools.partial(_dwconv_kernel, ts=ts, tc=tc, rc=rc),
        grid=(B, S // ts, D // tc),
        in_specs=[
            pl.BlockSpec((1, ts, tc), lambda bb, i, c: (bb, i, c)),
            pl.BlockSpec((1, CONV_HALO, tc), lambda bb, i, c: (bb, jnp.maximum(i * hb - 1, 0), c)),
            pl.BlockSpec((CONV_WIDTH, tc), lambda bb, i, c: (0, c)),
            pl.BlockSpec((1, tc), lambda bb, i, c: (0, c)),
        ],
        out_specs=pl.BlockSpec((1, ts, tc), lambda bb, i, c: (bb, i, c)),
        out_shape=jax.ShapeDtypeStruct((B, S, D), F32),
        scratch_shapes=[pltpu.VMEM((SUBLANES, CONV_HALO + ts, tc), F32)],
        compiler_params=_cparams(("parallel", "parallel", "parallel")),
        name="dwconv",
    )(x, x, w, b)


def _pw2_kernel(c_ref, x_ref, w_ref, cg_ref, cb_ref, g_ref, b_ref, of_ref, op_ref):
    y = _layer_norm(c_ref[...], cg_ref[...], cb_ref[...])
    y = y * _sigmoid(y)
    mix = _dot(y.astype(BF16), w_ref[...])
    h = _layer_norm(DEEPNORM_ALPHA * x_ref[...] + mix, g_ref[...], b_ref[...])
    of_ref[...] = h
    op_ref[...] = _pack_halves(h)


def _pw2_res_ln(conv, x, w, cg, cb, g, b, tm=256):
    T, D = x.shape
    row = lambda i: (i, 0)
    fix = lambda i: (0, 0)
    return pl.pallas_call(
        _pw2_kernel,
        grid=(T // tm,),
        in_specs=[
            pl.BlockSpec((tm, D), row), pl.BlockSpec((tm, D), row),
            pl.BlockSpec((D, D), fix),
            pl.BlockSpec((1, D), fix), pl.BlockSpec((1, D), fix),
            pl.BlockSpec((1, D), fix), pl.BlockSpec((1, D), fix),
        ],
        out_specs=[pl.BlockSpec((tm, D), row), pl.BlockSpec((tm, D // 2), row)],
        out_shape=[jax.ShapeDtypeStruct((T, D), F32), jax.ShapeDtypeStruct((T, D // 2), U32)],
        compiler_params=_cparams(("parallel",), VMEM_LIMIT),
        name="pw2_res_ln",
    )(conv, x, w, cg, cb, g, b)


def _router_kernel(h_ref, w_ref, b_ref, ri_ref, rw_ref, cnt_ref, *, tm):
    s = pl.program_id(0)

    @pl.when(s == 0)
    def _():
        cnt_ref[...] = jnp.zeros_like(cnt_ref)

    h = h_ref[...]
    w = w_ref[...]
    h_hi = h.astype(BF16)
    h_lo = (h - h_hi.astype(F32)).astype(BF16)
    w_hi = w.astype(BF16)
    w_lo = (w - w_hi.astype(F32)).astype(BF16)
    lg = _dot(h_hi, w_hi) + _dot(h_lo, w_hi) + _dot(h_hi, w_lo) + b_ref[...]

    lane = lax.broadcasted_iota(I32, (tm, LANES), 1)
    lanef = lane.astype(F32)
    neg = jnp.float32(-jnp.inf)
    big = jnp.float32(1e9)

    is_g = lane < N_GROUPS
    lgm = jnp.where(is_g, lg, neg)
    mg = jnp.max(lgm, axis=-1, keepdims=True)
    grp = jnp.min(jnp.where(lgm == mg, lanef, big), axis=-1, keepdims=True)
    sg = jnp.sum(jnp.where(is_g, jnp.exp(lg - mg), 0.0), axis=-1, keepdims=True)
    gw = 1.0 / sg

    lo = N_GROUPS + grp * EXPERTS_PER_GROUP
    in_g = (lanef >= lo) & (lanef < lo + EXPERTS_PER_GROUP)
    le = jnp.where(in_g, lg, neg)
    m1 = jnp.max(le, axis=-1, keepdims=True)
    i1 = jnp.min(jnp.where(le == m1, lanef, big), axis=-1, keepdims=True)
    le2 = jnp.where(lanef == i1, neg, le)
    m2 = jnp.max(le2, axis=-1, keepdims=True)
    i2 = jnp.min(jnp.where(le2 == m2, lanef, big), axis=-1, keepdims=True)
    t = jnp.exp(m2 - m1)
    p1 = 1.0 / (1.0 + t)
    w1 = p1 * gw
    w2 = (t * p1) * gw
    e1 = i1 - N_GROUPS
    e2 = i2 - N_GROUPS

    oh1 = (lanef == e1)
    oh2 = (lanef == e2)
    oh = (oh1 | oh2).astype(F32)
    r_i = lax.broadcasted_iota(I32, (tm, tm), 0)
    c_i = lax.broadcasted_iota(I32, (tm, tm), 1)
    tri = (c_i < r_i).astype(BF16)
    cs = _dot(tri, oh.astype(BF16)) + cnt_ref[...].astype(F32)
    rank1 = jnp.sum(jnp.where(oh1, cs, 0.0), axis=-1, keepdims=True)
    rank2 = jnp.sum(jnp.where(oh2, cs, 0.0), axis=-1, keepdims=True)
    cnt_ref[...] = cnt_ref[...] + jnp.sum(oh, axis=0, keepdims=True).astype(I32)

    ri = jnp.where(lane == 0, e1, jnp.where(lane == 1, e2, jnp.where(lane == 2, rank1, jnp.where(lane == 3, rank2, 0.0))))
    ri_ref[...] = ri.T[0:SUBLANES, :].astype(I32)
    rw_ref[...] = jnp.where(lane == 0, w1, jnp.where(lane == 1, w2, 0.0))


def _router(h, wcat, bcat, tm=256):
    T, D = h.shape
    return pl.pallas_call(
        functools.partial(_router_kernel, tm=tm),
        grid=(T // tm,),
        in_specs=[
            pl.BlockSpec((tm, D), lambda i: (i, 0)),
            pl.BlockSpec((D, LANES), lambda i: (0, 0)),
            pl.BlockSpec((1, LANES), lambda i: (0, 0)),
        ],
        out_specs=[
            pl.BlockSpec((SUBLANES, tm), lambda i: (0, i)),
            pl.BlockSpec((tm, LANES), lambda i: (i, 0)),
            pl.BlockSpec((1, LANES), lambda i: (0, 0)),
        ],
        out_shape=[
            jax.ShapeDtypeStruct((SUBLANES, T), I32),
            jax.ShapeDtypeStruct((T, LANES), F32),
            jax.ShapeDtypeStruct((1, LANES), I32),
        ],
        compiler_params=_cparams(("arbitrary",)),
        name="moe_router",
    )(h, wcat, bcat)


def _dispatch_kernel(dest_ref, zs_ref, zc_ref, nu_ref, x_ref, xs_hbm, zblk, sem, *, tm, n_tok, n_chunks):
    s = pl.program_id(0)

    def row_copy(r, d):
        return pltpu.make_async_copy(x_ref.at[pl.ds(r, 1)], xs_hbm.at[pl.ds(d, 1)], sem)

    def zero_row_copy(d):
        return pltpu.make_async_copy(zblk.at[pl.ds(0, 1)], xs_hbm.at[pl.ds(d, 1)], sem)

    def zero_chunk_copy(c):
        return pltpu.make_async_copy(zblk, xs_hbm.at[pl.ds(pl.multiple_of(c * MOE_CHUNK, MOE_CHUNK), MOE_CHUNK)], sem)

    @pl.when(s == 0)
    def _():
        zblk[...] = jnp.zeros_like(zblk)

        def fill(start):
            def per_expert(e, c):
                def per_row(r, c2):
                    cp = zero_row_copy(zs_ref[e] + r)
                    if start:
                        cp.start()
                    else:
                        cp.wait()
                    return c2
                lax.fori_loop(0, zc_ref[e], per_row, 0)
                return c
            lax.fori_loop(0, N_EXPERTS, per_expert, 0)

            def per_chunk(c, c2):
                cp = zero_chunk_copy(c)
                if start:
                    cp.start()
                else:
                    cp.wait()
                return c2
            lax.fori_loop(nu_ref[0], n_chunks, per_chunk, 0)

        fill(True)
        fill(False)

    base = s * tm

    def issue(r, c):
        t = base + r
        row_copy(r, dest_ref[t]).start(priority=0)
        row_copy(r, dest_ref[n_tok + t]).start(priority=1)
        return c

    lax.fori_loop(0, tm, issue, 0, unroll=8)

    def drain(r, c):
        row_copy(0, 0).wait()
        row_copy(0, 0).wait()
        return c

    lax.fori_loop(0, tm, drain, 0, unroll=8)


def _dispatch(x, dest_flat, zstart, zcount, n_used, n_rows, tm=256):
    T, W = x.shape
    return pl.pallas_call(
        functools.partial(_dispatch_kernel, tm=tm, n_tok=T, n_chunks=n_rows // MOE_CHUNK),
        grid_spec=pltpu.PrefetchScalarGridSpec(
            num_scalar_prefetch=4,
            grid=(T // tm,),
            in_specs=[pl.BlockSpec((tm, W), lambda i, *_: (i, 0))],
            out_specs=pl.BlockSpec(memory_space=pl.ANY),
            scratch_shapes=[pltpu.VMEM((MOE_CHUNK, W), x.dtype), pltpu.SemaphoreType.DMA(())],
        ),
        out_shape=jax.ShapeDtypeStruct((n_rows, W), x.dtype),
        compiler_params=_cparams(("arbitrary",)),
        name="moe_dispatch",
    )(dest_flat, zstart, zcount, n_used, x)


def _moe_mlp_kernel(ce_ref, nu_ref, first_ref, nxt_ref, x_ref, wgu_hbm, wd_hbm, o_ref,
                    wgu_stage, wd_stage, wgu_bf, wd_bf, sem, *, F, layer):
    c = pl.program_id(0)

    def w_copies(e):
        copies = []
        for ref, stage, s_idx in ((wgu_hbm, wgu_stage, 0), (wd_hbm, wd_stage, 1)):
            rows = stage.shape[0] // W_DMA_SPLIT
            for part in range(W_DMA_SPLIT):
                rs = pl.ds(part * rows, rows)
                copies.append(pltpu.make_async_copy(ref.at[layer, e, rs], stage.at[rs], sem.at[s_idx]))
        return copies

    def start_all(e):
        for cp in w_copies(e):
            cp.start(priority=1)

    @pl.when(c < nu_ref[0])
    def _():
        @pl.when(first_ref[c] == 1)
        def _():
            @pl.when(c == 0)
            def _():
                start_all(ce_ref[0])

            for cp in w_copies(ce_ref[c]):
                cp.wait()
            wgu_bf[...] = wgu_stage[...].astype(BF16)
            wd_bf[...] = wd_stage[...].astype(BF16)

            @pl.when(nxt_ref[c] >= 0)
            def _():
                start_all(nxt_ref[c])

        x_lo, x_hi = _unpack_halves(x_ref[...])
        half = x_lo.shape[-1]
        gu = _dot(x_lo.astype(BF16), wgu_bf[0:half, :]) + _dot(x_hi.astype(BF16), wgu_bf[half:2 * half, :])
        g = gu[:, :F]
        u = gu[:, F:]
        hmid = (g * _sigmoid(g)) * u
        o_ref[...] = _pack_halves(_dot(hmid.astype(BF16), wd_bf[...]))

    @pl.when(c >= nu_ref[0])
    def _():
        o_ref[...] = jnp.zeros_like(o_ref)


def _moe_mlp(xs, chunk_e, n_used, first, nxt, w_gu, w_down, layer):
    n_rows, W = xs.shape
    D = 2 * W
    F2 = w_gu.shape[-1]
    F = F2 // 2
    n_chunks = n_rows // MOE_CHUNK

    def x_map(c, ce, nu, fi, nx):
        return (jnp.minimum(c, nu[0] - 1), 0)

    return pl.pallas_call(
        functools.partial(_moe_mlp_kernel, F=F, layer=layer),
        grid_spec=pltpu.PrefetchScalarGridSpec(
            num_scalar_prefetch=4,
            grid=(n_chunks,),
            in_specs=[
                pl.BlockSpec((MOE_CHUNK, W), x_map),
                pl.BlockSpec(memory_space=pl.ANY),
                pl.BlockSpec(memory_space=pl.ANY),
            ],
            out_specs=pl.BlockSpec((MOE_CHUNK, W), lambda c, ce, nu, fi, nx: (c, 0)),
            scratch_shapes=[
                pltpu.VMEM((D, F2), F32), pltpu.VMEM((F, D), F32),
                pltpu.VMEM((D, F2), BF16), pltpu.VMEM((F, D), BF16),
                pltpu.SemaphoreType.DMA((2,)),
            ],
        ),
        out_shape=jax.ShapeDtypeStruct((n_rows, W), U32),
        compiler_params=_cparams(("arbitrary",), VMEM_LIMIT),
        name="moe_mlp",
    )(chunk_e, n_used, first, nxt, xs, w_gu, w_down)


def _combine_kernel(dest_ref, y_hbm, rw_ref, h_ref, g_ref, b_ref, of_ref, ob_ref, gbuf, sem, *, tm, nsteps):
    s = pl.program_id(0)

    def row_copy(src_row, slot, k, r):
        return pltpu.make_async_copy(y_hbm.at[pl.ds(src_row, 1)], gbuf.at[slot, k, pl.ds(r, 1)], sem.at[slot])

    def issue(step, slot):
        base = step * tm

        def body(r, c):
            t = base + r
            row_copy(dest_ref[t], slot, 0, r).start(priority=0)
            row_copy(dest_ref[tm * nsteps + t], slot, 1, r).start(priority=1)
            return c

        lax.fori_loop(0, tm, body, 0, unroll=8)

    @pl.when(s == 0)
    def _():
        issue(0, 0)

    @pl.when(s + 1 < nsteps)
    def _():
        issue(s + 1, (s + 1) % 2)

    slot = s % 2

    def drain(r, c):
        row_copy(0, slot, 0, 0).wait()
        row_copy(0, slot, 1, 0).wait()
        return c

    lax.fori_loop(0, tm, drain, 0, unroll=8)

    w0 = rw_ref[:, 0:1]
    w1 = rw_ref[:, 1:2]
    lo0, hi0 = _unpack_halves(gbuf[slot, 0])
    lo1, hi1 = _unpack_halves(gbuf[slot, 1])
    y = jnp.concatenate([w0 * lo0 + w1 * lo1, w0 * hi0 + w1 * hi1], axis=-1)
    h = _layer_norm(DEEPNORM_ALPHA * h_ref[...] + y, g_ref[...], b_ref[...])
    of_ref[...] = h
    ob_ref[...] = h.astype(BF16)


def _combine_res_ln(y_rows, dest_flat, rw, h, g, b, tm=256):
    T, D = h.shape
    nsteps = T // tm
    row = lambda i, d: (i, 0)
    fix = lambda i, d: (0, 0)
    return pl.pallas_call(
        functools.partial(_combine_kernel, tm=tm, nsteps=nsteps),
        grid_spec=pltpu.PrefetchScalarGridSpec(
            num_scalar_prefetch=1,
            grid=(nsteps,),
            in_specs=[
                pl.BlockSpec(memory_space=pl.ANY),
                pl.BlockSpec((tm, LANES), row),
                pl.BlockSpec((tm, D), row),
                pl.BlockSpec((1, D), fix), pl.BlockSpec((1, D), fix),
            ],
            out_specs=[pl.BlockSpec((tm, D), row), pl.BlockSpec((tm, D), row)],
            scratch_shapes=[pltpu.VMEM((2, 2, tm, D // 2), U32), pltpu.SemaphoreType.DMA((2,))],
        ),
        out_shape=[jax.ShapeDtypeStruct((T, D), F32), jax.ShapeDtypeStruct((T, D), BF16)],
        compiler_params=_cparams(("arbitrary",), VMEM_LIMIT),
        name="moe_combine",
    )(dest_flat, y_rows, rw, h, g, b)


def _hier_moe_res_ln(h, hp, wg, bg, we, be, w_gu, w_down, layer, g, b):
    T, D = h.shape
    wcat = jnp.zeros((D, LANES), F32).at[:, :N_GROUPS].set(wg).at[:, N_GROUPS:N_GROUPS + N_EXPERTS].set(we)
    bcat = jnp.zeros((1, LANES), F32).at[0, :N_GROUPS].set(bg).at[0, N_GROUPS:N_GROUPS + N_EXPERTS].set(be)
    ri, rw, cnt = _router(h, wcat, bcat)

    counts = cnt[0, :N_EXPERTS]
    padded = (counts + MOE_CHUNK - 1) // MOE_CHUNK * MOE_CHUNK
    pad_end = jnp.cumsum(padded)
    pad_start = pad_end - padded
    A = T * 2
    n_chunks = (A + N_EXPERTS * (MOE_CHUNK - 1) + MOE_CHUNK - 1) // MOE_CHUNK
    n_rows = n_chunks * MOE_CHUNK
    seg = jnp.zeros((2, T), I32)
    for e in range(N_EXPERTS):
        seg = jnp.where(ri[0:2] == e, pad_start[e], seg)
    dest_flat = (seg + ri[2:4]).reshape(-1).astype(I32)
    n_used = (pad_end[-1] // MOE_CHUNK).astype(I32)
    cidx = jnp.minimum(jnp.arange(n_chunks, dtype=I32), n_used - 1)
    chunk_e = jnp.sum((pad_end[None, :] <= (cidx * MOE_CHUNK)[:, None]).astype(I32), axis=1)
    chunk_e = jnp.minimum(chunk_e, N_EXPERTS - 1)
    zstart = (pad_start + counts).astype(I32)
    zcount = (padded - counts).astype(I32)
    carange = jnp.arange(n_chunks, dtype=I32)
    prev_e = jnp.concatenate([jnp.full((1,), -1, I32), chunk_e[:-1]])
    first = ((chunk_e != prev_e) & (carange < n_used)).astype(I32)
    run_end = (pad_end[chunk_e] // MOE_CHUNK).astype(I32)
    nxt = jnp.where(run_end < n_used, chunk_e[jnp.minimum(run_end, n_chunks - 1)], -1).astype(I32)

    n_used = n_used.reshape(1)
    xs = _dispatch(hp, dest_flat, zstart, zcount, n_used, n_rows)
    ys = _moe_mlp(xs, chunk_e, n_used, first, nxt, w_gu, w_down, layer)
    return _combine_res_ln(ys, dest_flat, rw, h, g, b)


def _proj_heads_kernel(x_ref, w_ref, o_ref, *, nh, scale):
    acc = _dot(x_ref[...], w_ref[...])
    if scale != 1.0:
        acc = acc * scale
    for c in range(nh):
        o_ref[c] = acc[:, c * HEAD_DIM:(c + 1) * HEAD_DIM].astype(o_ref.dtype)


def _proj_heads(xb, w, out_dtype, scale=1.0, tm=1024, tn=512):
    T, D = xb.shape
    N = w.shape[1]
    nh = tn // HEAD_DIM
    return pl.pallas_call(
        functools.partial(_proj_heads_kernel, nh=nh, scale=scale),
        grid=(N // tn, T // tm),
        in_specs=[
            pl.BlockSpec((tm, D), lambda j, i: (i, 0)),
            pl.BlockSpec((D, tn), lambda j, i: (0, j)),
        ],
        out_specs=pl.BlockSpec((nh, tm, HEAD_DIM), lambda j, i: (j, i, 0)),
        out_shape=jax.ShapeDtypeStruct((N // HEAD_DIM, T, HEAD_DIM), out_dtype),
        compiler_params=_cparams(("parallel", "parallel"), VMEM_LIMIT),
        name="proj_heads",
    )(xb, w)


def _gates_kernel(x_ref, w_ref, o_ref):
    o_ref[...] = _sigmoid(_dot(x_ref[...], w_ref[...]))


def _gates(xb, w, tm=512):
    T, D = xb.shape
    return pl.pallas_call(
        _gates_kernel,
        grid=(T // tm,),
        in_specs=[pl.BlockSpec((tm, D), lambda i: (i, 0)), pl.BlockSpec((D, LANES), lambda i: (0, 0))],
        out_specs=pl.BlockSpec((tm, LANES), lambda i: (i, 0)),
        out_shape=jax.ShapeDtypeStruct((T, LANES), F32),
        compiler_params=_cparams(("parallel",)),
        name="nsa_gates",
    )(xb, w)


def _cmp_mlp_kernel(x_ref, pos_ref, w1_ref, w2_ref, o_ref, *, nhb):
    x = x_ref[0, 0]
    half = x.shape[1]
    lo = (x + pos_ref[0, 0:1, :]).astype(BF16)
    hi = (x + pos_ref[0, 1:2, :]).astype(BF16)
    a_lo = _dot(lo, w1_ref[0, 0:half, :])
    a_hi = _dot(hi, w1_ref[0, half:2 * half, :])
    pre = a_lo + pltpu.roll(a_hi, nhb - 1, 0)
    hid = 0.5 * pre * (1.0 + jnp.tanh(0.7978845608028654 * (pre + 0.044715 * pre * pre * pre)))
    comp = _dot(hid.astype(BF16), w2_ref[0])
    row = lax.broadcasted_iota(I32, comp.shape, 0)
    o_ref[0, 0] = jnp.where(row < nhb - 1, comp, 0.0).astype(o_ref.dtype)


def _cmp_mlp(kvc, pos, w1, w2, B):
    R, T, dh = kvc.shape
    S = T // B
    nhb = S // CMP_STRIDE
    half = CMP_STRIDE * dh
    x4 = kvc.reshape(R, B, nhb, half)
    H = w1.shape[-1]
    return pl.pallas_call(
        functools.partial(_cmp_mlp_kernel, nhb=nhb),
        grid=(R, B),
        in_specs=[
            pl.BlockSpec((1, 1, nhb, half), lambda r, bb: (r, bb, 0, 0)),
            pl.BlockSpec((1, 2, half), lambda r, bb: (r // N_KV_GROUPS, 0, 0)),
            pl.BlockSpec((1, 2 * half, H), lambda r, bb: (r // N_KV_GROUPS, 0, 0)),
            pl.BlockSpec((1, H, dh), lambda r, bb: (r // N_KV_GROUPS, 0, 0)),
        ],
        out_specs=pl.BlockSpec((1, 1, nhb, dh), lambda r, bb: (r, bb, 0, 0)),
        out_shape=jax.ShapeDtypeStruct((R, B, nhb, dh), BF16),
        compiler_params=_cparams(("parallel", "parallel"), VMEM_LIMIT),
        name="cmp_mlp",
    )(x4, pos, w1, w2)


def _cmp_attn_kernel(q_ref, k_ref, v_ref, sl_ref, ovt_ref, qf_ref, o_ref, qa_ref, *, tq, nhb, n_sel, topk):
    i = pl.program_id(2)
    q0 = i * tq
    qpos = q0 + lax.broadcasted_iota(I32, (tq, 1), 0)
    n = lax.broadcasted_iota(I32, (1, nhb), 1)
    dist = qpos - (n * CMP_STRIDE + (CMP_BLOCK - 1))
    valid = (dist >= 0) & (n < nhb - 1)
    distf = dist.astype(F32)
    k = k_ref[0, 0]
    v = v_ref[0, 0]
    psum = jnp.zeros((tq, nhb), F32)
    for h in range(HEADS_PER_GROUP):
        s = _dot_nt(q_ref[h], k) - sl_ref[0, h:h + 1, 0:1] * distf
        s = jnp.where(valid, s, MASK_VALUE)
        m = jnp.max(s, axis=-1, keepdims=True)
        e = jnp.where(valid, jnp.exp(s - m), 0.0)
        l = jnp.sum(e, axis=-1, keepdims=True)
        p = e / jnp.maximum(l, 1e-30)
        o_ref[h] = _dot(p.astype(BF16), v).astype(o_ref.dtype)
        psum = psum + p

    ovt = ovt_ref[...]
    p_hi = psum.astype(BF16)
    p_lo = (psum - p_hi.astype(F32)).astype(BF16)
    imp_t = _dot_nt(ovt, p_hi) + _dot_nt(ovt, p_lo)

    blk = lax.broadcasted_iota(I32, (SEL_ROWS, tq), 0)
    cur = lax.shift_right_logical(q0 + lax.broadcasted_iota(I32, (1, tq), 1), 6)
    in_range = (blk <= cur) & (blk < n_sel)
    forced = in_range & ((blk == 0) | (blk == cur) | (blk == cur - 1))
    k_free = (topk - 1 - (cur >= 1).astype(I32) - (cur >= 2).astype(I32)).astype(F32)
    neg = jnp.float32(-jnp.inf)
    x = jnp.where(in_range & jnp.logical_not(forced), imp_t, neg)
    n_grp = (n_sel + 7) // 8
    xg = [x[8 * r:8 * r + 8, :] for r in range(n_grp)]
    rank_g = [jnp.zeros((8, tq), F32) for _ in range(n_grp)]
    sub = lax.broadcasted_iota(I32, (8, tq), 0)
    for c in range(n_sel):
        vc = jnp.broadcast_to(x[c:c + 1, :], (8, tq))
        for r in range(n_grp):
            if 8 * r > c:
                hit = jnp.where(vc >= xg[r], 1.0, 0.0)
            elif 8 * r + 7 <= c:
                hit = jnp.where(vc > xg[r], 1.0, 0.0)
            else:
                hit = jnp.where(sub > c - 8 * r, jnp.where(vc >= xg[r], 1.0, 0.0), jnp.where(vc > xg[r], 1.0, 0.0))
            rank_g[r] = rank_g[r] + hit
    rank = jnp.concatenate(rank_g + [jnp.zeros((SEL_ROWS - 8 * n_grp, tq), F32)] * (SEL_ROWS > 8 * n_grp), axis=0)
    selected = forced | ((x > neg) & (rank < k_free))
    bias_t = jnp.where(selected, 0.0, MASK_VALUE)
    bias = jnp.concatenate([bias_t, jnp.zeros((LANES - SEL_ROWS, tq), F32)], axis=0).T
    lane = lax.broadcasted_iota(I32, (tq, LANES), 1)
    for h in range(HEADS_PER_GROUP):
        qa_ref[h, :, 0:HEAD_DIM] = q_ref[h]
        qa_ref[h, :, HEAD_DIM:2 * HEAD_DIM] = jnp.where(lane < SEL_ROWS, bias, qf_ref[0, h:h + 1, :]).astype(BF16)


def _cmp_attn(q_t, kvcmp, slopes, ovt, qfeat, B, tq=512):
    NH, T, dh = q_t.shape
    S = T // B
    nq = S // tq
    nhb = kvcmp.shape[2]
    n_sel = S // SEL_BLOCK
    assert n_sel <= SEL_ROWS
    topk = min(SEL_TOPK, n_sel)
    qmap = lambda bb, g, i: (g, bb * nq + i, 0)
    return pl.pallas_call(
        functools.partial(_cmp_attn_kernel, tq=tq, nhb=nhb, n_sel=n_sel, topk=topk),
        grid=(B, N_KV_GROUPS, nq),
        in_specs=[
            pl.BlockSpec((HEADS_PER_GROUP, tq, dh), qmap),
            pl.BlockSpec((1, 1, nhb, dh), lambda bb, g, i: (g, bb, 0, 0)),
            pl.BlockSpec((1, 1, nhb, dh), lambda bb, g, i: (N_KV_GROUPS + g, bb, 0, 0)),
            pl.BlockSpec((1, 8, LANES), lambda bb, g, i: (g, 0, 0)),
            pl.BlockSpec((SEL_ROWS, nhb), lambda bb, g, i: (0, 0)),
            pl.BlockSpec((1, 8, LANES), lambda bb, g, i: (g, 0, 0)),
        ],
        out_specs=[
            pl.BlockSpec((HEADS_PER_GROUP, tq, dh), qmap),
            pl.BlockSpec((HEADS_PER_GROUP, tq, 2 * dh), qmap),
        ],
        out_shape=[
            jax.ShapeDtypeStruct((NH, T, dh), BF16),
            jax.ShapeDtypeStruct((NH, T, 2 * dh), BF16),
        ],
        compiler_params=_cparams(("parallel", "parallel", "parallel")),
        name="cmp_attn",
    )(q_t, kvcmp, kvcmp, slopes, ovt, qfeat)


def _sel_attn_kernel(q_ref, k_ref, v_ref, o_ref, m_scr, acc_scr, s_buf, *, tq, tk):
    i = pl.program_id(2)
    q0 = i * tq
    nkt = k_ref.shape[1]
    m_scr[...] = jnp.full_like(m_scr, MASK_VALUE)
    acc_scr[...] = jnp.zeros_like(acc_scr)
    qpos = q0 + lax.broadcasted_iota(I32, (tq, 1), 0)

    def scores_to(t, slot):
        ka_t = k_ref[0, jnp.minimum(t, nkt - 1)]
        causal = (t * tk + lax.broadcasted_iota(I32, (1, tk), 1)) <= qpos
        for h in range(HEADS_PER_GROUP):
            s_buf[slot, h] = jnp.where(causal, _dot(q_ref[h], ka_t), MASK_VALUE)

    def absorb_from(t, slot):
        k0 = pl.multiple_of(jnp.minimum(t, nkt - 1) * tk, tk)
        va = v_ref[0, pl.ds(k0, tk), :]
        half = tq // 2
        for h in range(HEADS_PER_GROUP):
            m_prev = m_scr[h]
            m_new = jnp.maximum(m_prev, jnp.max(s_buf[slot, h], axis=-1, keepdims=True))
            alpha = jnp.exp(m_prev - m_new)
            p = jnp.concatenate([jnp.exp(s_buf[slot, h, 0:half, :] - m_new[0:half]).astype(BF16),
                                 jnp.exp(s_buf[slot, h, half:tq, :] - m_new[half:tq]).astype(BF16)], axis=0)
            acc_scr[h] = alpha * acc_scr[h] + _dot(p, va)
            m_scr[h] = m_new

    n_tiles = q0 // tk + 1

    def tile_pair(pr, carry):
        t = 2 * pr
        scores_to(t + 1, 1)
        absorb_from(t, 0)

        @pl.when(t + 1 < n_tiles)
        def _():
            scores_to(t + 2, 0)
            absorb_from(t + 1, 1)

        return carry

    scores_to(0, 0)
    lax.fori_loop(0, (n_tiles + 1) // 2, tile_pair, 0)
    for h in range(HEADS_PER_GROUP):
        o_ref[h] = (acc_scr[h, :, 0:HEAD_DIM] / acc_scr[h, :, HEAD_DIM:HEAD_DIM + 1]).astype(o_ref.dtype)


def _sel_attn(q_aug, k_aug_t, v_aug, B, tq=256, tk=SEL_KEY_TILE):
    NH, T, dq = q_aug.shape
    dh = dq // 2
    S = T // B
    assert tk % tq == 0 and S % tk == 0
    nq = S // tq
    nkt = S // tk
    qmap = lambda bb, g, i: (g, bb * nq + i, 0)
    return pl.pallas_call(
        functools.partial(_sel_attn_kernel, tq=tq, tk=tk),
        grid=(B, N_KV_GROUPS, nq),
        in_specs=[
            pl.BlockSpec((HEADS_PER_GROUP, tq, dq), qmap),
            pl.BlockSpec((1, nkt, dq, tk), lambda bb, g, i: (g, bb, 0, 0)),
            pl.BlockSpec((1, S, dq), lambda bb, g, i: (g, bb, 0)),
        ],
        out_specs=pl.BlockSpec((HEADS_PER_GROUP, tq, dh), qmap),
        out_shape=jax.ShapeDtypeStruct((NH, T, dh), BF16),
        scratch_shapes=[
            pltpu.VMEM((HEADS_PER_GROUP, tq, 1), F32),
            pltpu.VMEM((HEADS_PER_GROUP, tq, dq), F32),
            pltpu.VMEM((2, HEADS_PER_GROUP, tq, tk), F32),
        ],
        compiler_params=_cparams(("parallel", "parallel", "parallel")),
        name="nsa_sel_attn",
    )(q_aug, k_aug_t, v_aug)


def _win_attn_kernel(q_ref, k_ref, v_ref, o_ref, *, tq):
    i = pl.program_id(2)
    q0 = i * tq
    back = WINDOW // tq
    qpos = q0 + lax.broadcasted_iota(I32, (tq, 1), 0)
    tiles = []
    for d in range(back, -1, -1):
        kt = i - d
        ktc = jnp.maximum(kt, 0)
        kpos = kt * tq + lax.broadcasted_iota(I32, (1, tq), 1)
        if d == back:
            valid = ((qpos - kpos) < WINDOW) & (kt >= 0)
        elif d == 0:
            valid = kpos <= qpos
        else:
            valid = kt >= 0
        tiles.append((k_ref[0, ktc], v_ref[0, pl.ds(pl.multiple_of(ktc * tq, tq), tq), :], valid))
    for h in range(HEADS_PER_GROUP):
        q = q_ref[h]
        s_all = [jnp.where(valid, _dot(q, k_t), MASK_VALUE) for k_t, _, valid in tiles]
        m = functools.reduce(jnp.maximum, [jnp.max(s, axis=-1, keepdims=True) for s in s_all])
        acc = functools.reduce(lambda a, b: a + b,
                               [_dot(jnp.exp(s - m).astype(BF16), va) for s, (_, va, _) in zip(s_all, tiles)])
        o_ref[h] = (acc[:, 0:HEAD_DIM] / acc[:, HEAD_DIM:HEAD_DIM + 1]).astype(o_ref.dtype)


def _win_attn(q_aug, k_aug_t, v_aug, B, tq=256):
    NH, T, dq = q_aug.shape
    dh = dq // 2
    S = T // B
    assert WINDOW % tq == 0 and S % tq == 0
    nq = S // tq
    qmap = lambda bb, g, i: (g, bb * nq + i, 0)
    return pl.pallas_call(
        functools.partial(_win_attn_kernel, tq=tq),
        grid=(B, N_KV_GROUPS, nq),
        in_specs=[
            pl.BlockSpec((HEADS_PER_GROUP, tq, dq), qmap),
            pl.BlockSpec((1, nq, dq, tq), lambda bb, g, i: (g, bb, 0, 0)),
            pl.BlockSpec((1, S, dq), lambda bb, g, i: (g, bb, 0)),
        ],
        out_specs=pl.BlockSpec((HEADS_PER_GROUP, tq, dh), qmap),
        out_shape=jax.ShapeDtypeStruct((NH, T, dh), BF16),
        compiler_params=_cparams(("parallel", "parallel", "parallel")),
        name="nsa_win_attn",
    )(q_aug, k_aug_t, v_aug)


def _attn_out_kernel(oc_ref, os_ref, ow_ref, gt_ref, ex_ref, x_ref, w_ref, g_ref, b_ref, of_ref, op_ref, lhs):
    gt = gt_ref[...]
    gt_hi = gt.astype(BF16)
    gt_lo = (gt - gt_hi.astype(F32)).astype(BF16)
    gt2 = jnp.concatenate([gt_hi, gt_lo], axis=-1)
    ge = [_dot(gt2, ex_ref[br]) for br in range(N_BRANCHES)]
    for hd in range(N_HEADS):
        cols = slice(hd * HEAD_DIM, (hd + 1) * HEAD_DIM)
        o = (ge[0][:, cols] * oc_ref[hd].astype(F32) + ge[1][:, cols] * os_ref[hd].astype(F32)
             + ge[2][:, cols] * ow_ref[hd].astype(F32))
        lhs[:, cols] = o.astype(BF16)
    mix = _dot(lhs[...], w_ref[...])
    h = _layer_norm(DEEPNORM_ALPHA * x_ref[...] + mix, g_ref[...], b_ref[...])
    of_ref[...] = h
    op_ref[...] = _pack_halves(h)


def _attn_out_res_ln(o_cmp, o_sel, o_win, gates, x, w, g, b, tm=256):
    T, D = x.shape
    NH = o_cmp.shape[0]
    hmap = lambda i: (0, i, 0)
    row = lambda i: (i, 0)
    fix = lambda i: (0, 0)
    expand = np.zeros((N_BRANCHES, 2 * LANES, NH * HEAD_DIM), np.float32)
    for hd in range(NH):
        for br in range(N_BRANCHES):
            for piece in range(2):
                expand[br, piece * LANES + N_BRANCHES * hd + br, hd * HEAD_DIM:(hd + 1) * HEAD_DIM] = 1.0
    expand = jnp.asarray(expand, dtype=BF16)
    return pl.pallas_call(
        _attn_out_kernel,
        grid=(T // tm,),
        in_specs=[
            pl.BlockSpec((NH, tm, HEAD_DIM), hmap), pl.BlockSpec((NH, tm, HEAD_DIM), hmap),
            pl.BlockSpec((NH, tm, HEAD_DIM), hmap),
            pl.BlockSpec((tm, LANES), row),
            pl.BlockSpec((N_BRANCHES, 2 * LANES, NH * HEAD_DIM), lambda i: (0, 0, 0)),
            pl.BlockSpec((tm, D), row),
            pl.BlockSpec((D, D), fix),
            pl.BlockSpec((1, D), fix), pl.BlockSpec((1, D), fix),
        ],
        out_specs=[pl.BlockSpec((tm, D), row), pl.BlockSpec((tm, D // 2), row)],
        out_shape=[jax.ShapeDtypeStruct((T, D), F32), jax.ShapeDtypeStruct((T, D // 2), U32)],
        scratch_shapes=[pltpu.VMEM((tm, D), BF16)],
        compiler_params=_cparams(("parallel",), VMEM_LIMIT),
        name="attn_out_res_ln",
    )(o_cmp, o_sel, o_win, gates, expand, x, w, g, b)


def _bf16_pieces(x, n=3):
    out, rest = [], np.asarray(x, np.float32)
    for _ in range(n):
        piece = rest.astype(BF16).astype(np.float32)
        out.append(piece)
        rest = (rest - piece).astype(np.float32)
    return out


def _nsa_tables(S):
    nhb = S // CMP_STRIDE
    n_sel = S // SEL_BLOCK
    heads = np.arange(1, N_HEADS + 1, dtype=np.float32)
    slope = (2.0 ** (-8.0 * heads / N_HEADS)).astype(np.float32)
    slopes = np.zeros((N_KV_GROUPS, 8, LANES), np.float32)
    slopes[:, :HEADS_PER_GROUP, :] = slope.reshape(N_KV_GROUPS, HEADS_PER_GROUP, 1)
    c0 = np.arange(nhb)[None, :] * CMP_STRIDE
    j0 = np.arange(SEL_ROWS)[:, None] * SEL_BLOCK
    ovt = np.maximum(np.minimum(c0 + CMP_BLOCK, j0 + SEL_BLOCK) - np.maximum(c0, j0), 0).astype(np.float32) / CMP_BLOCK
    ovt[:, nhb - 1] = 0.0
    ovt[n_sel:, :] = 0.0
    qfeat = np.zeros((N_KV_GROUPS, 8, LANES), np.float32)
    pieces = _bf16_pieces(slope)
    for t, piece in enumerate(pieces):
        qfeat[:, :HEADS_PER_GROUP, SEL_ROWS + t] = piece.reshape(N_KV_GROUPS, HEADS_PER_GROUP)
        qfeat[:, :HEADS_PER_GROUP, SEL_ROWS + 3 + t] = piece.reshape(N_KV_GROUPS, HEADS_PER_GROUP)
    kpos = np.arange(S)
    kx = np.zeros((S, LANES), np.float32)
    kx[kpos, kpos // SEL_BLOCK] = 1.0
    kx[:, SEL_ROWS:SEL_ROWS + 3] = (SEL_BLOCK * (kpos // SEL_BLOCK))[:, None]
    kx[:, SEL_ROWS + 3:SEL_ROWS + 6] = (kpos % SEL_BLOCK)[:, None]
    return jnp.asarray(slopes), jnp.asarray(ovt, dtype=BF16), jnp.asarray(qfeat), jnp.asarray(kx, dtype=BF16)


def _nsa_layer_res_ln(hf, hb, B, kv_w, cmp_pos, cmp_w1, cmp_w2, w_qg, w_o, g, b):
    T, D = hf.shape
    S = T // B
    hd = N_HEADS * HEAD_DIM
    gsz = N_KV_GROUPS * HEAD_DIM
    slopes, ovt, qfeat, kx = _nsa_tables(S)

    kvw = kv_w.astype(BF16)
    kvc = _proj_heads(hb, kvw[:, :2 * gsz], F32)
    kv_t = _proj_heads(hb, kvw[:, 2 * gsz:], BF16)
    wq = w_qg[:, :hd].astype(BF16)
    wgt = jnp.zeros((D, LANES), F32).at[:, :N_BRANCHES * N_HEADS].set(w_qg[:, hd:]).astype(BF16)
    q_t = _proj_heads(hb, wq, BF16, scale=HEAD_DIM ** -0.5)
    gates = _gates(hb, wgt)

    pos = cmp_pos.reshape(2, 2, CMP_STRIDE * HEAD_DIM)
    kvcmp = _cmp_mlp(kvc, pos, cmp_w1.astype(BF16), cmp_w2.astype(BF16), B)
    o_cmp, q_aug = _cmp_attn(q_t, kvcmp, slopes, ovt, qfeat, B)
    kx_all = jnp.broadcast_to(jnp.tile(kx, (B, 1))[None], (N_KV_GROUPS, T, LANES))
    k_aug = jnp.concatenate([kv_t[:N_KV_GROUPS], kx_all], axis=-1)
    k_aug_t = k_aug.reshape(N_KV_GROUPS, T // SEL_KEY_TILE, SEL_KEY_TILE, 2 * HEAD_DIM).transpose(0, 1, 3, 2)
    ones_col = jnp.zeros((1, 1, LANES), BF16).at[0, 0, 0].set(1.0)
    v_aug = jnp.concatenate([kv_t[N_KV_GROUPS:2 * N_KV_GROUPS],
                             jnp.broadcast_to(ones_col, (N_KV_GROUPS, T, LANES))], axis=-1)
    o_sel = _sel_attn(q_aug, k_aug_t, v_aug, B)
    win_tq = 256
    lane_id = jnp.arange(LANES)[None, None, :]
    kw_aug = jnp.concatenate([kv_t[2 * N_KV_GROUPS:3 * N_KV_GROUPS],
                              jnp.where(lane_id < SEL_ROWS, jnp.zeros((), BF16), kx_all)], axis=-1)
    kw_aug_t = kw_aug.reshape(N_KV_GROUPS, T // win_tq, win_tq, 2 * HEAD_DIM).transpose(0, 1, 3, 2)
    vw_aug = jnp.concatenate([kv_t[3 * N_KV_GROUPS:],
                              jnp.broadcast_to(ones_col, (N_KV_GROUPS, T, LANES))], axis=-1)
    o_win = _win_attn(q_aug, kw_aug_t, vw_aug, B, tq=win_tq)
    return _attn_out_res_ln(o_cmp, o_sel, o_win, gates, hf, w_o.astype(BF16), g, b)


def kernel(x, conv_w_pw1, conv_b_pw1, conv_w_dw, conv_b_dw, conv_ln_g, conv_ln_b, conv_w_pw2, kv_w, cmp_pos, cmp_w1, cmp_w2, nsa_w_qg, nsa_w_o, moe_wg, moe_bg, moe_we, moe_be, moe_w_gu, moe_w_down, ln_g, ln_b):
    B, S, D = x.shape
    T = B * S
    hf = x.reshape(T, D)
    hb = hf.astype(BF16)
    r = lambda v: v.reshape(1, -1)
    for l in range(DEPTH):
        if l < N_A_LAYERS:
            glu = _pw1_glu(hb, conv_w_pw1[l].astype(BF16), r(conv_b_pw1[l]))
            conv = _dwconv(glu.reshape(B, S, D), conv_w_dw[l], r(conv_b_dw[l])).reshape(T, D)
            hf, hp = _pw2_res_ln(conv, hf, conv_w_pw2[l].astype(BF16), r(conv_ln_g[l]), r(conv_ln_b[l]),
                                 r(ln_g[l, 0]), r(ln_b[l, 0]))
        else:
            j = l - N_A_LAYERS
            hf, hp = _nsa_layer_res_ln(hf, hb, B, kv_w, cmp_pos, cmp_w1, cmp_w2, nsa_w_qg[j], nsa_w_o[j],
                                       r(ln_g[l, 0]), r(ln_b[l, 0]))
        hf, hb = _hier_moe_res_ln(hf, hp, moe_wg[l], moe_bg[l], moe_we[l], moe_be[l], moe_w_gu, moe_w_down, l,
                                  r(ln_g[l, 1]), r(ln_b[l, 1]))
    return hf.reshape(B, S, D)
```

```python
import functools

import numpy as np
import jax
import jax.numpy as jnp
from jax import lax
from jax.experimental import pallas as pl
from jax.experimental.pallas import tpu as pltpu

F32 = jnp.float32
BF16 = jnp.bfloat16
I32 = jnp.int32
U32 = jnp.uint32

DEPTH = 2
N_A_LAYERS = DEPTH // 2
DEEPNORM_ALPHA = (2.0 * DEPTH) ** 0.25
LN_EPS = 1e-5
CONV_WIDTH = 31
N_HEADS = 16
HEAD_DIM = 128
N_KV_GROUPS = 4
HEADS_PER_GROUP = N_HEADS // N_KV_GROUPS
CMP_BLOCK = 32
CMP_STRIDE = 16
SEL_BLOCK = 64
SEL_TOPK = 16
WINDOW = 512
N_BRANCHES = 3
MASK_VALUE = -1e30
N_GROUPS = 4
EXPERTS_PER_GROUP = 8
N_EXPERTS = N_GROUPS * EXPERTS_PER_GROUP
MOE_CHUNK = 128

LANES = 128
SUBLANES = 8
VMEM_LIMIT = 56 * 1024 * 1024

CONV_HALO = 32
SEL_ROWS = 64
SEL_KEY_TILE = 512
W_DMA_SPLIT = 4


def _cparams(sem, vmem=None):
    return pltpu.CompilerParams(dimension_semantics=sem, vmem_limit_bytes=vmem)


def _sigmoid(x):
    return 1.0 / (1.0 + jnp.exp(-x))


def _layer_norm(x, g, b):
    mu = jnp.mean(x, axis=-1, keepdims=True)
    xc = x - mu
    var = jnp.mean(xc * xc, axis=-1, keepdims=True)
    return xc * lax.rsqrt(var + LN_EPS) * g + b


def _pack_halves(x):
    n = x.shape[-1] // 2
    lo = lax.bitcast_convert_type(x[:, :n].astype(BF16).astype(F32), U32)
    hi = lax.bitcast_convert_type(x[:, n:].astype(BF16).astype(F32), U32)
    return lax.shift_right_logical(lo, jnp.uint32(16)) | hi


def _unpack_halves(p):
    lo = lax.bitcast_convert_type(lax.shift_left(p, jnp.uint32(16)), F32)
    hi = lax.bitcast_convert_type(p & jnp.uint32(0xFFFF0000), F32)
    return lo, hi


def _dot(a, b):
    return jnp.dot(a, b, preferred_element_type=F32)


def _dot_nt(a, b):
    return lax.dot_general(a, b, (((1,), (1,)), ((), ())), preferred_element_type=F32)


def _pw1_glu_kernel(x_ref, wa_ref, wg_ref, ba_ref, bg_ref, o_ref):
    x = x_ref[...]
    a = _dot(x, wa_ref[...]) + ba_ref[...]
    g = _dot(x, wg_ref[...]) + bg_ref[...]
    o_ref[...] = a * _sigmoid(g)


def _pw1_glu(xb, w, b, tm=1024, tn=512):
    T, D = xb.shape
    nj = D // tn
    return pl.pallas_call(
        _pw1_glu_kernel,
        grid=(nj, T // tm),
        in_specs=[
            pl.BlockSpec((tm, D), lambda j, i: (i, 0)),
            pl.BlockSpec((D, tn), lambda j, i: (0, j)),
            pl.BlockSpec((D, tn), lambda j, i: (0, j + nj)),
            pl.BlockSpec((1, tn), lambda j, i: (0, j)),
            pl.BlockSpec((1, tn), lambda j, i: (0, j + nj)),
        ],
        out_specs=pl.BlockSpec((tm, tn), lambda j, i: (i, j)),
        out_shape=jax.ShapeDtypeStruct((T, D), F32),
        compiler_params=_cparams(("parallel", "parallel"), VMEM_LIMIT),
        name="pw1_glu",
    )(xb, w, w, b, b)


def _dwconv_kernel(x_ref, halo_ref, w_ref, b_ref, o_ref, xsh, *, ts, tc, rc):
    i = pl.program_id(1)
    xsh[0, 0:CONV_HALO, :] = jnp.where(i > 0, halo_ref[0], 0.0)
    xsh[0, CONV_HALO:CONV_HALO + ts, :] = x_ref[0]
    span = ts + CONV_HALO - SUBLANES
    for p in range(1, SUBLANES):
        xsh[p, 0:span, :] = xsh[0, p:p + span, :]
    off = CONV_HALO - (CONV_WIDTH - 1)
    for c0 in range(0, tc, LANES):
        for r0 in range(0, ts, rc):
            acc = jnp.broadcast_to(b_ref[:, c0:c0 + LANES], (rc, LANES))
            for k in range(CONV_WIDTH):
                p = (k + off) % SUBLANES
                base = r0 + k + off - p
                acc = acc + w_ref[k:k + 1, c0:c0 + LANES] * xsh[p, base:base + rc, c0:c0 + LANES]
            o_ref[0, r0:r0 + rc, c0:c0 + LANES] = acc


def _dwconv(x, w, b, ts=512, tc=512, rc=128):
    B, S, D = x.shape
    hb = ts // CONV_HALO
    return pl.pallas_call(
        functools.partial(_dwconv_kernel, ts=ts, tc=tc, rc=rc),
        grid=(B, S // ts, D // tc),
        in_specs=[
            pl.BlockSpec((1, ts, tc), lambda bb, i, c: (bb, i, c)),
            pl.BlockSpec((1, CONV_HALO, tc), lambda bb, i, c: (bb, jnp.maximum(i * hb - 1, 0), c)),
            pl.BlockSpec((CONV_WIDTH, tc), lambda bb, i, c: (0, c)),
            pl.BlockSpec((1, tc), lambda bb, i, c: (0, c)),
        ],
        out_specs=pl.BlockSpec((1, ts, tc), lambda bb, i, c: (bb, i, c)),
        out_shape=jax.ShapeDtypeStruct((B, S, D), F32),
        scratch_shapes=[pltpu.VMEM((SUBLANES, CONV_HALO + ts, tc), F32)],
        compiler_params=_cparams(("parallel", "parallel", "parallel")),
        name="dwconv",
    )(x, x, w, b)


def _pw2_kernel(c_ref, x_ref, w_ref, cg_ref, cb_ref, g_ref, b_ref, of_ref, op_ref):
    y = _layer_norm(c_ref[...], cg_ref[...], cb_ref[...])
    y = y * _sigmoid(y)
    mix = _dot(y.astype(BF16), w_ref[...])
    h = _layer_norm(DEEPNORM_ALPHA * x_ref[...] + mix, g_ref[...], b_ref[...])
    of_ref[...] = h
    op_ref[...] = _pack_halves(h)


def _pw2_res_ln(conv, x, w, cg, cb, g, b, tm=256):
    T, D = x.shape
    row = lambda i: (i, 0)
    fix = lambda i: (0, 0)
    return pl.pallas_call(
        _pw2_kernel,
        grid=(T // tm,),
        in_specs=[
            pl.BlockSpec((tm, D), row), pl.BlockSpec((tm, D), row),
            pl.BlockSpec((D, D), fix),
            pl.BlockSpec((1, D), fix), pl.BlockSpec((1, D), fix),
            pl.BlockSpec((1, D), fix), pl.BlockSpec((1, D), fix),
        ],
        out_specs=[pl.BlockSpec((tm, D), row), pl.BlockSpec((tm, D // 2), row)],
        out_shape=[jax.ShapeDtypeStruct((T, D), F32), jax.ShapeDtypeStruct((T, D // 2), U32)],
        compiler_params=_cparams(("parallel",), VMEM_LIMIT),
        name="pw2_res_ln",
    )(conv, x, w, cg, cb, g, b)


def _router_kernel(h_ref, w_ref, b_ref, ri_ref, rw_ref, cnt_ref, *, tm):
    s = pl.program_id(0)

    @pl.when(s == 0)
    def _():
        cnt_ref[...] = jnp.zeros_like(cnt_ref)

    h = h_ref[...]
    w = w_ref[...]
    h_hi = h.astype(BF16)
    h_lo = (h - h_hi.astype(F32)).astype(BF16)
    w_hi = w.astype(BF16)
    w_lo = (w - w_hi.astype(F32)).astype(BF16)
    lg = _dot(h_hi, w_hi) + _dot(h_lo, w_hi) + _dot(h_hi, w_lo) + b_ref[...]

    lane = lax.broadcasted_iota(I32, (tm, LANES), 1)
    lanef = lane.astype(F32)
    neg = jnp.float32(-jnp.inf)
    big = jnp.float32(1e9)

    is_g = lane < N_GROUPS
    lgm = jnp.where(is_g, lg, neg)
    mg = jnp.max(lgm, axis=-1, keepdims=True)
    grp = jnp.min(jnp.where(lgm == mg, lanef, big), axis=-1, keepdims=True)
    sg = jnp.sum(jnp.where(is_g, jnp.exp(lg - mg), 0.0), axis=-1, keepdims=True)
    gw = 1.0 / sg

    lo = N_GROUPS + grp * EXPERTS_PER_GROUP
    in_g = (lanef >= lo) & (lanef < lo + EXPERTS_PER_GROUP)
    le = jnp.where(in_g, lg, neg)
    m1 = jnp.max(le, axis=-1, keepdims=True)
    i1 = jnp.min(jnp.where(le == m1, lanef, big), axis=-1, keepdims=True)
    le2 = jnp.where(lanef == i1, neg, le)
    m2 = jnp.max(le2, axis=-1, keepdims=True)
    i2 = jnp.min(jnp.where(le2 == m2, lanef, big), axis=-1, keepdims=True)
    t = jnp.exp(m2 - m1)
    p1 = 1.0 / (1.0 + t)
    w1 = p1 * gw
    w2 = (t * p1) * gw
    e1 = i1 - N_GROUPS
    e2 = i2 - N_GROUPS

    oh1 = (lanef == e1)
    oh2 = (lanef == e2)
    oh = (oh1 | oh2).astype(F32)
    r_i = lax.broadcasted_iota(I32, (tm, tm), 0)
    c_i = lax.broadcasted_iota(I32, (tm, tm), 1)
    tri = (c_i < r_i).astype(BF16)
    cs = _dot(tri, oh.astype(BF16)) + cnt_ref[...].astype(F32)
    rank1 = jnp.sum(jnp.where(oh1, cs, 0.0), axis=-1, keepdims=True)
    rank2 = jnp.sum(jnp.where(oh2, cs, 0.0), axis=-1, keepdims=True)
    cnt_ref[...] = cnt_ref[...] + jnp.sum(oh, axis=0, keepdims=True).astype(I32)

    ri = jnp.where(lane == 0, e1, jnp.where(lane == 1, e2, jnp.where(lane == 2, rank1, jnp.where(lane == 3, rank2, 0.0))))
    ri_ref[...] = ri.T[0:SUBLANES, :].astype(I32)
    rw_ref[...] = jnp.where(lane == 0, w1, jnp.where(lane == 1, w2, 0.0))


def _router(h, wcat, bcat, tm=256):
    T, D = h.shape
    return pl.pallas_call(
        functools.partial(_router_kernel, tm=tm),
        grid=(T // tm,),
        in_specs=[
            pl.BlockSpec((tm, D), lambda i: (i, 0)),
            pl.BlockSpec((D, LANES), lambda i: (0, 0)),
            pl.BlockSpec((1, LANES), lambda i: (0, 0)),
        ],
        out_specs=[
            pl.BlockSpec((SUBLANES, tm), lambda i: (0, i)),
            pl.BlockSpec((tm, LANES), lambda i: (i, 0)),
            pl.BlockSpec((1, LANES), lambda i: (0, 0)),
        ],
        out_shape=[
            jax.ShapeDtypeStruct((SUBLANES, T), I32),
            jax.ShapeDtypeStruct((T, LANES), F32),
            jax.ShapeDtypeStruct((1, LANES), I32),
        ],
        compiler_params=_cparams(("arbitrary",)),
        name="moe_router",
    )(h, wcat, bcat)


def _dispatch_kernel(dest_ref, zs_ref, zc_ref, nu_ref, x_ref, xs_hbm, zblk, sem, *, tm, n_tok, n_chunks):
    s = pl.program_id(0)

    def row_copy(r, d):
        return pltpu.make_async_copy(x_ref.at[pl.ds(r, 1)], xs_hbm.at[pl.ds(d, 1)], sem)

    def zero_row_copy(d):
        return pltpu.make_async_copy(zblk.at[pl.ds(0, 1)], xs_hbm.at[pl.ds(d, 1)], sem)

    def zero_chunk_copy(c):
        return pltpu.make_async_copy(zblk, xs_hbm.at[pl.ds(pl.multiple_of(c * MOE_CHUNK, MOE_CHUNK), MOE_CHUNK)], sem)

    @pl.when(s == 0)
    def _():
        zblk[...] = jnp.zeros_like(zblk)

        def fill(start):
            def per_expert(e, c):
                def per_row(r, c2):
                    cp = zero_row_copy(zs_ref[e] + r)
                    if start:
                        cp.start()
                    else:
                        cp.wait()
                    return c2
                lax.fori_loop(0, zc_ref[e], per_row, 0)
                return c
            lax.fori_loop(0, N_EXPERTS, per_expert, 0)

            def per_chunk(c, c2):
                cp = zero_chunk_copy(c)
                if start:
                    cp.start()
                else:
                    cp.wait()
                return c2
            lax.fori_loop(nu_ref[0], n_chunks, per_chunk, 0)

        fill(True)
        fill(False)

    base = s * tm

    def issue(r, c):
        t = base + r
        row_copy(r, dest_ref[t]).start(priority=0)
        row_copy(r, dest_ref[n_tok + t]).start(priority=1)
        return c

    lax.fori_loop(0, tm, issue, 0, unroll=8)

    def drain(r, c):
        row_copy(0, 0).wait()
        row_copy(0, 0).wait()
        return c

    lax.fori_loop(0, tm, drain, 0, unroll=8)


def _dispatch(x, dest_flat, zstart, zcount, n_used, n_rows, tm=256):
    T, W = x.shape
    return pl.pallas_call(
        functools.partial(_dispatch_kernel, tm=tm, n_tok=T, n_chunks=n_rows // MOE_CHUNK),
        grid_spec=pltpu.PrefetchScalarGridSpec(
            num_scalar_prefetch=4,
            grid=(T // tm,),
            in_specs=[pl.BlockSpec((tm, W), lambda i, *_: (i, 0))],
            out_specs=pl.BlockSpec(memory_space=pl.ANY),
            scratch_shapes=[pltpu.VMEM((MOE_CHUNK, W), x.dtype), pltpu.SemaphoreType.DMA(())],
        ),
        out_shape=jax.ShapeDtypeStruct((n_rows, W), x.dtype),
        compiler_params=_cparams(("arbitrary",)),
        name="moe_dispatch",
    )(dest_flat, zstart, zcount, n_used, x)


def _moe_mlp_kernel(ce_ref, nu_ref, first_ref, nxt_ref, x_ref, wgu_hbm, wd_hbm, o_ref,
                    wgu_stage, wd_stage, wgu_bf, wd_bf, sem, *, F, layer):
    c = pl.program_id(0)

    def w_copies(e):
        copies = []
        for ref, stage, s_idx in ((wgu_hbm, wgu_stage, 0), (wd_hbm, wd_stage, 1)):
            rows = stage.shape[0] // W_DMA_SPLIT
            for part in range(W_DMA_SPLIT):
                rs = pl.ds(part * rows, rows)
                copies.append(pltpu.make_async_copy(ref.at[layer, e, rs], stage.at[rs], sem.at[s_idx]))
        return copies

    def start_all(e):
        for cp in w_copies(e):
            cp.start(priority=1)

    @pl.when(c < nu_ref[0])
    def _():
        @pl.when(first_ref[c] == 1)
        def _():
            @pl.when(c == 0)
            def _():
                start_all(ce_ref[0])

            for cp in w_copies(ce_ref[c]):
                cp.wait()
            wgu_bf[...] = wgu_stage[...].astype(BF16)
            wd_bf[...] = wd_stage[...].astype(BF16)

            @pl.when(nxt_ref[c] >= 0)
            def _():
                start_all(nxt_ref[c])

        x_lo, x_hi = _unpack_halves(x_ref[...])
        half = x_lo.shape[-1]
        gu = _dot(x_lo.astype(BF16), wgu_bf[0:half, :]) + _dot(x_hi.astype(BF16), wgu_bf[half:2 * half, :])
        g = gu[:, :F]
        u = gu[:, F:]
        hmid = (g * _sigmoid(g)) * u
        o_ref[...] = _pack_halves(_dot(hmid.astype(BF16), wd_bf[...]))

    @pl.when(c >= nu_ref[0])
    def _():
        o_ref[...] = jnp.zeros_like(o_ref)


def _moe_mlp(xs, chunk_e, n_used, first, nxt, w_gu, w_down, layer):
    n_rows, W = xs.shape
    D = 2 * W
    F2 = w_gu.shape[-1]
    F = F2 // 2
    n_chunks = n_rows // MOE_CHUNK

    def x_map(c, ce, nu, fi, nx):
        return (jnp.minimum(c, nu[0] - 1), 0)

    return pl.pallas_call(
        functools.partial(_moe_mlp_kernel, F=F, layer=layer),
        grid_spec=pltpu.PrefetchScalarGridSpec(
            num_scalar_prefetch=4,
            grid=(n_chunks,),
            in_specs=[
                pl.BlockSpec((MOE_CHUNK, W), x_map),
                pl.BlockSpec(memory_space=pl.ANY),
                pl.BlockSpec(memory_space=pl.ANY),
            ],
            out_specs=pl.BlockSpec((MOE_CHUNK, W), lambda c, ce, nu, fi, nx: (c, 0)),
            scratch_shapes=[
                pltpu.VMEM((D, F2), F32), pltpu.VMEM((F, D), F32),
                pltpu.VMEM((D, F2), BF16), pltpu.VMEM((F, D), BF16),
                pltpu.SemaphoreType.DMA((2,)),
            ],
        ),
        out_shape=jax.ShapeDtypeStruct((n_rows, W), U32),
        compiler_params=_cparams(("arbitrary",), VMEM_LIMIT),
        name="moe_mlp",
    )(chunk_e, n_used, first, nxt, xs, w_gu, w_down)


def _combine_kernel(dest_ref, y_hbm, rw_ref, h_ref, g_ref, b_ref, of_ref, ob_ref, gbuf, sem, *, tm, nsteps):
    s = pl.program_id(0)

    def row_copy(src_row, slot, k, r):
        return pltpu.make_async_copy(y_hbm.at[pl.ds(src_row, 1)], gbuf.at[slot, k, pl.ds(r, 1)], sem.at[slot])

    def issue(step, slot):
        base = step * tm

        def body(r, c):
            t = base + r
            row_copy(dest_ref[t], slot, 0, r).start(priority=0)
            row_copy(dest_ref[tm * nsteps + t], slot, 1, r).start(priority=1)
            return c

        lax.fori_loop(0, tm, body, 0, unroll=8)

    @pl.when(s == 0)
    def _():
        issue(0, 0)

    @pl.when(s + 1 < nsteps)
    def _():
        issue(s + 1, (s + 1) % 2)

    slot = s % 2

    def drain(r, c):
        row_copy(0, slot, 0, 0).wait()
        row_copy(0, slot, 1, 0).wait()
        return c

    lax.fori_loop(0, tm, drain, 0, unroll=8)

    w0 = rw_ref[:, 0:1]
    w1 = rw_ref[:, 1:2]
    lo0, hi0 = _unpack_halves(gbuf[slot, 0])
    lo1, hi1 = _unpack_halves(gbuf[slot, 1])
    y = jnp.concatenate([w0 * lo0 + w1 * lo1, w0 * hi0 + w1 * hi1], axis=-1)
    h = _layer_norm(DEEPNORM_ALPHA * h_ref[...] + y, g_ref[...], b_ref[...])
    of_ref[...] = h
    ob_ref[...] = h.astype(BF16)


def _combine_res_ln(y_rows, dest_flat, rw, h, g, b, tm=256):
    T, D = h.shape
    nsteps = T // tm
    row = lambda i, d: (i, 0)
    fix = lambda i, d: (0, 0)
    return pl.pallas_call(
        functools.partial(_combine_kernel, tm=tm, nsteps=nsteps),
        grid_spec=pltpu.PrefetchScalarGridSpec(
            num_scalar_prefetch=1,
            grid=(nsteps,),
            in_specs=[
                pl.BlockSpec(memory_space=pl.ANY),
                pl.BlockSpec((tm, LANES), row),
                pl.BlockSpec((tm, D), row),
                pl.BlockSpec((1, D), fix), pl.BlockSpec((1, D), fix),
            ],
            out_specs=[pl.BlockSpec((tm, D), row), pl.BlockSpec((tm, D), row)],
            scratch_shapes=[pltpu.VMEM((2, 2, tm, D // 2), U32), pltpu.SemaphoreType.DMA((2,))],
        ),
        out_shape=[jax.ShapeDtypeStruct((T, D), F32), jax.ShapeDtypeStruct((T, D), BF16)],
        compiler_params=_cparams(("arbitrary",), VMEM_LIMIT),
        name="moe_combine",
    )(dest_flat, y_rows, rw, h, g, b)


def _hier_moe_res_ln(h, hp, wg, bg, we, be, w_gu, w_down, layer, g, b):
    T, D = h.shape
    wcat = jnp.zeros((D, LANES), F32).at[:, :N_GROUPS].set(wg).at[:, N_GROUPS:N_GROUPS + N_EXPERTS].set(we)
    bcat = jnp.zeros((1, LANES), F32).at[0, :N_GROUPS].set(bg).at[0, N_GROUPS:N_GROUPS + N_EXPERTS].set(be)
    ri, rw, cnt = _router(h, wcat, bcat)

    counts = cnt[0, :N_EXPERTS]
    padded = (counts + MOE_CHUNK - 1) // MOE_CHUNK * MOE_CHUNK
    pad_end = jnp.cumsum(padded)
    pad_start = pad_end - padded
    A = T * 2
    n_chunks = (A + N_EXPERTS * (MOE_CHUNK - 1) + MOE_CHUNK - 1) // MOE_CHUNK
    n_rows = n_chunks * MOE_CHUNK
    seg = jnp.zeros((2, T), I32)
    for e in range(N_EXPERTS):
        seg = jnp.where(ri[0:2] == e, pad_start[e], seg)
    dest_flat = (seg + ri[2:4]).reshape(-1).astype(I32)
    n_used = (pad_end[-1] // MOE_CHUNK).astype(I32)
    cidx = jnp.minimum(jnp.arange(n_chunks, dtype=I32), n_used - 1)
    chunk_e = jnp.sum((pad_end[None, :] <= (cidx * MOE_CHUNK)[:, None]).astype(I32), axis=1)
    chunk_e = jnp.minimum(chunk_e, N_EXPERTS - 1)
    zstart = (pad_start + counts).astype(I32)
    zcount = (padded - counts).astype(I32)
    carange = jnp.arange(n_chunks, dtype=I32)
    prev_e = jnp.concatenate([jnp.full((1,), -1, I32), chunk_e[:-1]])
    first = ((chunk_e != prev_e) & (carange < n_used)).astype(I32)
    run_end = (pad_end[chunk_e] // MOE_CHUNK).astype(I32)
    nxt = jnp.where(run_end < n_used, chunk_e[jnp.minimum(run_end, n_chunks - 1)], -1).astype(I32)

    n_used = n_used.reshape(1)
    xs = _dispatch(hp, dest_flat, zstart, zcount, n_used, n_rows)
    ys = _moe_mlp(xs, chunk_e, n_used, first, nxt, w_gu, w_down, layer)
    return _combine_res_ln(ys, dest_flat, rw, h, g, b)


def _proj_heads_kernel(x_ref, w_ref, o_ref, *, nh, scale):
    acc = _dot(x_ref[...], w_ref[...])
    if scale != 1.0:
        acc = acc * scale
    for c in range(nh):
        o_ref[c] = acc[:, c * HEAD_DIM:(c + 1) * HEAD_DIM].astype(o_ref.dtype)


def _proj_heads(xb, w, out_dtype, scale=1.0, tm=1024, tn=512):
    T, D = xb.shape
    N = w.shape[1]
    nh = tn // HEAD_DIM
    return pl.pallas_call(
        functools.partial(_proj_heads_kernel, nh=nh, scale=scale),
        grid=(N // tn, T // tm),
        in_specs=[
            pl.BlockSpec((tm, D), lambda j, i: (i, 0)),
            pl.BlockSpec((D, tn), lambda j, i: (0, j)),
        ],
        out_specs=pl.BlockSpec((nh, tm, HEAD_DIM), lambda j, i: (j, i, 0)),
        out_shape=jax.ShapeDtypeStruct((N // HEAD_DIM, T, HEAD_DIM), out_dtype),
        compiler_params=_cparams(("parallel", "parallel"), VMEM_LIMIT),
        name="proj_heads",
    )(xb, w)


def _gates_kernel(x_ref, w_ref, o_ref):
    o_ref[...] = _sigmoid(_dot(x_ref[...], w_ref[...]))


def _gates(xb, w, tm=512):
    T, D = xb.shape
    return pl.pallas_call(
        _gates_kernel,
        grid=(T // tm,),
        in_specs=[pl.BlockSpec((tm, D), lambda i: (i, 0)), pl.BlockSpec((D, LANES), lambda i: (0, 0))],
        out_specs=pl.BlockSpec((tm, LANES), lambda i: (i, 0)),
        out_shape=jax.ShapeDtypeStruct((T, LANES), F32),
        compiler_params=_cparams(("parallel",)),
        name="nsa_gates",
    )(xb, w)


def _cmp_mlp_kernel(x_ref, pos_ref, w1_ref, w2_ref, o_ref, *, nhb):
    x = x_ref[0, 0]
    half = x.shape[1]
    lo = (x + pos_ref[0, 0:1, :]).astype(BF16)
    hi = (x + pos_ref[0, 1:2, :]).astype(BF16)
    a_lo = _dot(lo, w1_ref[0, 0:half, :])
    a_hi = _dot(hi, w1_ref[0, half:2 * half, :])
    pre = a_lo + pltpu.roll(a_hi, nhb - 1, 0)
    hid = 0.5 * pre * (1.0 + jnp.tanh(0.7978845608028654 * (pre + 0.044715 * pre * pre * pre)))
    comp = _dot(hid.astype(BF16), w2_ref[0])
    row = lax.broadcasted_iota(I32, comp.shape, 0)
    o_ref[0, 0] = jnp.where(row < nhb - 1, comp, 0.0).astype(o_ref.dtype)


def _cmp_mlp(kvc, pos, w1, w2, B):
    R, T, dh = kvc.shape
    S = T // B
    nhb = S // CMP_STRIDE
    half = CMP_STRIDE * dh
    x4 = kvc.reshape(R, B, nhb, half)
    H = w1.shape[-1]
    return pl.pallas_call(
        functools.partial(_cmp_mlp_kernel, nhb=nhb),
        grid=(R, B),
        in_specs=[
            pl.BlockSpec((1, 1, nhb, half), lambda r, bb: (r, bb, 0, 0)),
            pl.BlockSpec((1, 2, half), lambda r, bb: (r // N_KV_GROUPS, 0, 0)),
            pl.BlockSpec((1, 2 * half, H), lambda r, bb: (r // N_KV_GROUPS, 0, 0)),
            pl.BlockSpec((1, H, dh), lambda r, bb: (r // N_KV_GROUPS, 0, 0)),
        ],
        out_specs=pl.BlockSpec((1, 1, nhb, dh), lambda r, bb: (r, bb, 0, 0)),
        out_shape=jax.ShapeDtypeStruct((R, B, nhb, dh), BF16),
        compiler_params=_cparams(("parallel", "parallel"), VMEM_LIMIT),
        name="cmp_mlp",
    )(x4, pos, w1, w2)


def _cmp_attn_kernel(q_ref, k_ref, v_ref, sl_ref, ovt_ref, qf_ref, o_ref, qa_ref, *, tq, nhb, n_sel, topk):
    i = pl.program_id(2)
    q0 = i * tq
    qpos = q0 + lax.broadcasted_iota(I32, (tq, 1), 0)
    n = lax.broadcasted_iota(I32, (1, nhb), 1)
    dist = qpos - (n * CMP_STRIDE + (CMP_BLOCK - 1))
    valid = (dist >= 0) & (n < nhb - 1)
    distf = dist.astype(F32)
    k = k_ref[0, 0]
    v = v_ref[0, 0]
    psum = jnp.zeros((tq, nhb), F32)
    for h in range(HEADS_PER_GROUP):
        s = _dot_nt(q_ref[h], k) - sl_ref[0, h:h + 1, 0:1] * distf
        s = jnp.where(valid, s, MASK_VALUE)
        m = jnp.max(s, axis=-1, keepdims=True)
        e = jnp.where(valid, jnp.exp(s - m), 0.0)
        l = jnp.sum(e, axis=-1, keepdims=True)
        p = e / jnp.maximum(l, 1e-30)
        o_ref[h] = _dot(p.astype(BF16), v).astype(o_ref.dtype)
        psum = psum + p

    ovt = ovt_ref[...]
    p_hi = psum.astype(BF16)
    p_lo = (psum - p_hi.astype(F32)).astype(BF16)
    imp_t = _dot_nt(ovt, p_hi) + _dot_nt(ovt, p_lo)

    blk = lax.broadcasted_iota(I32, (SEL_ROWS, tq), 0)
    cur = lax.shift_right_logical(q0 + lax.broadcasted_iota(I32, (1, tq), 1), 6)
    in_range = (blk <= cur) & (blk < n_sel)
    forced = in_range & ((blk == 0) | (blk == cur) | (blk == cur - 1))
    k_free = (topk - 1 - (cur >= 1).astype(I32) - (cur >= 2).astype(I32)).astype(F32)
    neg = jnp.float32(-jnp.inf)
    x = jnp.where(in_range & jnp.logical_not(forced), imp_t, neg)
    n_grp = (n_sel + 7) // 8
    xg = [x[8 * r:8 * r + 8, :] for r in range(n_grp)]
    rank_g = [jnp.zeros((8, tq), F32) for _ in range(n_grp)]
    sub = lax.broadcasted_iota(I32, (8, tq), 0)
    for c in range(n_sel):
        vc = jnp.broadcast_to(x[c:c + 1, :], (8, tq))
        for r in range(n_grp):
            if 8 * r > c:
                hit = jnp.where(vc >= xg[r], 1.0, 0.0)
            elif 8 * r + 7 <= c:
                hit = jnp.where(vc > xg[r], 1.0, 0.0)
            else:
                hit = jnp.where(sub > c - 8 * r, jnp.where(vc >= xg[r], 1.0, 0.0), jnp.where(vc > xg[r], 1.0, 0.0))
            rank_g[r] = rank_g[r] + hit
    rank = jnp.concatenate(rank_g + [jnp.zeros((SEL_ROWS - 8 * n_grp, tq), F32)] * (SEL_ROWS > 8 * n_grp), axis=0)
    selected = forced | ((x > neg) & (rank < k_free))
    bias_t = jnp.where(selected, 0.0, MASK_VALUE)
    bias = jnp.concatenate([bias_t, jnp.zeros((LANES - SEL_ROWS, tq), F32)], axis=0).T
    lane = lax.broadcasted_iota(I32, (tq, LANES), 1)
    for h in range(HEADS_PER_GROUP):
        qa_ref[h, :, 0:HEAD_DIM] = q_ref[h]
        qa_ref[h, :, HEAD_DIM:2 * HEAD_DIM] = jnp.where(lane < SEL_ROWS, bias, qf_ref[0, h:h + 1, :]).astype(BF16)


def _cmp_attn(q_t, kvcmp, slopes, ovt, qfeat, B, tq=512):
    NH, T, dh = q_t.shape
    S = T // B
    nq = S // tq
    nhb = kvcmp.shape[2]
    n_sel = S // SEL_BLOCK
    assert n_sel <= SEL_ROWS
    topk = min(SEL_TOPK, n_sel)
    qmap = lambda bb, g, i: (g, bb * nq + i, 0)
    return pl.pallas_call(
        functools.partial(_cmp_attn_kernel, tq=tq, nhb=nhb, n_sel=n_sel, topk=topk),
        grid=(B, N_KV_GROUPS, nq),
        in_specs=[
            pl.BlockSpec((HEADS_PER_GROUP, tq, dh), qmap),
            pl.BlockSpec((1, 1, nhb, dh), lambda bb, g, i: (g, bb, 0, 0)),
            pl.BlockSpec((1, 1, nhb, dh), lambda bb, g, i: (N_KV_GROUPS + g, bb, 0, 0)),
            pl.BlockSpec((1, 8, LANES), lambda bb, g, i: (g, 0, 0)),
            pl.BlockSpec((SEL_ROWS, nhb), lambda bb, g, i: (0, 0)),
            pl.BlockSpec((1, 8, LANES), lambda bb, g, i: (g, 0, 0)),
        ],
        out_specs=[
            pl.BlockSpec((HEADS_PER_GROUP, tq, dh), qmap),
            pl.BlockSpec((HEADS_PER_GROUP, tq, 2 * dh), qmap),
        ],
        out_shape=[
            jax.ShapeDtypeStruct((NH, T, dh), BF16),
            jax.ShapeDtypeStruct((NH, T, 2 * dh), BF16),
        ],
        compiler_params=_cparams(("parallel", "parallel", "parallel")),
        name="cmp_attn",
    )(q_t, kvcmp, kvcmp, slopes, ovt, qfeat)


def _sel_attn_kernel(q_ref, k_ref, v_ref, o_ref, m_scr, acc_scr, s_buf, *, tq, tk):
    i = pl.program_id(2)
    q0 = i * tq
    nkt = k_ref.shape[1]
    m_scr[...] = jnp.full_like(m_scr, MASK_VALUE)
    acc_scr[...] = jnp.zeros_like(acc_scr)
    qpos = q0 + lax.broadcasted_iota(I32, (tq, 1), 0)

    def scores_to(t, slot):
        ka_t = k_ref[0, jnp.minimum(t, nkt - 1)]
        causal = (t * tk + lax.broadcasted_iota(I32, (1, tk), 1)) <= qpos
        for h in range(HEADS_PER_GROUP):
            s_buf[slot, h] = jnp.where(causal, _dot(q_ref[h], ka_t), MASK_VALUE)

    def absorb_from(t, slot):
        k0 = pl.multiple_of(jnp.minimum(t, nkt - 1) * tk, tk)
        va = v_ref[0, pl.ds(k0, tk), :]
        half = tq // 2
        for h in range(HEADS_PER_GROUP):
            m_prev = m_scr[h]
            m_new = jnp.maximum(m_prev, jnp.max(s_buf[slot, h], axis=-1, keepdims=True))
            alpha = jnp.exp(m_prev - m_new)
            p = jnp.concatenate([jnp.exp(s_buf[slot, h, 0:half, :] - m_new[0:half]).astype(BF16),
                                 jnp.exp(s_buf[slot, h, half:tq, :] - m_new[half:tq]).astype(BF16)], axis=0)
            acc_scr[h] = alpha * acc_scr[h] + _dot(p, va)
            m_scr[h] = m_new

    n_tiles = q0 // tk + 1

    def tile_pair(pr, carry):
        t = 2 * pr
        scores_to(t + 1, 1)
        absorb_from(t, 0)

        @pl.when(t + 1 < n_tiles)
        def _():
            scores_to(t + 2, 0)
            absorb_from(t + 1, 1)

        return carry

    scores_to(0, 0)
    lax.fori_loop(0, (n_tiles + 1) // 2, tile_pair, 0)
    for h in range(HEADS_PER_GROUP):
        o_ref[h] = (acc_scr[h, :, 0:HEAD_DIM] / acc_scr[h, :, HEAD_DIM:HEAD_DIM + 1]).astype(o_ref.dtype)


def _sel_attn(q_aug, k_aug_t, v_aug, B, tq=512, tk=SEL_KEY_TILE):
    NH, T, dq = q_aug.shape
    dh = dq // 2
    S = T // B
    assert tk % tq == 0 and S % tk == 0
    nq = S // tq
    nkt = S // tk
    qmap = lambda bb, g, i: (g, bb * nq + i, 0)
    return pl.pallas_call(
        functools.partial(_sel_attn_kernel, tq=tq, tk=tk),
        grid=(B, N_KV_GROUPS, nq),
        in_specs=[
            pl.BlockSpec((HEADS_PER_GROUP, tq, dq), qmap),
            pl.BlockSpec((1, nkt, dq, tk), lambda bb, g, i: (g, bb, 0, 0)),
            pl.BlockSpec((1, S, dq), lambda bb, g, i: (g, bb, 0)),
        ],
        out_specs=pl.BlockSpec((HEADS_PER_GROUP, tq, dh), qmap),
        out_shape=jax.ShapeDtypeStruct((NH, T, dh), BF16),
        scratch_shapes=[
            pltpu.VMEM((HEADS_PER_GROUP, tq, 1), F32),
            pltpu.VMEM((HEADS_PER_GROUP, tq, dq), F32),
            pltpu.VMEM((2, HEADS_PER_GROUP, tq, tk), F32),
        ],
        compiler_params=_cparams(("parallel", "parallel", "parallel")),
        name="nsa_sel_attn",
    )(q_aug, k_aug_t, v_aug)


def _win_attn_kernel(q_ref, k_ref, v_ref, o_ref, *, tq):
    i = pl.program_id(2)
    q0 = i * tq
    back = WINDOW // tq
    qpos = q0 + lax.broadcasted_iota(I32, (tq, 1), 0)
    tiles = []
    for d in range(back, -1, -1):
        kt = i - d
        ktc = jnp.maximum(kt, 0)
        kpos = kt * tq + lax.broadcasted_iota(I32, (1, tq), 1)
        if d == back:
            valid = ((qpos - kpos) < WINDOW) & (kt >= 0)
        elif d == 0:
            valid = kpos <= qpos
        else:
            valid = kt >= 0
        tiles.append((k_ref[0, ktc], v_ref[0, pl.ds(pl.multiple_of(ktc * tq, tq), tq), :], valid))
    for h in range(HEADS_PER_GROUP):
        q = q_ref[h]
        s_all = [jnp.where(valid, _dot(q, k_t), MASK_VALUE) for k_t, _, valid in tiles]
        m = functools.reduce(jnp.maximum, [jnp.max(s, axis=-1, keepdims=True) for s in s_all])
        acc = functools.reduce(lambda a, b: a + b,
                               [_dot(jnp.exp(s - m).astype(BF16), va) for s, (_, va, _) in zip(s_all, tiles)])
        o_ref[h] = (acc[:, 0:HEAD_DIM] / acc[:, HEAD_DIM:HEAD_DIM + 1]).astype(o_ref.dtype)


def _win_attn(q_aug, k_aug_t, v_aug, B, tq=256):
    NH, T, dq = q_aug.shape
    dh = dq // 2
    S = T // B
    assert WINDOW % tq == 0 and S % tq == 0
    nq = S // tq
    qmap = lambda bb, g, i: (g, bb * nq + i, 0)
    return pl.pallas_call(
        functools.partial(_win_attn_kernel, tq=tq),
        grid=(B, N_KV_GROUPS, nq),
        in_specs=[
            pl.BlockSpec((HEADS_PER_GROUP, tq, dq), qmap),
            pl.BlockSpec((1, nq, dq, tq), lambda bb, g, i: (g, bb, 0, 0)),
            pl.BlockSpec((1, S, dq), lambda bb, g, i: (g, bb, 0)),
        ],
        out_specs=pl.BlockSpec((HEADS_PER_GROUP, tq, dh), qmap),
        out_shape=jax.ShapeDtypeStruct((NH, T, dh), BF16),
        compiler_params=_cparams(("parallel", "parallel", "parallel")),
        name="nsa_win_attn",
    )(q_aug, k_aug_t, v_aug)


def _attn_out_kernel(oc_ref, os_ref, ow_ref, gt_ref, ex_ref, x_ref, w_ref, g_ref, b_ref, of_ref, op_ref, lhs):
    gt = gt_ref[...]
    gt_hi = gt.astype(BF16)
    gt_lo = (gt - gt_hi.astype(F32)).astype(BF16)
    gt2 = jnp.concatenate([gt_hi, gt_lo], axis=-1)
    ge = [_dot(gt2, ex_ref[br]) for br in range(N_BRANCHES)]
    for hd in range(N_HEADS):
        cols = slice(hd * HEAD_DIM, (hd + 1) * HEAD_DIM)
        o = (ge[0][:, cols] * oc_ref[hd].astype(F32) + ge[1][:, cols] * os_ref[hd].astype(F32)
             + ge[2][:, cols] * ow_ref[hd].astype(F32))
        lhs[:, cols] = o.astype(BF16)
    mix = _dot(lhs[...], w_ref[...])
    h = _layer_norm(DEEPNORM_ALPHA * x_ref[...] + mix, g_ref[...], b_ref[...])
    of_ref[...] = h
    op_ref[...] = _pack_halves(h)


def _attn_out_res_ln(o_cmp, o_sel, o_win, gates, x, w, g, b, tm=256):
    T, D = x.shape
    NH = o_cmp.shape[0]
    hmap = lambda i: (0, i, 0)
    row = lambda i: (i, 0)
    fix = lambda i: (0, 0)
    expand = np.zeros((N_BRANCHES, 2 * LANES, NH * HEAD_DIM), np.float32)
    for hd in range(NH):
        for br in range(N_BRANCHES):
            for piece in range(2):
                expand[br, piece * LANES + N_BRANCHES * hd + br, hd * HEAD_DIM:(hd + 1) * HEAD_DIM] = 1.0
    expand = jnp.asarray(expand, dtype=BF16)
    return pl.pallas_call(
        _attn_out_kernel,
        grid=(T // tm,),
        in_specs=[
            pl.BlockSpec((NH, tm, HEAD_DIM), hmap), pl.BlockSpec((NH, tm, HEAD_DIM), hmap),
            pl.BlockSpec((NH, tm, HEAD_DIM), hmap),
            pl.BlockSpec((tm, LANES), row),
            pl.BlockSpec((N_BRANCHES, 2 * LANES, NH * HEAD_DIM), lambda i: (0, 0, 0)),
            pl.BlockSpec((tm, D), row),
            pl.BlockSpec((D, D), fix),
            pl.BlockSpec((1, D), fix), pl.BlockSpec((1, D), fix),
        ],
        out_specs=[pl.BlockSpec((tm, D), row), pl.BlockSpec((tm, D // 2), row)],
        out_shape=[jax.ShapeDtypeStruct((T, D), F32), jax.ShapeDtypeStruct((T, D // 2), U32)],
        scratch_shapes=[pltpu.VMEM((tm, D), BF16)],
        compiler_params=_cparams(("parallel",), VMEM_LIMIT),
        name="attn_out_res_ln",
    )(o_cmp, o_sel, o_win, gates, expand, x, w, g, b)


def _bf16_pieces(x, n=3):
    out, rest = [], np.asarray(x, np.float32)
    for _ in range(n):
        piece = rest.astype(BF16).astype(np.float32)
        out.append(piece)
        rest = (rest - piece).astype(np.float32)
    return out


def _nsa_tables(S):
    nhb = S // CMP_STRIDE
    n_sel = S // SEL_BLOCK
    heads = np.arange(1, N_HEADS + 1, dtype=np.float32)
    slope = (2.0 ** (-8.0 * heads / N_HEADS)).astype(np.float32)
    slopes = np.zeros((N_KV_GROUPS, 8, LANES), np.float32)
    slopes[:, :HEADS_PER_GROUP, :] = slope.reshape(N_KV_GROUPS, HEADS_PER_GROUP, 1)
    c0 = np.arange(nhb)[None, :] * CMP_STRIDE
    j0 = np.arange(SEL_ROWS)[:, None] * SEL_BLOCK
    ovt = np.maximum(np.minimum(c0 + CMP_BLOCK, j0 + SEL_BLOCK) - np.maximum(c0, j0), 0).astype(np.float32) / CMP_BLOCK
    ovt[:, nhb - 1] = 0.0
    ovt[n_sel:, :] = 0.0
    qfeat = np.zeros((N_KV_GROUPS, 8, LANES), np.float32)
    pieces = _bf16_pieces(slope)
    for t, piece in enumerate(pieces):
        qfeat[:, :HEADS_PER_GROUP, SEL_ROWS + t] = piece.reshape(N_KV_GROUPS, HEADS_PER_GROUP)
        qfeat[:, :HEADS_PER_GROUP, SEL_ROWS + 3 + t] = piece.reshape(N_KV_GROUPS, HEADS_PER_GROUP)
    kpos = np.arange(S)
    kx = np.zeros((S, LANES), np.float32)
    kx[kpos, kpos // SEL_BLOCK] = 1.0
    kx[:, SEL_ROWS:SEL_ROWS + 3] = (SEL_BLOCK * (kpos // SEL_BLOCK))[:, None]
    kx[:, SEL_ROWS + 3:SEL_ROWS + 6] = (kpos % SEL_BLOCK)[:, None]
    return jnp.asarray(slopes), jnp.asarray(ovt, dtype=BF16), jnp.asarray(qfeat), jnp.asarray(kx, dtype=BF16)


def _nsa_layer_res_ln(hf, hb, B, kv_w, cmp_pos, cmp_w1, cmp_w2, w_qg, w_o, g, b):
    T, D = hf.shape
    S = T // B
    hd = N_HEADS * HEAD_DIM
    gsz = N_KV_GROUPS * HEAD_DIM
    slopes, ovt, qfeat, kx = _nsa_tables(S)

    kvw = kv_w.astype(BF16)
    kvc = _proj_heads(hb, kvw[:, :2 * gsz], F32)
    kv_t = _proj_heads(hb, kvw[:, 2 * gsz:], BF16)
    wq = w_qg[:, :hd].astype(BF16)
    wgt = jnp.zeros((D, LANES), F32).at[:, :N_BRANCHES * N_HEADS].set(w_qg[:, hd:]).astype(BF16)
    q_t = _proj_heads(hb, wq, BF16, scale=HEAD_DIM ** -0.5)
    gates = _gates(hb, wgt)

    pos = cmp_pos.reshape(2, 2, CMP_STRIDE * HEAD_DIM)
    kvcmp = _cmp_mlp(kvc, pos, cmp_w1.astype(BF16), cmp_w2.astype(BF16), B)
    o_cmp, q_aug = _cmp_attn(q_t, kvcmp, slopes, ovt, qfeat, B)
    kx_all = jnp.broadcast_to(jnp.tile(kx, (B, 1))[None], (N_KV_GROUPS, T, LANES))
    k_aug = jnp.concatenate([kv_t[:N_KV_GROUPS], kx_all], axis=-1)
    k_aug_t = k_aug.reshape(N_KV_GROUPS, T // SEL_KEY_TILE, SEL_KEY_TILE, 2 * HEAD_DIM).transpose(0, 1, 3, 2)
    ones_col = jnp.zeros((1, 1, LANES), BF16).at[0, 0, 0].set(1.0)
    v_aug = jnp.concatenate([kv_t[N_KV_GROUPS:2 * N_KV_GROUPS],
                             jnp.broadcast_to(ones_col, (N_KV_GROUPS, T, LANES))], axis=-1)
    o_sel = _sel_attn(q_aug, k_aug_t, v_aug, B)
    win_tq = 256
    lane_id = jnp.arange(LANES)[None, None, :]
    kw_aug = jnp.concatenate([kv_t[2 * N_KV_GROUPS:3 * N_KV_GROUPS],
                              jnp.where(lane_id < SEL_ROWS, jnp.zeros((), BF16), kx_all)], axis=-1)
    kw_aug_t = kw_aug.reshape(N_KV_GROUPS, T // win_tq, win_tq, 2 * HEAD_DIM).transpose(0, 1, 3, 2)
    vw_aug = jnp.concatenate([kv_t[3 * N_KV_GROUPS:],
                              jnp.broadcast_to(ones_col, (N_KV_GROUPS, T, LANES))], axis=-1)
    o_win = _win_attn(q_aug, kw_aug_t, vw_aug, B, tq=win_tq)
    return _attn_out_res_ln(o_cmp, o_sel, o_win, gates, hf, w_o.astype(BF16), g, b)


def kernel(x, conv_w_pw1, conv_b_pw1, conv_w_dw, conv_b_dw, conv_ln_g, conv_ln_b, conv_w_pw2, kv_w, cmp_pos, cmp_w1, cmp_w2, nsa_w_qg, nsa_w_o, moe_wg, moe_bg, moe_we, moe_be, moe_w_gu, moe_w_down, ln_g, ln_b):
    B, S, D = x.shape
    T = B * S
    hf = x.reshape(T, D)
    hb = hf.astype(BF16)
    r = lambda v: v.reshape(1, -1)
    for l in range(DEPTH):
        if l < N_A_LAYERS:
            glu = _pw1_glu(hb, conv_w_pw1[l].astype(BF16), r(conv_b_pw1[l]))
            conv = _dwconv(glu.reshape(B, S, D), conv_w_dw[l], r(conv_b_dw[l])).reshape(T, D)
            hf, hp = _pw2_res_ln(conv, hf, conv_w_pw2[l].astype(BF16), r(conv_ln_g[l]), r(conv_ln_b[l]),
                                 r(ln_g[l, 0]), r(ln_b[l, 0]))
        else:
            j = l - N_A_LAYERS
            hf, hp = _nsa_layer_res_ln(hf, hb, B, kv_w, cmp_pos, cmp_w1, cmp_w2, nsa_w_qg[j], nsa_w_o[j],
                                       r(ln_g[l, 0]), r(ln_b[l, 0]))
        hf, hb = _hier_moe_res_ln(hf, hp, moe_wg[l], moe_bg[l], moe_we[l], moe_be[l], moe_w_gu, moe_w_down, l,
                                  r(ln_g[l, 1]), r(ln_b[l, 1]))
    return hf.reshape(B, S, D)
```

```python
import functools

import numpy as np
import jax
import jax.numpy as jnp
from jax import lax
from jax.experimental import pallas as pl
from jax.experimental.pallas import tpu as pltpu

F32 = jnp.float32
BF16 = jnp.bfloat16
I32 = jnp.int32
U32 = jnp.uint32

DEPTH = 2
N_A_LAYERS = DEPTH // 2
DEEPNORM_ALPHA = (2.0 * DEPTH) ** 0.25
LN_EPS = 1e-5
CONV_WIDTH = 31
N_HEADS = 16
HEAD_DIM = 128
N_KV_GROUPS = 4
HEADS_PER_GROUP = N_HEADS // N_KV_GROUPS
CMP_BLOCK = 32
CMP_STRIDE = 16
SEL_BLOCK = 64
SEL_TOPK = 16
WINDOW = 512
N_BRANCHES = 3
MASK_VALUE = -1e30
N_GROUPS = 4
EXPERTS_PER_GROUP = 8
N_EXPERTS = N_GROUPS * EXPERTS_PER_GROUP
MOE_CHUNK = 128

LANES = 128
SUBLANES = 8
VMEM_LIMIT = 56 * 1024 * 1024

CONV_HALO = 32
SEL_ROWS = 64
SEL_KEY_TILE = 512
W_DMA_SPLIT = 4


def _cparams(sem, vmem=None):
    return pltpu.CompilerParams(dimension_semantics=sem, vmem_limit_bytes=vmem)


def _sigmoid(x):
    return 1.0 / (1.0 + jnp.exp(-x))


def _layer_norm(x, g, b):
    mu = jnp.mean(x, axis=-1, keepdims=True)
    xc = x - mu
    var = jnp.mean(xc * xc, axis=-1, keepdims=True)
    return xc * lax.rsqrt(var + LN_EPS) * g + b


def _pack_halves(x):
    n = x.shape[-1] // 2
    lo = lax.bitcast_convert_type(x[:, :n].astype(BF16).astype(F32), U32)
    hi = lax.bitcast_convert_type(x[:, n:].astype(BF16).astype(F32), U32)
    return lax.shift_right_logical(lo, jnp.uint32(16)) | hi


def _unpack_halves(p):
    lo = lax.bitcast_convert_type(lax.shift_left(p, jnp.uint32(16)), F32)
    hi = lax.bitcast_convert_type(p & jnp.uint32(0xFFFF0000), F32)
    return lo, hi


def _dot(a, b):
    return jnp.dot(a, b, preferred_element_type=F32)


def _dot_nt(a, b):
    return lax.dot_general(a, b, (((1,), (1,)), ((), ())), preferred_element_type=F32)


def _pw1_glu_kernel(x_ref, wa_ref, wg_ref, ba_ref, bg_ref, o_ref):
    x = x_ref[...]
    a = _dot(x, wa_ref[...]) + ba_ref[...]
    g = _dot(x, wg_ref[...]) + bg_ref[...]
    o_ref[...] = a * _sigmoid(g)


def _pw1_glu(xb, w, b, tm=1024, tn=512):
    T, D = xb.shape
    nj = D // tn
    return pl.pallas_call(
        _pw1_glu_kernel,
        grid=(nj, T // tm),
        in_specs=[
            pl.BlockSpec((tm, D), lambda j, i: (i, 0)),
            pl.BlockSpec((D, tn), lambda j, i: (0, j)),
            pl.BlockSpec((D, tn), lambda j, i: (0, j + nj)),
            pl.BlockSpec((1, tn), lambda j, i: (0, j)),
            pl.BlockSpec((1, tn), lambda j, i: (0, j + nj)),
        ],
        out_specs=pl.BlockSpec((tm, tn), lambda j, i: (i, j)),
        out_shape=jax.ShapeDtypeStruct((T, D), F32),
        compiler_params=_cparams(("parallel", "parallel"), VMEM_LIMIT),
        name="pw1_glu",
    )(xb, w, w, b, b)


def _dwconv_kernel(x_ref, halo_ref, w_ref, b_ref, o_ref, xsh, *, ts, tc, rc):
    i = pl.program_id(1)
    xsh[0, 0:CONV_HALO, :] = jnp.where(i > 0, halo_ref[0], 0.0)
    xsh[0, CONV_HALO:CONV_HALO + ts, :] = x_ref[0]
    span = ts + CONV_HALO - SUBLANES
    for p in range(1, SUBLANES):
        xsh[p, 0:span, :] = xsh[0, p:p + span, :]
    off = CONV_HALO - (CONV_WIDTH - 1)
    for c0 in range(0, tc, LANES):
        for r0 in range(0, ts, rc):
            acc = jnp.broadcast_to(b_ref[:, c0:c0 + LANES], (rc, LANES))
            for k in range(CONV_WIDTH):
                p = (k + off) % SUBLANES
                base = r0 + k + off - p
                acc = acc + w_ref[k:k + 1, c0:c0 + LANES] * xsh[p, base:base + rc, c0:c0 + LANES]
            o_ref[0, r0:r0 + rc, c0:c0 + LANES] = acc


def _dwconv(x, w, b, ts=512, tc=512, rc=128):
    B, S, D = x.shape
    hb = ts // CONV_HALO
    return pl.pallas_call(
        functools.partial(_dwconv_kernel, ts=ts, tc=tc, rc=rc),
        grid=(B, S // ts, D // tc),
        in_specs=[
            pl.BlockSpec((1, ts, tc), lambda bb, i, c: (bb, i, c)),
            pl.BlockSpec((1, CONV_HALO, tc), lambda bb, i, c: (bb, jnp.maximum(i * hb - 1, 0), c)),
            pl.BlockSpec((CONV_WIDTH, tc), lambda bb, i, c: (0, c)),
            pl.BlockSpec((1, tc), lambda bb, i, c: (0, c)),
        ],
        out_specs=pl.BlockSpec((1, ts, tc), lambda bb, i, c: (bb, i, c)),
        out_shape=jax.ShapeDtypeStruct((B, S, D), F32),
        scratch_shapes=[pltpu.VMEM((SUBLANES, CONV_HALO + ts, tc), F32)],
        compiler_params=_cparams(("parallel", "parallel", "parallel")),
        name="dwconv",
    )(x, x, w, b)


def _pw2_kernel(c_ref, x_ref, w_ref, cg_ref, cb_ref, g_ref, b_ref, of_ref, op_ref):
    y = _layer_norm(c_ref[...], cg_ref[...], cb_ref[...])
    y = y * _sigmoid(y)
    mix = _dot(y.astype(BF16), w_ref[...])
    h = _layer_norm(DEEPNORM_ALPHA * x_ref[...] + mix, g_ref[...], b_ref[...])
    of_ref[...] = h
    op_ref[...] = _pack_halves(h)


def _pw2_res_ln(conv, x, w, cg, cb, g, b, tm=256):
    T, D = x.shape
    row = lambda i: (i, 0)
    fix = lambda i: (0, 0)
    return pl.pallas_call(
        _pw2_kernel,
        grid=(T // tm,),
        in_specs=[
            pl.BlockSpec((tm, D), row), pl.BlockSpec((tm, D), row),
            pl.BlockSpec((D, D), fix),
            pl.BlockSpec((1, D), fix), pl.BlockSpec((1, D), fix),
            pl.BlockSpec((1, D), fix), pl.BlockSpec((1, D), fix),
        ],
        out_specs=[pl.BlockSpec((tm, D), row), pl.BlockSpec((tm, D // 2), row)],
        out_shape=[jax.ShapeDtypeStruct((T, D), F32), jax.ShapeDtypeStruct((T, D // 2), U32)],
        compiler_params=_cparams(("parallel",), VMEM_LIMIT),
        name="pw2_res_ln",
    )(conv, x, w, cg, cb, g, b)


def _router_kernel(h_ref, w_ref, b_ref, ri_ref, rw_ref, cnt_ref, *, tm):
    s = pl.program_id(0)

    @pl.when(s == 0)
    def _():
        cnt_ref[...] = jnp.zeros_like(cnt_ref)

    h = h_ref[...]
    w = w_ref[...]
    h_hi = h.astype(BF16)
    h_lo = (h - h_hi.astype(F32)).astype(BF16)
    w_hi = w.astype(BF16)
    w_lo = (w - w_hi.astype(F32)).astype(BF16)
    lg = _dot(h_hi, w_hi) + _dot(h_lo, w_hi) + _dot(h_hi, w_lo) + b_ref[...]

    lane = lax.broadcasted_iota(I32, (tm, LANES), 1)
    lanef = lane.astype(F32)
    neg = jnp.float32(-jnp.inf)
    big = jnp.float32(1e9)

    is_g = lane < N_GROUPS
    lgm = jnp.where(is_g, lg, neg)
    mg = jnp.max(lgm, axis=-1, keepdims=True)
    grp = jnp.min(jnp.where(lgm == mg, lanef, big), axis=-1, keepdims=True)
    sg = jnp.sum(jnp.where(is_g, jnp.exp(lg - mg), 0.0), axis=-1, keepdims=True)
    gw = 1.0 / sg

    lo = N_GROUPS + grp * EXPERTS_PER_GROUP
    in_g = (lanef >= lo) & (lanef < lo + EXPERTS_PER_GROUP)
    le = jnp.where(in_g, lg, neg)
    m1 = jnp.max(le, axis=-1, keepdims=True)
    i1 = jnp.min(jnp.where(le == m1, lanef, big), axis=-1, keepdims=True)
    le2 = jnp.where(lanef == i1, neg, le)
    m2 = jnp.max(le2, axis=-1, keepdims=True)
    i2 = jnp.min(jnp.where(le2 == m2, lanef, big), axis=-1, keepdims=True)
    t = jnp.exp(m2 - m1)
    p1 = 1.0 / (1.0 + t)
    w1 = p1 * gw
    w2 = (t * p1) * gw
    e1 = i1 - N_GROUPS
    e2 = i2 - N_GROUPS

    oh1 = (lanef == e1)
    oh2 = (lanef == e2)
    oh = (oh1 | oh2).astype(F32)
    r_i = lax.broadcasted_iota(I32, (tm, tm), 0)
    c_i = lax.broadcasted_iota(I32, (tm, tm), 1)
    tri = (c_i < r_i).astype(BF16)
    cs = _dot(tri, oh.astype(BF16)) + cnt_ref[...].astype(F32)
    rank1 = jnp.sum(jnp.where(oh1, cs, 0.0), axis=-1, keepdims=True)
    rank2 = jnp.sum(jnp.where(oh2, cs, 0.0), axis=-1, keepdims=True)
    cnt_ref[...] = cnt_ref[...] + jnp.sum(oh, axis=0, keepdims=True).astype(I32)

    ri = jnp.where(lane == 0, e1, jnp.where(lane == 1, e2, jnp.where(lane == 2, rank1, jnp.where(lane == 3, rank2, 0.0))))
    ri_ref[...] = ri.T[0:SUBLANES, :].astype(I32)
    rw_ref[...] = jnp.where(lane == 0, w1, jnp.where(lane == 1, w2, 0.0))


def _router(h, wcat, bcat, tm=256):
    T, D = h.shape
    return pl.pallas_call(
        functools.partial(_router_kernel, tm=tm),
        grid=(T // tm,),
        in_specs=[
            pl.BlockSpec((tm, D), lambda i: (i, 0)),
            pl.BlockSpec((D, LANES), lambda i: (0, 0)),
            pl.BlockSpec((1, LANES), lambda i: (0, 0)),
        ],
        out_specs=[
            pl.BlockSpec((SUBLANES, tm), lambda i: (0, i)),
            pl.BlockSpec((tm, LANES), lambda i: (i, 0)),
            pl.BlockSpec((1, LANES), lambda i: (0, 0)),
        ],
        out_shape=[
            jax.ShapeDtypeStruct((SUBLANES, T), I32),
            jax.ShapeDtypeStruct((T, LANES), F32),
            jax.ShapeDtypeStruct((1, LANES), I32),
        ],
        compiler_params=_cparams(("arbitrary",)),
        name="moe_router",
    )(h, wcat, bcat)


def _dispatch_kernel(dest_ref, zs_ref, zc_ref, nu_ref, x_ref, xs_hbm, zblk, sem, *, tm, n_tok, n_chunks):
    s = pl.program_id(0)

    def row_copy(r, d):
        return pltpu.make_async_copy(x_ref.at[pl.ds(r, 1)], xs_hbm.at[pl.ds(d, 1)], sem)

    def zero_row_copy(d):
        return pltpu.make_async_copy(zblk.at[pl.ds(0, 1)], xs_hbm.at[pl.ds(d, 1)], sem)

    def zero_chunk_copy(c):
        return pltpu.make_async_copy(zblk, xs_hbm.at[pl.ds(pl.multiple_of(c * MOE_CHUNK, MOE_CHUNK), MOE_CHUNK)], sem)

    @pl.when(s == 0)
    def _():
        zblk[...] = jnp.zeros_like(zblk)

        def fill(start):
            def per_expert(e, c):
                def per_row(r, c2):
                    cp = zero_row_copy(zs_ref[e] + r)
                    if start:
                        cp.start()
                    else:
                        cp.wait()
                    return c2
                lax.fori_loop(0, zc_ref[e], per_row, 0)
                return c
            lax.fori_loop(0, N_EXPERTS, per_expert, 0)

            def per_chunk(c, c2):
                cp = zero_chunk_copy(c)
                if start:
                    cp.start()
                else:
                    cp.wait()
                return c2
            lax.fori_loop(nu_ref[0], n_chunks, per_chunk, 0)

        fill(True)
        fill(False)

    base = s * tm

    def issue(r, c):
        t = base + r
        row_copy(r, dest_ref[t]).start(priority=0)
        row_copy(r, dest_ref[n_tok + t]).start(priority=1)
        return c

    lax.fori_loop(0, tm, issue, 0, unroll=8)

    def drain(r, c):
        row_copy(0, 0).wait()
        row_copy(0, 0).wait()
        return c

    lax.fori_loop(0, tm, drain, 0, unroll=8)


def _dispatch(x, dest_flat, zstart, zcount, n_used, n_rows, tm=256):
    T, W = x.shape
    return pl.pallas_call(
        functools.partial(_dispatch_kernel, tm=tm, n_tok=T, n_chunks=n_rows // MOE_CHUNK),
        grid_spec=pltpu.PrefetchScalarGridSpec(
            num_scalar_prefetch=4,
            grid=(T // tm,),
            in_specs=[pl.BlockSpec((tm, W), lambda i, *_: (i, 0))],
            out_specs=pl.BlockSpec(memory_space=pl.ANY),
            scratch_shapes=[pltpu.VMEM((MOE_CHUNK, W), x.dtype), pltpu.SemaphoreType.DMA(())],
        ),
        out_shape=jax.ShapeDtypeStruct((n_rows, W), x.dtype),
        compiler_params=_cparams(("arbitrary",)),
        name="moe_dispatch",
    )(dest_flat, zstart, zcount, n_used, x)


def _moe_mlp_kernel(ce_ref, nu_ref, first_ref, nxt_ref, x_ref, wgu_hbm, wd_hbm, o_ref,
                    wgu_stage, wd_stage, wgu_bf, wd_bf, sem, *, F, layer):
    c = pl.program_id(0)

    def w_copies(e, slot):
        copies = []
        for ref, stage, s_idx in ((wgu_hbm, wgu_stage, 0), (wd_hbm, wd_stage, 1)):
            rows = stage.shape[1] // W_DMA_SPLIT
            for part in range(W_DMA_SPLIT):
                rs = pl.ds(part * rows, rows)
                copies.append(pltpu.make_async_copy(ref.at[layer, e, rs], stage.at[slot, rs], sem.at[s_idx, slot]))
        return copies

    def start_all(e, slot):
        for cp in w_copies(e, slot):
            cp.start(priority=1)

    @pl.when(c < nu_ref[0])
    def _():
        @pl.when(first_ref[c] > 0)
        def _():
            slot = first_ref[c] - 1

            @pl.when(c == 0)
            def _():
                start_all(ce_ref[0], slot)

            @pl.when(nxt_ref[c] >= 0)
            def _():
                start_all(nxt_ref[c], 1 - slot)

            for cp in w_copies(ce_ref[c], slot):
                cp.wait()
            wgu_bf[...] = wgu_stage[slot].astype(BF16)
            wd_bf[...] = wd_stage[slot].astype(BF16)

        x_lo, x_hi = _unpack_halves(x_ref[...])
        half = x_lo.shape[-1]
        gu = _dot(x_lo.astype(BF16), wgu_bf[0:half, :]) + _dot(x_hi.astype(BF16), wgu_bf[half:2 * half, :])
        g = gu[:, :F]
        u = gu[:, F:]
        hmid = (g * _sigmoid(g)) * u
        o_ref[...] = _pack_halves(_dot(hmid.astype(BF16), wd_bf[...]))

    @pl.when(c >= nu_ref[0])
    def _():
        o_ref[...] = jnp.zeros_like(o_ref)


def _moe_mlp(xs, chunk_e, n_used, first, nxt, w_gu, w_down, layer):
    n_rows, W = xs.shape
    D = 2 * W
    F2 = w_gu.shape[-1]
    F = F2 // 2
    n_chunks = n_rows // MOE_CHUNK

    def x_map(c, ce, nu, fi, nx):
        return (jnp.minimum(c, nu[0] - 1), 0)

    return pl.pallas_call(
        functools.partial(_moe_mlp_kernel, F=F, layer=layer),
        grid_spec=pltpu.PrefetchScalarGridSpec(
            num_scalar_prefetch=4,
            grid=(n_chunks,),
            in_specs=[
                pl.BlockSpec((MOE_CHUNK, W), x_map),
                pl.BlockSpec(memory_space=pl.ANY),
                pl.BlockSpec(memory_space=pl.ANY),
            ],
            out_specs=pl.BlockSpec((MOE_CHUNK, W), lambda c, ce, nu, fi, nx: (c, 0)),
            scratch_shapes=[
                pltpu.VMEM((2, D, F2), F32), pltpu.VMEM((2, F, D), F32),
                pltpu.VMEM((D, F2), BF16), pltpu.VMEM((F, D), BF16),
                pltpu.SemaphoreType.DMA((2, 2)),
            ],
        ),
        out_shape=jax.ShapeDtypeStruct((n_rows, W), U32),
        compiler_params=_cparams(("arbitrary",), VMEM_LIMIT),
        name="moe_mlp",
    )(chunk_e, n_used, first, nxt, xs, w_gu, w_down)


def _combine_kernel(dest_ref, y_hbm, rw_ref, h_ref, g_ref, b_ref, of_ref, ob_ref, gbuf, sem, *, tm, nsteps):
    s = pl.program_id(0)

    def row_copy(src_row, slot, k, r):
        return pltpu.make_async_copy(y_hbm.at[pl.ds(src_row, 1)], gbuf.at[slot, k, pl.ds(r, 1)], sem.at[slot])

    def issue(step, slot):
        base = step * tm

        def body(r, c):
            t = base + r
            row_copy(dest_ref[t], slot, 0, r).start(priority=0)
            row_copy(dest_ref[tm * nsteps + t], slot, 1, r).start(priority=1)
            return c

        lax.fori_loop(0, tm, body, 0, unroll=8)

    @pl.when(s == 0)
    def _():
        issue(0, 0)

    @pl.when(s + 1 < nsteps)
    def _():
        issue(s + 1, (s + 1) % 2)

    slot = s % 2

    def drain(r, c):
        row_copy(0, slot, 0, 0).wait()
        row_copy(0, slot, 1, 0).wait()
        return c

    lax.fori_loop(0, tm, drain, 0, unroll=8)

    w0 = rw_ref[:, 0:1]
    w1 = rw_ref[:, 1:2]
    lo0, hi0 = _unpack_halves(gbuf[slot, 0])
    lo1, hi1 = _unpack_halves(gbuf[slot, 1])
    y = jnp.concatenate([w0 * lo0 + w1 * lo1, w0 * hi0 + w1 * hi1], axis=-1)
    h = _layer_norm(DEEPNORM_ALPHA * h_ref[...] + y, g_ref[...], b_ref[...])
    of_ref[...] = h
    ob_ref[...] = h.astype(BF16)


def _combine_res_ln(y_rows, dest_flat, rw, h, g, b, tm=256):
    T, D = h.shape
    nsteps = T // tm
    row = lambda i, d: (i, 0)
    fix = lambda i, d: (0, 0)
    return pl.pallas_call(
        functools.partial(_combine_kernel, tm=tm, nsteps=nsteps),
        grid_spec=pltpu.PrefetchScalarGridSpec(
            num_scalar_prefetch=1,
            grid=(nsteps,),
            in_specs=[
                pl.BlockSpec(memory_space=pl.ANY),
                pl.BlockSpec((tm, LANES), row),
                pl.BlockSpec((tm, D), row),
                pl.BlockSpec((1, D), fix), pl.BlockSpec((1, D), fix),
            ],
            out_specs=[pl.BlockSpec((tm, D), row), pl.BlockSpec((tm, D), row)],
            scratch_shapes=[pltpu.VMEM((2, 2, tm, D // 2), U32), pltpu.SemaphoreType.DMA((2,))],
        ),
        out_shape=[jax.ShapeDtypeStruct((T, D), F32), jax.ShapeDtypeStruct((T, D), BF16)],
        compiler_params=_cparams(("arbitrary",), VMEM_LIMIT),
        name="moe_combine",
    )(dest_flat, y_rows, rw, h, g, b)


def _hier_moe_res_ln(h, hp, wg, bg, we, be, w_gu, w_down, layer, g, b):
    T, D = h.shape
    wcat = jnp.zeros((D, LANES), F32).at[:, :N_GROUPS].set(wg).at[:, N_GROUPS:N_GROUPS + N_EXPERTS].set(we)
    bcat = jnp.zeros((1, LANES), F32).at[0, :N_GROUPS].set(bg).at[0, N_GROUPS:N_GROUPS + N_EXPERTS].set(be)
    ri, rw, cnt = _router(h, wcat, bcat)

    counts = cnt[0, :N_EXPERTS]
    padded = (counts + MOE_CHUNK - 1) // MOE_CHUNK * MOE_CHUNK
    pad_end = jnp.cumsum(padded)
    pad_start = pad_end - padded
    A = T * 2
    n_chunks = (A + N_EXPERTS * (MOE_CHUNK - 1) + MOE_CHUNK - 1) // MOE_CHUNK
    n_rows = n_chunks * MOE_CHUNK
    seg = jnp.zeros((2, T), I32)
    for e in range(N_EXPERTS):
        seg = jnp.where(ri[0:2] == e, pad_start[e], seg)
    dest_flat = (seg + ri[2:4]).reshape(-1).astype(I32)
    n_used = (pad_end[-1] // MOE_CHUNK).astype(I32)
    cidx = jnp.minimum(jnp.arange(n_chunks, dtype=I32), n_used - 1)
    chunk_e = jnp.sum((pad_end[None, :] <= (cidx * MOE_CHUNK)[:, None]).astype(I32), axis=1)
    chunk_e = jnp.minimum(chunk_e, N_EXPERTS - 1)
    zstart = (pad_start + counts).astype(I32)
    zcount = (padded - counts).astype(I32)
    carange = jnp.arange(n_chunks, dtype=I32)
    prev_e = jnp.concatenate([jnp.full((1,), -1, I32), chunk_e[:-1]])
    first = ((chunk_e != prev_e) & (carange < n_used)).astype(I32)
    first = first * (1 + (jnp.cumsum(first) - 1) % 2)
    run_end = (pad_end[chunk_e] // MOE_CHUNK).astype(I32)
    nxt = jnp.where(run_end < n_used, chunk_e[jnp.minimum(run_end, n_chunks - 1)], -1).astype(I32)

    n_used = n_used.reshape(1)
    xs = _dispatch(hp, dest_flat, zstart, zcount, n_used, n_rows)
    ys = _moe_mlp(xs, chunk_e, n_used, first, nxt, w_gu, w_down, layer)
    return _combine_res_ln(ys, dest_flat, rw, h, g, b)


def _proj_heads_kernel(x_ref, w_ref, o_ref, *, nh, scale):
    acc = _dot(x_ref[...], w_ref[...])
    if scale != 1.0:
        acc = acc * scale
    for c in range(nh):
        o_ref[c] = acc[:, c * HEAD_DIM:(c + 1) * HEAD_DIM].astype(o_ref.dtype)


def _proj_heads(xb, w, out_dtype, scale=1.0, tm=1024, tn=512):
    T, D = xb.shape
    N = w.shape[1]
    nh = tn // HEAD_DIM
    return pl.pallas_call(
        functools.partial(_proj_heads_kernel, nh=nh, scale=scale),
        grid=(N // tn, T // tm),
        in_specs=[
            pl.BlockSpec((tm, D), lambda j, i: (i, 0)),
            pl.BlockSpec((D, tn), lambda j, i: (0, j)),
        ],
        out_specs=pl.BlockSpec((nh, tm, HEAD_DIM), lambda j, i: (j, i, 0)),
        out_shape=jax.ShapeDtypeStruct((N // HEAD_DIM, T, HEAD_DIM), out_dtype),
        compiler_params=_cparams(("parallel", "parallel"), VMEM_LIMIT),
        name="proj_heads",
    )(xb, w)


def _gates_kernel(x_ref, w_ref, o_ref):
    o_ref[...] = _sigmoid(_dot(x_ref[...], w_ref[...]))


def _gates(xb, w, tm=512):
    T, D = xb.shape
    return pl.pallas_call(
        _gates_kernel,
        grid=(T // tm,),
        in_specs=[pl.BlockSpec((tm, D), lambda i: (i, 0)), pl.BlockSpec((D, LANES), lambda i: (0, 0))],
        out_specs=pl.BlockSpec((tm, LANES), lambda i: (i, 0)),
        out_shape=jax.ShapeDtypeStruct((T, LANES), F32),
        compiler_params=_cparams(("parallel",)),
        name="nsa_gates",
    )(xb, w)


def _cmp_mlp_kernel(x_ref, pos_ref, w1_ref, w2_ref, o_ref, *, nhb):
    x = x_ref[0, 0]
    half = x.shape[1]
    lo = (x + pos_ref[0, 0:1, :]).astype(BF16)
    hi = (x + pos_ref[0, 1:2, :]).astype(BF16)
    a_lo = _dot(lo, w1_ref[0, 0:half, :])
    a_hi = _dot(hi, w1_ref[0, half:2 * half, :])
    pre = a_lo + pltpu.roll(a_hi, nhb - 1, 0)
    hid = 0.5 * pre * (1.0 + jnp.tanh(0.7978845608028654 * (pre + 0.044715 * pre * pre * pre)))
    comp = _dot(hid.astype(BF16), w2_ref[0])
    row = lax.broadcasted_iota(I32, comp.shape, 0)
    o_ref[0, 0] = jnp.where(row < nhb - 1, comp, 0.0).astype(o_ref.dtype)


def _cmp_mlp(kvc, pos, w1, w2, B):
    R, T, dh = kvc.shape
    S = T // B
    nhb = S // CMP_STRIDE
    half = CMP_STRIDE * dh
    x4 = kvc.reshape(R, B, nhb, half)
    H = w1.shape[-1]
    return pl.pallas_call(
        functools.partial(_cmp_mlp_kernel, nhb=nhb),
        grid=(R, B),
        in_specs=[
            pl.BlockSpec((1, 1, nhb, half), lambda r, bb: (r, bb, 0, 0)),
            pl.BlockSpec((1, 2, half), lambda r, bb: (r // N_KV_GROUPS, 0, 0)),
            pl.BlockSpec((1, 2 * half, H), lambda r, bb: (r // N_KV_GROUPS, 0, 0)),
            pl.BlockSpec((1, H, dh), lambda r, bb: (r // N_KV_GROUPS, 0, 0)),
        ],
        out_specs=pl.BlockSpec((1, 1, nhb, dh), lambda r, bb: (r, bb, 0, 0)),
        out_shape=jax.ShapeDtypeStruct((R, B, nhb, dh), BF16),
        compiler_params=_cparams(("parallel", "parallel"), VMEM_LIMIT),
        name="cmp_mlp",
    )(x4, pos, w1, w2)


def _cmp_attn_kernel(q_ref, k_ref, v_ref, sl_ref, ovt_ref, qf_ref, o_ref, qa_ref, *, tq, nhb, n_sel, topk):
    i = pl.program_id(2)
    q0 = i * tq
    qpos = q0 + lax.broadcasted_iota(I32, (tq, 1), 0)
    n = lax.broadcasted_iota(I32, (1, nhb), 1)
    dist = qpos - (n * CMP_STRIDE + (CMP_BLOCK - 1))
    valid = (dist >= 0) & (n < nhb - 1)
    distf = dist.astype(F32)
    k = k_ref[0, 0]
    v = v_ref[0, 0]
    psum = jnp.zeros((tq, nhb), F32)
    for h in range(HEADS_PER_GROUP):
        s = _dot_nt(q_ref[h], k) - sl_ref[0, h:h + 1, 0:1] * distf
        s = jnp.where(valid, s, MASK_VALUE)
        m = jnp.max(s, axis=-1, keepdims=True)
        e = jnp.where(valid, jnp.exp(s - m), 0.0)
        l = jnp.sum(e, axis=-1, keepdims=True)
        p = e / jnp.maximum(l, 1e-30)
        o_ref[h] = _dot(p.astype(BF16), v).astype(o_ref.dtype)
        psum = psum + p

    ovt = ovt_ref[...]
    p_hi = psum.astype(BF16)
    p_lo = (psum - p_hi.astype(F32)).astype(BF16)
    imp_t = _dot_nt(ovt, p_hi) + _dot_nt(ovt, p_lo)

    blk = lax.broadcasted_iota(I32, (SEL_ROWS, tq), 0)
    cur = lax.shift_right_logical(q0 + lax.broadcasted_iota(I32, (1, tq), 1), 6)
    in_range = (blk <= cur) & (blk < n_sel)
    forced = in_range & ((blk == 0) | (blk == cur) | (blk == cur - 1))
    k_free = (topk - 1 - (cur >= 1).astype(I32) - (cur >= 2).astype(I32)).astype(F32)
    neg = jnp.float32(-jnp.inf)
    x = jnp.where(in_range & jnp.logical_not(forced), imp_t, neg)
    n_grp = (n_sel + 7) // 8
    xg = [x[8 * r:8 * r + 8, :] for r in range(n_grp)]
    rank_g = [jnp.zeros((8, tq), F32) for _ in range(n_grp)]
    sub = lax.broadcasted_iota(I32, (8, tq), 0)
    for c in range(n_sel):
        vc = jnp.broadcast_to(x[c:c + 1, :], (8, tq))
        for r in range(n_grp):
            if 8 * r > c:
                hit = jnp.where(vc >= xg[r], 1.0, 0.0)
            elif 8 * r + 7 <= c:
                hit = jnp.where(vc > xg[r], 1.0, 0.0)
            else:
                hit = jnp.where(sub > c - 8 * r, jnp.where(vc >= xg[r], 1.0, 0.0), jnp.where(vc > xg[r], 1.0, 0.0))
            rank_g[r] = rank_g[r] + hit
    rank = jnp.concatenate(rank_g + [jnp.zeros((SEL_ROWS - 8 * n_grp, tq), F32)] * (SEL_ROWS > 8 * n_grp), axis=0)
    selected = forced | ((x > neg) & (rank < k_free))
    bias_t = jnp.where(selected, 0.0, MASK_VALUE)
    bias = jnp.concatenate([bias_t, jnp.zeros((LANES - SEL_ROWS, tq), F32)], axis=0).T
    lane = lax.broadcasted_iota(I32, (tq, LANES), 1)
    for h in range(HEADS_PER_GROUP):
        qa_ref[h, :, 0:HEAD_DIM] = q_ref[h]
        qa_ref[h, :, HEAD_DIM:2 * HEAD_DIM] = jnp.where(lane < SEL_ROWS, bias, qf_ref[0, h:h + 1, :]).astype(BF16)


def _cmp_attn(q_t, kvcmp, slopes, ovt, qfeat, B, tq=512):
    NH, T, dh = q_t.shape
    S = T // B
    nq = S // tq
    nhb = kvcmp.shape[2]
    n_sel = S // SEL_BLOCK
    assert n_sel <= SEL_ROWS
    topk = min(SEL_TOPK, n_sel)
    qmap = lambda bb, g, i: (g, bb * nq + i, 0)
    return pl.pallas_call(
        functools.partial(_cmp_attn_kernel, tq=tq, nhb=nhb, n_sel=n_sel, topk=topk),
        grid=(B, N_KV_GROUPS, nq),
        in_specs=[
            pl.BlockSpec((HEADS_PER_GROUP, tq, dh), qmap),
            pl.BlockSpec((1, 1, nhb, dh), lambda bb, g, i: (g, bb, 0, 0)),
            pl.BlockSpec((1, 1, nhb, dh), lambda bb, g, i: (N_KV_GROUPS + g, bb, 0, 0)),
            pl.BlockSpec((1, 8, LANES), lambda bb, g, i: (g, 0, 0)),
            pl.BlockSpec((SEL_ROWS, nhb), lambda bb, g, i: (0, 0)),
            pl.BlockSpec((1, 8, LANES), lambda bb, g, i: (g, 0, 0)),
        ],
        out_specs=[
            pl.BlockSpec((HEADS_PER_GROUP, tq, dh), qmap),
            pl.BlockSpec((HEADS_PER_GROUP, tq, 2 * dh), qmap),
        ],
        out_shape=[
            jax.ShapeDtypeStruct((NH, T, dh), BF16),
            jax.ShapeDtypeStruct((NH, T, 2 * dh), BF16),
        ],
        compiler_params=_cparams(("parallel", "parallel", "parallel")),
        name="cmp_attn",
    )(q_t, kvcmp, kvcmp, slopes, ovt, qfeat)


def _sel_attn_kernel(q_ref, k_ref, v_ref, o_ref, m_scr, acc_scr, s_buf, *, tq, tk):
    i = pl.program_id(2)
    q0 = i * tq
    nkt = k_ref.shape[1]
    m_scr[...] = jnp.full_like(m_scr, MASK_VALUE)
    acc_scr[...] = jnp.zeros_like(acc_scr)
    qpos = q0 + lax.broadcasted_iota(I32, (tq, 1), 0)

    def scores_to(t, slot):
        ka_t = k_ref[0, jnp.minimum(t, nkt - 1)]
        causal = (t * tk + lax.broadcasted_iota(I32, (1, tk), 1)) <= qpos
        for h in range(HEADS_PER_GROUP):
            s_buf[slot, h] = jnp.where(causal, _dot(q_ref[h], ka_t), MASK_VALUE)

    def absorb_from(t, slot):
        k0 = pl.multiple_of(jnp.minimum(t, nkt - 1) * tk, tk)
        va = v_ref[0, pl.ds(k0, tk), :]
        half = tq // 2
        for h in range(HEADS_PER_GROUP):
            m_prev = m_scr[h]
            m_new = jnp.maximum(m_prev, jnp.max(s_buf[slot, h], axis=-1, keepdims=True))
            alpha = jnp.exp(m_prev - m_new)
            p = jnp.concatenate([jnp.exp(s_buf[slot, h, 0:half, :] - m_new[0:half]).astype(BF16),
                                 jnp.exp(s_buf[slot, h, half:tq, :] - m_new[half:tq]).astype(BF16)], axis=0)
            acc_scr[h] = alpha * acc_scr[h] + _dot(p, va)
            m_scr[h] = m_new

    n_tiles = q0 // tk + 1

    def tile_pair(pr, carry):
        t = 2 * pr
        scores_to(t + 1, 1)
        absorb_from(t, 0)

        @pl.when(t + 1 < n_tiles)
        def _():
            scores_to(t + 2, 0)
            absorb_from(t + 1, 1)

        return carry

    scores_to(0, 0)
    lax.fori_loop(0, (n_tiles + 1) // 2, tile_pair, 0)
    for h in range(HEADS_PER_GROUP):
        o_ref[h] = (acc_scr[h, :, 0:HEAD_DIM] / acc_scr[h, :, HEAD_DIM:HEAD_DIM + 1]).astype(o_ref.dtype)


def _sel_attn(q_aug, k_aug_t, v_aug, B, tq=512, tk=SEL_KEY_TILE):
    NH, T, dq = q_aug.shape
    dh = dq // 2
    S = T // B
    assert tk % tq == 0 and S % tk == 0
    nq = S // tq
    nkt = S // tk
    qmap = lambda bb, g, i: (g, bb * nq + i, 0)
    return pl.pallas_call(
        functools.partial(_sel_attn_kernel, tq=tq, tk=tk),
        grid=(B, N_KV_GROUPS, nq),
        in_specs=[
            pl.BlockSpec((HEADS_PER_GROUP, tq, dq), qmap),
            pl.BlockSpec((1, nkt, dq, tk), lambda bb, g, i: (g, bb, 0, 0)),
            pl.BlockSpec((1, S, dq), lambda bb, g, i: (g, bb, 0)),
        ],
        out_specs=pl.BlockSpec((HEADS_PER_GROUP, tq, dh), qmap),
        out_shape=jax.ShapeDtypeStruct((NH, T, dh), BF16),
        scratch_shapes=[
            pltpu.VMEM((HEADS_PER_GROUP, tq, 1), F32),
            pltpu.VMEM((HEADS_PER_GROUP, tq, dq), F32),
            pltpu.VMEM((2, HEADS_PER_GROUP, tq, tk), F32),
        ],
        compiler_params=_cparams(("parallel", "parallel", "parallel")),
        name="nsa_sel_attn",
    )(q_aug, k_aug_t, v_aug)


def _win_attn_kernel(q_ref, k_ref, v_ref, o_ref, *, tq):
    i = pl.program_id(2)
    q0 = i * tq
    back = WINDOW // tq
    qpos = q0 + lax.broadcasted_iota(I32, (tq, 1), 0)
    tiles = []
    for d in range(back, -1, -1):
        kt = i - d
        ktc = jnp.maximum(kt, 0)
        kpos = kt * tq + lax.broadcasted_iota(I32, (1, tq), 1)
        if d == back:
            valid = ((qpos - kpos) < WINDOW) & (kt >= 0)
        elif d == 0:
            valid = kpos <= qpos
        else:
            valid = kt >= 0
        tiles.append((k_ref[0, ktc], v_ref[0, pl.ds(pl.multiple_of(ktc * tq, tq), tq), :], valid))
    for h in range(HEADS_PER_GROUP):
        q = q_ref[h]
        s_all = [jnp.where(valid, _dot(q, k_t), MASK_VALUE) for k_t, _, valid in tiles]
        m = functools.reduce(jnp.maximum, [jnp.max(s, axis=-1, keepdims=True) for s in s_all])
        acc = functools.reduce(lambda a, b: a + b,
                               [_dot(jnp.exp(s - m).astype(BF16), va) for s, (_, va, _) in zip(s_all, tiles)])
        o_ref[h] = (acc[:, 0:HEAD_DIM] / acc[:, HEAD_DIM:HEAD_DIM + 1]).astype(o_ref.dtype)


def _win_attn(q_aug, k_aug_t, v_aug, B, tq=256):
    NH, T, dq = q_aug.shape
    dh = dq // 2
    S = T // B
    assert WINDOW % tq == 0 and S % tq == 0
    nq = S // tq
    qmap = lambda bb, g, i: (g, bb * nq + i, 0)
    return pl.pallas_call(
        functools.partial(_win_attn_kernel, tq=tq),
        grid=(B, N_KV_GROUPS, nq),
        in_specs=[
            pl.BlockSpec((HEADS_PER_GROUP, tq, dq), qmap),
            pl.BlockSpec((1, nq, dq, tq), lambda bb, g, i: (g, bb, 0, 0)),
            pl.BlockSpec((1, S, dq), lambda bb, g, i: (g, bb, 0)),
        ],
        out_specs=pl.BlockSpec((HEADS_PER_GROUP, tq, dh), qmap),
        out_shape=jax.ShapeDtypeStruct((NH, T, dh), BF16),
        compiler_params=_cparams(("parallel", "parallel", "parallel")),
        name="nsa_win_attn",
    )(q_aug, k_aug_t, v_aug)


def _attn_out_kernel(oc_ref, os_ref, ow_ref, gt_ref, ex_ref, x_ref, w_ref, g_ref, b_ref, of_ref, op_ref, lhs):
    gt = gt_ref[...]
    gt_hi = gt.astype(BF16)
    gt_lo = (gt - gt_hi.astype(F32)).astype(BF16)
    gt2 = jnp.concatenate([gt_hi, gt_lo], axis=-1)
    ge = [_dot(gt2, ex_ref[br]) for br in range(N_BRANCHES)]
    for hd in range(N_HEADS):
        cols = slice(hd * HEAD_DIM, (hd + 1) * HEAD_DIM)
        o = (ge[0][:, cols] * oc_ref[hd].astype(F32) + ge[1][:, cols] * os_ref[hd].astype(F32)
             + ge[2][:, cols] * ow_ref[hd].astype(F32))
        lhs[:, cols] = o.astype(BF16)
    mix = _dot(lhs[...], w_ref[...])
    h = _layer_norm(DEEPNORM_ALPHA * x_ref[...] + mix, g_ref[...], b_ref[...])
    of_ref[...] = h
    op_ref[...] = _pack_halves(h)


def _attn_out_res_ln(o_cmp, o_sel, o_win, gates, x, w, g, b, tm=256):
    T, D = x.shape
    NH = o_cmp.shape[0]
    hmap = lambda i: (0, i, 0)
    row = lambda i: (i, 0)
    fix = lambda i: (0, 0)
    expand = np.zeros((N_BRANCHES, 2 * LANES, NH * HEAD_DIM), np.float32)
    for hd in range(NH):
        for br in range(N_BRANCHES):
            for piece in range(2):
                expand[br, piece * LANES + N_BRANCHES * hd + br, hd * HEAD_DIM:(hd + 1) * HEAD_DIM] = 1.0
    expand = jnp.asarray(expand, dtype=BF16)
    return pl.pallas_call(
        _attn_out_kernel,
        grid=(T // tm,),
        in_specs=[
            pl.BlockSpec((NH, tm, HEAD_DIM), hmap), pl.BlockSpec((NH, tm, HEAD_DIM), hmap),
            pl.BlockSpec((NH, tm, HEAD_DIM), hmap),
            pl.BlockSpec((tm, LANES), row),
            pl.BlockSpec((N_BRANCHES, 2 * LANES, NH * HEAD_DIM), lambda i: (0, 0, 0)),
            pl.BlockSpec((tm, D), row),
            pl.BlockSpec((D, D), fix),
            pl.BlockSpec((1, D), fix), pl.BlockSpec((1, D), fix),
        ],
        out_specs=[pl.BlockSpec((tm, D), row), pl.BlockSpec((tm, D // 2), row)],
        out_shape=[jax.ShapeDtypeStruct((T, D), F32), jax.ShapeDtypeStruct((T, D // 2), U32)],
        scratch_shapes=[pltpu.VMEM((tm, D), BF16)],
        compiler_params=_cparams(("parallel",), VMEM_LIMIT),
        name="attn_out_res_ln",
    )(o_cmp, o_sel, o_win, gates, expand, x, w, g, b)


def _bf16_pieces(x, n=3):
    out, rest = [], np.asarray(x, np.float32)
    for _ in range(n):
        piece = rest.astype(BF16).astype(np.float32)
        out.append(piece)
        rest = (rest - piece).astype(np.float32)
    return out


def _nsa_tables(S):
    nhb = S // CMP_STRIDE
    n_sel = S // SEL_BLOCK
    heads = np.arange(1, N_HEADS + 1, dtype=np.float32)
    slope = (2.0 ** (-8.0 * heads / N_HEADS)).astype(np.float32)
    slopes = np.zeros((N_KV_GROUPS, 8, LANES), np.float32)
    slopes[:, :HEADS_PER_GROUP, :] = slope.reshape(N_KV_GROUPS, HEADS_PER_GROUP, 1)
    c0 = np.arange(nhb)[None, :] * CMP_STRIDE
    j0 = np.arange(SEL_ROWS)[:, None] * SEL_BLOCK
    ovt = np.maximum(np.minimum(c0 + CMP_BLOCK, j0 + SEL_BLOCK) - np.maximum(c0, j0), 0).astype(np.float32) / CMP_BLOCK
    ovt[:, nhb - 1] = 0.0
    ovt[n_sel:, :] = 0.0
    qfeat = np.zeros((N_KV_GROUPS, 8, LANES), np.float32)
    pieces = _bf16_pieces(slope)
    for t, piece in enumerate(pieces):
        qfeat[:, :HEADS_PER_GROUP, SEL_ROWS + t] = piece.reshape(N_KV_GROUPS, HEADS_PER_GROUP)
        qfeat[:, :HEADS_PER_GROUP, SEL_ROWS + 3 + t] = piece.reshape(N_KV_GROUPS, HEADS_PER_GROUP)
    kpos = np.arange(S)
    kx = np.zeros((S, LANES), np.float32)
    kx[kpos, kpos // SEL_BLOCK] = 1.0
    kx[:, SEL_ROWS:SEL_ROWS + 3] = (SEL_BLOCK * (kpos // SEL_BLOCK))[:, None]
    kx[:, SEL_ROWS + 3:SEL_ROWS + 6] = (kpos % SEL_BLOCK)[:, None]
    return jnp.asarray(slopes), jnp.asarray(ovt, dtype=BF16), jnp.asarray(qfeat), jnp.asarray(kx, dtype=BF16)


def _nsa_layer_res_ln(hf, hb, B, kv_w, cmp_pos, cmp_w1, cmp_w2, w_qg, w_o, g, b):
    T, D = hf.shape
    S = T // B
    hd = N_HEADS * HEAD_DIM
    gsz = N_KV_GROUPS * HEAD_DIM
    slopes, ovt, qfeat, kx = _nsa_tables(S)

    kvw = kv_w.astype(BF16)
    kvc = _proj_heads(hb, kvw[:, :2 * gsz], F32)
    kv_t = _proj_heads(hb, kvw[:, 2 * gsz:], BF16)
    wq = w_qg[:, :hd].astype(BF16)
    wgt = jnp.zeros((D, LANES), F32).at[:, :N_BRANCHES * N_HEADS].set(w_qg[:, hd:]).astype(BF16)
    q_t = _proj_heads(hb, wq, BF16, scale=HEAD_DIM ** -0.5)
    gates = _gates(hb, wgt)

    pos = cmp_pos.reshape(2, 2, CMP_STRIDE * HEAD_DIM)
    kvcmp = _cmp_mlp(kvc, pos, cmp_w1.astype(BF16), cmp_w2.astype(BF16), B)
    o_cmp, q_aug = _cmp_attn(q_t, kvcmp, slopes, ovt, qfeat, B)
    kx_all = jnp.broadcast_to(jnp.tile(kx, (B, 1))[None], (N_KV_GROUPS, T, LANES))
    k_aug = jnp.concatenate([kv_t[:N_KV_GROUPS], kx_all], axis=-1)
    k_aug_t = k_aug.reshape(N_KV_GROUPS, T // SEL_KEY_TILE, SEL_KEY_TILE, 2 * HEAD_DIM).transpose(0, 1, 3, 2)
    ones_col = jnp.zeros((1, 1, LANES), BF16).at[0, 0, 0].set(1.0)
    v_aug = jnp.concatenate([kv_t[N_KV_GROUPS:2 * N_KV_GROUPS],
                             jnp.broadcast_to(ones_col, (N_KV_GROUPS, T, LANES))], axis=-1)
    o_sel = _sel_attn(q_aug, k_aug_t, v_aug, B)
    win_tq = 256
    lane_id = jnp.arange(LANES)[None, None, :]
    kw_aug = jnp.concatenate([kv_t[2 * N_KV_GROUPS:3 * N_KV_GROUPS],
                              jnp.where(lane_id < SEL_ROWS, jnp.zeros((), BF16), kx_all)], axis=-1)
    kw_aug_t = kw_aug.reshape(N_KV_GROUPS, T // win_tq, win_tq, 2 * HEAD_DIM).transpose(0, 1, 3, 2)
    vw_aug = jnp.concatenate([kv_t[3 * N_KV_GROUPS:],
                              jnp.broadcast_to(ones_col, (N_KV_GROUPS, T, LANES))], axis=-1)
    o_win = _win_attn(q_aug, kw_aug_t, vw_aug, B, tq=win_tq)
    return _attn_out_res_ln(o_cmp, o_sel, o_win, gates, hf, w_o.astype(BF16), g, b)


def kernel(x, conv_w_pw1, conv_b_pw1, conv_w_dw, conv_b_dw, conv_ln_g, conv_ln_b, conv_w_pw2, kv_w, cmp_pos, cmp_w1, cmp_w2, nsa_w_qg, nsa_w_o, moe_wg, moe_bg, moe_we, moe_be, moe_w_gu, moe_w_down, ln_g, ln_b):
    B, S, D = x.shape
    T = B * S
    hf = x.reshape(T, D)
    hb = hf.astype(BF16)
    r = lambda v: v.reshape(1, -1)
    for l in range(DEPTH):
        if l < N_A_LAYERS:
            glu = _pw1_glu(hb, conv_w_pw1[l].astype(BF16), r(conv_b_pw1[l]))
            conv = _dwconv(glu.reshape(B, S, D), conv_w_dw[l], r(conv_b_dw[l])).reshape(T, D)
            hf, hp = _pw2_res_ln(conv, hf, conv_w_pw2[l].astype(BF16), r(conv_ln_g[l]), r(conv_ln_b[l]),
                                 r(ln_g[l, 0]), r(ln_b[l, 0]))
        else:
            j = l - N_A_LAYERS
            hf, hp = _nsa_layer_res_ln(hf, hb, B, kv_w, cmp_pos, cmp_w1, cmp_w2, nsa_w_qg[j], nsa_w_o[j],
                                       r(ln_g[l, 0]), r(ln_b[l, 0]))
        hf, hb = _hier_moe_res_ln(hf, hp, moe_wg[l], moe_bg[l], moe_we[l], moe_be[l], moe_w_gu, moe_w_down, l,
                                  r(ln_g[l, 1]), r(ln_b[l, 1]))
    return hf.reshape(B, S, D)
```

```python
import functools

import numpy as np
import jax
import jax.numpy as jnp
from jax import lax
from jax.experimental import pallas as pl
from jax.experimental.pallas import tpu as pltpu

F32 = jnp.float32
BF16 = jnp.bfloat16
I32 = jnp.int32
U32 = jnp.uint32

DEPTH = 2
N_A_LAYERS = DEPTH // 2
DEEPNORM_ALPHA = (2.0 * DEPTH) ** 0.25
LN_EPS = 1e-5
CONV_WIDTH = 31
N_HEADS = 16
HEAD_DIM = 128
N_KV_GROUPS = 4
HEADS_PER_GROUP = N_HEADS // N_KV_GROUPS
CMP_BLOCK = 32
CMP_STRIDE = 16
SEL_BLOCK = 64
SEL_TOPK = 16
WINDOW = 512
N_BRANCHES = 3
MASK_VALUE = -1e30
N_GROUPS = 4
EXPERTS_PER_GROUP = 8
N_EXPERTS = N_GROUPS * EXPERTS_PER_GROUP
MOE_CHUNK = 128

LANES = 128
SUBLANES = 8
VMEM_LIMIT = 56 * 1024 * 1024

CONV_HALO = 32
SEL_ROWS = 64
SEL_KEY_TILE = 512
W_DMA_SPLIT = 8


def _cparams(sem, vmem=None):
    return pltpu.CompilerParams(dimension_semantics=sem, vmem_limit_bytes=vmem)


def _sigmoid(x):
    return 1.0 / (1.0 + jnp.exp(-x))


def _layer_norm(x, g, b):
    mu = jnp.mean(x, axis=-1, keepdims=True)
    xc = x - mu
    var = jnp.mean(xc * xc, axis=-1, keepdims=True)
    return xc * lax.rsqrt(var + LN_EPS) * g + b


def _pack_halves(x):
    n = x.shape[-1] // 2
    lo = lax.bitcast_convert_type(x[:, :n].astype(BF16).astype(F32), U32)
    hi = lax.bitcast_convert_type(x[:, n:].astype(BF16).astype(F32), U32)
    return lax.shift_right_logical(lo, jnp.uint32(16)) | hi


def _unpack_halves(p):
    lo = lax.bitcast_convert_type(lax.shift_left(p, jnp.uint32(16)), F32)
    hi = lax.bitcast_convert_type(p & jnp.uint32(0xFFFF0000), F32)
    return lo, hi


def _dot(a, b):
    return jnp.dot(a, b, preferred_element_type=F32)


def _dot_nt(a, b):
    return lax.dot_general(a, b, (((1,), (1,)), ((), ())), preferred_element_type=F32)


def _pw1_glu_kernel(x_ref, wa_ref, wg_ref, ba_ref, bg_ref, o_ref):
    x = x_ref[...]
    a = _dot(x, wa_ref[...]) + ba_ref[...]
    g = _dot(x, wg_ref[...]) + bg_ref[...]
    o_ref[...] = a * _sigmoid(g)


def _pw1_glu(xb, w, b, tm=1024, tn=512):
    T, D = xb.shape
    nj = D // tn
    return pl.pallas_call(
        _pw1_glu_kernel,
        grid=(nj, T // tm),
        in_specs=[
            pl.BlockSpec((tm, D), lambda j, i: (i, 0)),
            pl.BlockSpec((D, tn), lambda j, i: (0, j)),
            pl.BlockSpec((D, tn), lambda j, i: (0, j + nj)),
            pl.BlockSpec((1, tn), lambda j, i: (0, j)),
            pl.BlockSpec((1, tn), lambda j, i: (0, j + nj)),
        ],
        out_specs=pl.BlockSpec((tm, tn), lambda j, i: (i, j)),
        out_shape=jax.ShapeDtypeStruct((T, D), F32),
        compiler_params=_cparams(("parallel", "parallel"), VMEM_LIMIT),
        name="pw1_glu",
    )(xb, w, w, b, b)


def _dwconv_kernel(x_ref, halo_ref, w_ref, b_ref, o_ref, xsh, *, ts, tc, rc):
    i = pl.program_id(1)
    xsh[0, 0:CONV_HALO, :] = jnp.where(i > 0, halo_ref[0], 0.0)
    xsh[0, CONV_HALO:CONV_HALO + ts, :] = x_ref[0]
    span = ts + CONV_HALO - SUBLANES
    for p in range(1, SUBLANES):
        xsh[p, 0:span, :] = xsh[0, p:p + span, :]
    off = CONV_HALO - (CONV_WIDTH - 1)
    for c0 in range(0, tc, LANES):
        for r0 in range(0, ts, rc):
            acc = jnp.broadcast_to(b_ref[:, c0:c0 + LANES], (rc, LANES))
            for k in range(CONV_WIDTH):
                p = (k + off) % SUBLANES
                base = r0 + k + off - p
                acc = acc + w_ref[k:k + 1, c0:c0 + LANES] * xsh[p, base:base + rc, c0:c0 + LANES]
            o_ref[0, r0:r0 + rc, c0:c0 + LANES] = acc


def _dwconv(x, w, b, ts=512, tc=512, rc=128):
    B, S, D = x.shape
    hb = ts // CONV_HALO
    return pl.pallas_call(
        functools.partial(_dwconv_kernel, ts=ts, tc=tc, rc=rc),
        grid=(B, S // ts, D // tc),
        in_specs=[
            pl.BlockSpec((1, ts, tc), lambda bb, i, c: (bb, i, c)),
            pl.BlockSpec((1, CONV_HALO, tc), lambda bb, i, c: (bb, jnp.maximum(i * hb - 1, 0), c)),
            pl.BlockSpec((CONV_WIDTH, tc), lambda bb, i, c: (0, c)),
            pl.BlockSpec((1, tc), lambda bb, i, c: (0, c)),
        ],
        out_specs=pl.BlockSpec((1, ts, tc), lambda bb, i, c: (bb, i, c)),
        out_shape=jax.ShapeDtypeStruct((B, S, D), F32),
        scratch_shapes=[pltpu.VMEM((SUBLANES, CONV_HALO + ts, tc), F32)],
        compiler_params=_cparams(("parallel", "parallel", "parallel")),
        name="dwconv",
    )(x, x, w, b)


def _pw2_kernel(c_ref, x_ref, w_ref, cg_ref, cb_ref, g_ref, b_ref, of_ref, op_ref):
    y = _layer_norm(c_ref[...], cg_ref[...], cb_ref[...])
    y = y * _sigmoid(y)
    mix = _dot(y.astype(BF16), w_ref[...])
    h = _layer_norm(DEEPNORM_ALPHA * x_ref[...] + mix, g_ref[...], b_ref[...])
    of_ref[...] = h
    op_ref[...] = _pack_halves(h)


def _pw2_res_ln(conv, x, w, cg, cb, g, b, tm=256):
    T, D = x.shape
    row = lambda i: (i, 0)
    fix = lambda i: (0, 0)
    return pl.pallas_call(
        _pw2_kernel,
        grid=(T // tm,),
        in_specs=[
            pl.BlockSpec((tm, D), row), pl.BlockSpec((tm, D), row),
            pl.BlockSpec((D, D), fix),
            pl.BlockSpec((1, D), fix), pl.BlockSpec((1, D), fix),
            pl.BlockSpec((1, D), fix), pl.BlockSpec((1, D), fix),
        ],
        out_specs=[pl.BlockSpec((tm, D), row), pl.BlockSpec((tm, D // 2), row)],
        out_shape=[jax.ShapeDtypeStruct((T, D), F32), jax.ShapeDtypeStruct((T, D // 2), U32)],
        compiler_params=_cparams(("parallel",), VMEM_LIMIT),
        name="pw2_res_ln",
    )(conv, x, w, cg, cb, g, b)


def _router_kernel(h_ref, w_ref, b_ref, ri_ref, rw_ref, cnt_ref, *, tm):
    s = pl.program_id(0)

    @pl.when(s == 0)
    def _():
        cnt_ref[...] = jnp.zeros_like(cnt_ref)

    h = h_ref[...]
    w = w_ref[...]
    h_hi = h.astype(BF16)
    h_lo = (h - h_hi.astype(F32)).astype(BF16)
    w_hi = w.astype(BF16)
    w_lo = (w - w_hi.astype(F32)).astype(BF16)
    lg = _dot(h_hi, w_hi) + _dot(h_lo, w_hi) + _dot(h_hi, w_lo) + b_ref[...]

    lane = lax.broadcasted_iota(I32, (tm, LANES), 1)
    lanef = lane.astype(F32)
    neg = jnp.float32(-jnp.inf)
    big = jnp.float32(1e9)

    is_g = lane < N_GROUPS
    lgm = jnp.where(is_g, lg, neg)
    mg = jnp.max(lgm, axis=-1, keepdims=True)
    grp = jnp.min(jnp.where(lgm == mg, lanef, big), axis=-1, keepdims=True)
    sg = jnp.sum(jnp.where(is_g, jnp.exp(lg - mg), 0.0), axis=-1, keepdims=True)
    gw = 1.0 / sg

    lo = N_GROUPS + grp * EXPERTS_PER_GROUP
    in_g = (lanef >= lo) & (lanef < lo + EXPERTS_PER_GROUP)
    le = jnp.where(in_g, lg, neg)
    m1 = jnp.max(le, axis=-1, keepdims=True)
    i1 = jnp.min(jnp.where(le == m1, lanef, big), axis=-1, keepdims=True)
    le2 = jnp.where(lanef == i1, neg, le)
    m2 = jnp.max(le2, axis=-1, keepdims=True)
    i2 = jnp.min(jnp.where(le2 == m2, lanef, big), axis=-1, keepdims=True)
    t = jnp.exp(m2 - m1)
    p1 = 1.0 / (1.0 + t)
    w1 = p1 * gw
    w2 = (t * p1) * gw
    e1 = i1 - N_GROUPS
    e2 = i2 - N_GROUPS

    oh1 = (lanef == e1)
    oh2 = (lanef == e2)
    oh = (oh1 | oh2).astype(F32)
    r_i = lax.broadcasted_iota(I32, (tm, tm), 0)
    c_i = lax.broadcasted_iota(I32, (tm, tm), 1)
    tri = (c_i < r_i).astype(BF16)
    cs = _dot(tri, oh.astype(BF16)) + cnt_ref[...].astype(F32)
    rank1 = jnp.sum(jnp.where(oh1, cs, 0.0), axis=-1, keepdims=True)
    rank2 = jnp.sum(jnp.where(oh2, cs, 0.0), axis=-1, keepdims=True)
    cnt_ref[...] = cnt_ref[...] + jnp.sum(oh, axis=0, keepdims=True).astype(I32)

    ri = jnp.where(lane == 0, e1, jnp.where(lane == 1, e2, jnp.where(lane == 2, rank1, jnp.where(lane == 3, rank2, 0.0))))
    ri_ref[...] = ri.T[0:SUBLANES, :].astype(I32)
    rw_ref[...] = jnp.where(lane == 0, w1, jnp.where(lane == 1, w2, 0.0))


def _router(h, wcat, bcat, tm=256):
    T, D = h.shape
    return pl.pallas_call(
        functools.partial(_router_kernel, tm=tm),
        grid=(T // tm,),
        in_specs=[
            pl.BlockSpec((tm, D), lambda i: (i, 0)),
            pl.BlockSpec((D, LANES), lambda i: (0, 0)),
            pl.BlockSpec((1, LANES), lambda i: (0, 0)),
        ],
        out_specs=[
            pl.BlockSpec((SUBLANES, tm), lambda i: (0, i)),
            pl.BlockSpec((tm, LANES), lambda i: (i, 0)),
            pl.BlockSpec((1, LANES), lambda i: (0, 0)),
        ],
        out_shape=[
            jax.ShapeDtypeStruct((SUBLANES, T), I32),
            jax.ShapeDtypeStruct((T, LANES), F32),
            jax.ShapeDtypeStruct((1, LANES), I32),
        ],
        compiler_params=_cparams(("arbitrary",)),
        name="moe_router",
    )(h, wcat, bcat)


def _dispatch_kernel(dest_ref, zs_ref, zc_ref, nu_ref, x_ref, xs_hbm, zblk, sem, *, tm, n_tok, n_chunks):
    s = pl.program_id(0)

    def row_copy(r, d):
        return pltpu.make_async_copy(x_ref.at[pl.ds(r, 1)], xs_hbm.at[pl.ds(d, 1)], sem)

    def zero_row_copy(d):
        return pltpu.make_async_copy(zblk.at[pl.ds(0, 1)], xs_hbm.at[pl.ds(d, 1)], sem)

    def zero_chunk_copy(c):
        return pltpu.make_async_copy(zblk, xs_hbm.at[pl.ds(pl.multiple_of(c * MOE_CHUNK, MOE_CHUNK), MOE_CHUNK)], sem)

    @pl.when(s == 0)
    def _():
        zblk[...] = jnp.zeros_like(zblk)

        def fill(start):
            def per_expert(e, c):
                def per_row(r, c2):
                    cp = zero_row_copy(zs_ref[e] + r)
                    if start:
                        cp.start()
                    else:
                        cp.wait()
                    return c2
                lax.fori_loop(0, zc_ref[e], per_row, 0)
                return c
            lax.fori_loop(0, N_EXPERTS, per_expert, 0)

            def per_chunk(c, c2):
                cp = zero_chunk_copy(c)
                if start:
                    cp.start()
                else:
                    cp.wait()
                return c2
            lax.fori_loop(nu_ref[0], n_chunks, per_chunk, 0)

        fill(True)
        fill(False)

    base = s * tm

    def issue(r, c):
        t = base + r
        row_copy(r, dest_ref[t]).start(priority=0)
        row_copy(r, dest_ref[n_tok + t]).start(priority=1)
        return c

    lax.fori_loop(0, tm, issue, 0, unroll=8)

    def drain(r, c):
        row_copy(0, 0).wait()
        row_copy(0, 0).wait()
        return c

    lax.fori_loop(0, tm, drain, 0, unroll=8)


def _dispatch(x, dest_flat, zstart, zcount, n_used, n_rows, tm=256):
    T, W = x.shape
    return pl.pallas_call(
        functools.partial(_dispatch_kernel, tm=tm, n_tok=T, n_chunks=n_rows // MOE_CHUNK),
        grid_spec=pltpu.PrefetchScalarGridSpec(
            num_scalar_prefetch=4,
            grid=(T // tm,),
            in_specs=[pl.BlockSpec((tm, W), lambda i, *_: (i, 0))],
            out_specs=pl.BlockSpec(memory_space=pl.ANY),
            scratch_shapes=[pltpu.VMEM((MOE_CHUNK, W), x.dtype), pltpu.SemaphoreType.DMA(())],
        ),
        out_shape=jax.ShapeDtypeStruct((n_rows, W), x.dtype),
        compiler_params=_cparams(("arbitrary",)),
        name="moe_dispatch",
    )(dest_flat, zstart, zcount, n_used, x)


def _moe_mlp_kernel(ce_ref, nu_ref, first_ref, nxt_ref, x_ref, wgu_hbm, wd_hbm, o_ref,
                    wgu_stage, wd_stage, wgu_bf, wd_bf, sem, *, F, layer):
    c = pl.program_id(0)

    def w_copies(e, slot):
        copies = []
        for ref, stage, s_idx in ((wgu_hbm, wgu_stage, 0), (wd_hbm, wd_stage, 1)):
            rows = stage.shape[1] // W_DMA_SPLIT
            for part in range(W_DMA_SPLIT):
                rs = pl.ds(part * rows, rows)
                copies.append(pltpu.make_async_copy(ref.at[layer, e, rs], stage.at[slot, rs], sem.at[s_idx, slot]))
        return copies

    def start_all(e, slot):
        for cp in w_copies(e, slot):
            cp.start(priority=1)

    @pl.when(c < nu_ref[0])
    def _():
        @pl.when(first_ref[c] > 0)
        def _():
            slot = first_ref[c] - 1

            @pl.when(c == 0)
            def _():
                start_all(ce_ref[0], slot)

            @pl.when(nxt_ref[c] >= 0)
            def _():
                start_all(nxt_ref[c], 1 - slot)

            for cp in w_copies(ce_ref[c], slot):
                cp.wait()
            wgu_bf[...] = wgu_stage[slot].astype(BF16)
            wd_bf[...] = wd_stage[slot].astype(BF16)

        x_lo, x_hi = _unpack_halves(x_ref[...])
        half = x_lo.shape[-1]
        gu = _dot(x_lo.astype(BF16), wgu_bf[0:half, :]) + _dot(x_hi.astype(BF16), wgu_bf[half:2 * half, :])
        g = gu[:, :F]
        u = gu[:, F:]
        hmid = (g * _sigmoid(g)) * u
        o_ref[...] = _pack_halves(_dot(hmid.astype(BF16), wd_bf[...]))

    @pl.when(c >= nu_ref[0])
    def _():
        o_ref[...] = jnp.zeros_like(o_ref)


def _moe_mlp(xs, chunk_e, n_used, first, nxt, w_gu, w_down, layer):
    n_rows, W = xs.shape
    D = 2 * W
    F2 = w_gu.shape[-1]
    F = F2 // 2
    n_chunks = n_rows // MOE_CHUNK

    def x_map(c, ce, nu, fi, nx):
        return (jnp.minimum(c, nu[0] - 1), 0)

    return pl.pallas_call(
        functools.partial(_moe_mlp_kernel, F=F, layer=layer),
        grid_spec=pltpu.PrefetchScalarGridSpec(
            num_scalar_prefetch=4,
            grid=(n_chunks,),
            in_specs=[
                pl.BlockSpec((MOE_CHUNK, W), x_map),
                pl.BlockSpec(memory_space=pl.ANY),
                pl.BlockSpec(memory_space=pl.ANY),
            ],
            out_specs=pl.BlockSpec((MOE_CHUNK, W), lambda c, ce, nu, fi, nx: (c, 0)),
            scratch_shapes=[
                pltpu.VMEM((2, D, F2), F32), pltpu.VMEM((2, F, D), F32),
                pltpu.VMEM((D, F2), BF16), pltpu.VMEM((F, D), BF16),
                pltpu.SemaphoreType.DMA((2, 2)),
            ],
        ),
        out_shape=jax.ShapeDtypeStruct((n_rows, W), U32),
        compiler_params=_cparams(("arbitrary",), VMEM_LIMIT),
        name="moe_mlp",
    )(chunk_e, n_used, first, nxt, xs, w_gu, w_down)


def _combine_kernel(dest_ref, y_hbm, rw_ref, h_ref, g_ref, b_ref, of_ref, ob_ref, gbuf, sem, *, tm, nsteps):
    s = pl.program_id(0)

    def row_copy(src_row, slot, k, r):
        return pltpu.make_async_copy(y_hbm.at[pl.ds(src_row, 1)], gbuf.at[slot, k, pl.ds(r, 1)], sem.at[slot])

    def issue(step, slot):
        base = step * tm

        def body(r, c):
            t = base + r
            row_copy(dest_ref[t], slot, 0, r).start(priority=0)
            row_copy(dest_ref[tm * nsteps + t], slot, 1, r).start(priority=1)
            return c

        lax.fori_loop(0, tm, body, 0, unroll=8)

    @pl.when(s == 0)
    def _():
        issue(0, 0)

    @pl.when(s + 1 < nsteps)
    def _():
        issue(s + 1, (s + 1) % 2)

    slot = s % 2

    def drain(r, c):
        row_copy(0, slot, 0, 0).wait()
        row_copy(0, slot, 1, 0).wait()
        return c

    lax.fori_loop(0, tm, drain, 0, unroll=8)

    w0 = rw_ref[:, 0:1]
    w1 = rw_ref[:, 1:2]
    lo0, hi0 = _unpack_halves(gbuf[slot, 0])
    lo1, hi1 = _unpack_halves(gbuf[slot, 1])
    y = jnp.concatenate([w0 * lo0 + w1 * lo1, w0 * hi0 + w1 * hi1], axis=-1)
    h = _layer_norm(DEEPNORM_ALPHA * h_ref[...] + y, g_ref[...], b_ref[...])
    of_ref[...] = h
    ob_ref[...] = h.astype(BF16)


def _combine_res_ln(y_rows, dest_flat, rw, h, g, b, tm=256):
    T, D = h.shape
    nsteps = T // tm
    row = lambda i, d: (i, 0)
    fix = lambda i, d: (0, 0)
    return pl.pallas_call(
        functools.partial(_combine_kernel, tm=tm, nsteps=nsteps),
        grid_spec=pltpu.PrefetchScalarGridSpec(
            num_scalar_prefetch=1,
            grid=(nsteps,),
            in_specs=[
                pl.BlockSpec(memory_space=pl.ANY),
                pl.BlockSpec((tm, LANES), row),
                pl.BlockSpec((tm, D), row),
                pl.BlockSpec((1, D), fix), pl.BlockSpec((1, D), fix),
            ],
            out_specs=[pl.BlockSpec((tm, D), row), pl.BlockSpec((tm, D), row)],
            scratch_shapes=[pltpu.VMEM((2, 2, tm, D // 2), U32), pltpu.SemaphoreType.DMA((2,))],
        ),
        out_shape=[jax.ShapeDtypeStruct((T, D), F32), jax.ShapeDtypeStruct((T, D), BF16)],
        compiler_params=_cparams(("arbitrary",), VMEM_LIMIT),
        name="moe_combine",
    )(dest_flat, y_rows, rw, h, g, b)


def _hier_moe_res_ln(h, hp, wg, bg, we, be, w_gu, w_down, layer, g, b):
    T, D = h.shape
    wcat = jnp.zeros((D, LANES), F32).at[:, :N_GROUPS].set(wg).at[:, N_GROUPS:N_GROUPS + N_EXPERTS].set(we)
    bcat = jnp.zeros((1, LANES), F32).at[0, :N_GROUPS].set(bg).at[0, N_GROUPS:N_GROUPS + N_EXPERTS].set(be)
    ri, rw, cnt = _router(h, wcat, bcat)

    counts = cnt[0, :N_EXPERTS]
    padded = (counts + MOE_CHUNK - 1) // MOE_CHUNK * MOE_CHUNK
    pad_end = jnp.cumsum(padded)
    pad_start = pad_end - padded
    A = T * 2
    n_chunks = (A + N_EXPERTS * (MOE_CHUNK - 1) + MOE_CHUNK - 1) // MOE_CHUNK
    n_rows = n_chunks * MOE_CHUNK
    seg = jnp.zeros((2, T), I32)
    for e in range(N_EXPERTS):
        seg = jnp.where(ri[0:2] == e, pad_start[e], seg)
    dest_flat = (seg + ri[2:4]).reshape(-1).astype(I32)
    n_used = (pad_end[-1] // MOE_CHUNK).astype(I32)
    cidx = jnp.minimum(jnp.arange(n_chunks, dtype=I32), n_used - 1)
    chunk_e = jnp.sum((pad_end[None, :] <= (cidx * MOE_CHUNK)[:, None]).astype(I32), axis=1)
    chunk_e = jnp.minimum(chunk_e, N_EXPERTS - 1)
    zstart = (pad_start + counts).astype(I32)
    zcount = (padded - counts).astype(I32)
    carange = jnp.arange(n_chunks, dtype=I32)
    prev_e = jnp.concatenate([jnp.full((1,), -1, I32), chunk_e[:-1]])
    first = ((chunk_e != prev_e) & (carange < n_used)).astype(I32)
    first = first * (1 + (jnp.cumsum(first) - 1) % 2)
    run_end = (pad_end[chunk_e] // MOE_CHUNK).astype(I32)
    nxt = jnp.where(run_end < n_used, chunk_e[jnp.minimum(run_end, n_chunks - 1)], -1).astype(I32)

    n_used = n_used.reshape(1)
    xs = _dispatch(hp, dest_flat, zstart, zcount, n_used, n_rows)
    ys = _moe_mlp(xs, chunk_e, n_used, first, nxt, w_gu, w_down, layer)
    return _combine_res_ln(ys, dest_flat, rw, h, g, b)


def _proj_heads_kernel(x_ref, w_ref, o_ref, *, nh, scale):
    acc = _dot(x_ref[...], w_ref[...])
    if scale != 1.0:
        acc = acc * scale
    for c in range(nh):
        o_ref[c] = acc[:, c * HEAD_DIM:(c + 1) * HEAD_DIM].astype(o_ref.dtype)


def _proj_heads(xb, w, out_dtype, scale=1.0, tm=1024, tn=512):
    T, D = xb.shape
    N = w.shape[1]
    nh = tn // HEAD_DIM
    return pl.pallas_call(
        functools.partial(_proj_heads_kernel, nh=nh, scale=scale),
        grid=(N // tn, T // tm),
        in_specs=[
            pl.BlockSpec((tm, D), lambda j, i: (i, 0)),
            pl.BlockSpec((D, tn), lambda j, i: (0, j)),
        ],
        out_specs=pl.BlockSpec((nh, tm, HEAD_DIM), lambda j, i: (j, i, 0)),
        out_shape=jax.ShapeDtypeStruct((N // HEAD_DIM, T, HEAD_DIM), out_dtype),
        compiler_params=_cparams(("parallel", "parallel"), VMEM_LIMIT),
        name="proj_heads",
    )(xb, w)


def _gates_kernel(x_ref, w_ref, o_ref):
    o_ref[...] = _sigmoid(_dot(x_ref[...], w_ref[...]))


def _gates(xb, w, tm=512):
    T, D = xb.shape
    return pl.pallas_call(
        _gates_kernel,
        grid=(T // tm,),
        in_specs=[pl.BlockSpec((tm, D), lambda i: (i, 0)), pl.BlockSpec((D, LANES), lambda i: (0, 0))],
        out_specs=pl.BlockSpec((tm, LANES), lambda i: (i, 0)),
        out_shape=jax.ShapeDtypeStruct((T, LANES), F32),
        compiler_params=_cparams(("parallel",)),
        name="nsa_gates",
    )(xb, w)


def _cmp_mlp_kernel(x_ref, pos_ref, w1_ref, w2_ref, o_ref, *, nhb):
    x = x_ref[0, 0]
    half = x.shape[1]
    lo = (x + pos_ref[0, 0:1, :]).astype(BF16)
    hi = (x + pos_ref[0, 1:2, :]).astype(BF16)
    a_lo = _dot(lo, w1_ref[0, 0:half, :])
    a_hi = _dot(hi, w1_ref[0, half:2 * half, :])
    pre = a_lo + pltpu.roll(a_hi, nhb - 1, 0)
    hid = 0.5 * pre * (1.0 + jnp.tanh(0.7978845608028654 * (pre + 0.044715 * pre * pre * pre)))
    comp = _dot(hid.astype(BF16), w2_ref[0])
    row = lax.broadcasted_iota(I32, comp.shape, 0)
    o_ref[0, 0] = jnp.where(row < nhb - 1, comp, 0.0).astype(o_ref.dtype)


def _cmp_mlp(kvc, pos, w1, w2, B):
    R, T, dh = kvc.shape
    S = T // B
    nhb = S // CMP_STRIDE
    half = CMP_STRIDE * dh
    x4 = kvc.reshape(R, B, nhb, half)
    H = w1.shape[-1]
    return pl.pallas_call(
        functools.partial(_cmp_mlp_kernel, nhb=nhb),
        grid=(R, B),
        in_specs=[
            pl.BlockSpec((1, 1, nhb, half), lambda r, bb: (r, bb, 0, 0)),
            pl.BlockSpec((1, 2, half), lambda r, bb: (r // N_KV_GROUPS, 0, 0)),
            pl.BlockSpec((1, 2 * half, H), lambda r, bb: (r // N_KV_GROUPS, 0, 0)),
            pl.BlockSpec((1, H, dh), lambda r, bb: (r // N_KV_GROUPS, 0, 0)),
        ],
        out_specs=pl.BlockSpec((1, 1, nhb, dh), lambda r, bb: (r, bb, 0, 0)),
        out_shape=jax.ShapeDtypeStruct((R, B, nhb, dh), BF16),
        compiler_params=_cparams(("parallel", "parallel"), VMEM_LIMIT),
        name="cmp_mlp",
    )(x4, pos, w1, w2)


def _cmp_attn_kernel(q_ref, k_ref, v_ref, sl_ref, ovt_ref, qf_ref, o_ref, qa_ref, *, tq, nhb, n_sel, topk):
    i = pl.program_id(2)
    q0 = i * tq
    qpos = q0 + lax.broadcasted_iota(I32, (tq, 1), 0)
    n = lax.broadcasted_iota(I32, (1, nhb), 1)
    dist = qpos - (n * CMP_STRIDE + (CMP_BLOCK - 1))
    valid = (dist >= 0) & (n < nhb - 1)
    distf = dist.astype(F32)
    k = k_ref[0, 0]
    v = v_ref[0, 0]
    psum = jnp.zeros((tq, nhb), F32)
    for h in range(HEADS_PER_GROUP):
        s = _dot_nt(q_ref[h], k) - sl_ref[0, h:h + 1, 0:1] * distf
        s = jnp.where(valid, s, MASK_VALUE)
        m = jnp.max(s, axis=-1, keepdims=True)
        e = jnp.where(valid, jnp.exp(s - m), 0.0)
        l = jnp.sum(e, axis=-1, keepdims=True)
        p = e / jnp.maximum(l, 1e-30)
        o_ref[h] = _dot(p.astype(BF16), v).astype(o_ref.dtype)
        psum = psum + p

    ovt = ovt_ref[...]
    p_hi = psum.astype(BF16)
    p_lo = (psum - p_hi.astype(F32)).astype(BF16)
    imp_t = _dot_nt(ovt, p_hi) + _dot_nt(ovt, p_lo)

    blk = lax.broadcasted_iota(I32, (SEL_ROWS, tq), 0)
    cur = lax.shift_right_logical(q0 + lax.broadcasted_iota(I32, (1, tq), 1), 6)
    in_range = (blk <= cur) & (blk < n_sel)
    forced = in_range & ((blk == 0) | (blk == cur) | (blk == cur - 1))
    k_free = (topk - 1 - (cur >= 1).astype(I32) - (cur >= 2).astype(I32)).astype(F32)
    neg = jnp.float32(-jnp.inf)
    x = jnp.where(in_range & jnp.logical_not(forced), imp_t, neg)
    n_grp = (n_sel + 7) // 8
    xg = [x[8 * r:8 * r + 8, :] for r in range(n_grp)]
    rank_g = [jnp.zeros((8, tq), F32) for _ in range(n_grp)]
    sub = lax.broadcasted_iota(I32, (8, tq), 0)
    for c in range(n_sel):
        vc = jnp.broadcast_to(x[c:c + 1, :], (8, tq))
        for r in range(n_grp):
            if 8 * r > c:
                hit = jnp.where(vc >= xg[r], 1.0, 0.0)
            elif 8 * r + 7 <= c:
                hit = jnp.where(vc > xg[r], 1.0, 0.0)
            else:
                hit = jnp.where(sub > c - 8 * r, jnp.where(vc >= xg[r], 1.0, 0.0), jnp.where(vc > xg[r], 1.0, 0.0))
            rank_g[r] = rank_g[r] + hit
    rank = jnp.concatenate(rank_g + [jnp.zeros((SEL_ROWS - 8 * n_grp, tq), F32)] * (SEL_ROWS > 8 * n_grp), axis=0)
    selected = forced | ((x > neg) & (rank < k_free))
    bias_t = jnp.where(selected, 0.0, MASK_VALUE)
    bias = jnp.concatenate([bias_t, jnp.zeros((LANES - SEL_ROWS, tq), F32)], axis=0).T
    lane = lax.broadcasted_iota(I32, (tq, LANES), 1)
    for h in range(HEADS_PER_GROUP):
        qa_ref[h, :, 0:HEAD_DIM] = q_ref[h]
        qa_ref[h, :, HEAD_DIM:2 * HEAD_DIM] = jnp.where(lane < SEL_ROWS, bias, qf_ref[0, h:h + 1, :]).astype(BF16)


def _cmp_attn(q_t, kvcmp, slopes, ovt, qfeat, B, tq=1024):
    NH, T, dh = q_t.shape
    S = T // B
    nq = S // tq
    nhb = kvcmp.shape[2]
    n_sel = S // SEL_BLOCK
    assert n_sel <= SEL_ROWS
    topk = min(SEL_TOPK, n_sel)
    qmap = lambda bb, g, i: (g, bb * nq + i, 0)
    return pl.pallas_call(
        functools.partial(_cmp_attn_kernel, tq=tq, nhb=nhb, n_sel=n_sel, topk=topk),
        grid=(B, N_KV_GROUPS, nq),
        in_specs=[
            pl.BlockSpec((HEADS_PER_GROUP, tq, dh), qmap),
            pl.BlockSpec((1, 1, nhb, dh), lambda bb, g, i: (g, bb, 0, 0)),
            pl.BlockSpec((1, 1, nhb, dh), lambda bb, g, i: (N_KV_GROUPS + g, bb, 0, 0)),
            pl.BlockSpec((1, 8, LANES), lambda bb, g, i: (g, 0, 0)),
            pl.BlockSpec((SEL_ROWS, nhb), lambda bb, g, i: (0, 0)),
            pl.BlockSpec((1, 8, LANES), lambda bb, g, i: (g, 0, 0)),
        ],
        out_specs=[
            pl.BlockSpec((HEADS_PER_GROUP, tq, dh), qmap),
            pl.BlockSpec((HEADS_PER_GROUP, tq, 2 * dh), qmap),
        ],
        out_shape=[
            jax.ShapeDtypeStruct((NH, T, dh), BF16),
            jax.ShapeDtypeStruct((NH, T, 2 * dh), BF16),
        ],
        compiler_params=_cparams(("parallel", "parallel", "parallel")),
        name="cmp_attn",
    )(q_t, kvcmp, kvcmp, slopes, ovt, qfeat)


def _sel_attn_kernel(q_ref, k_ref, v_ref, o_ref, m_scr, acc_scr, s_buf, *, tq, tk):
    i = pl.program_id(2)
    q0 = i * tq
    nkt = k_ref.shape[1]
    m_scr[...] = jnp.full_like(m_scr, MASK_VALUE)
    acc_scr[...] = jnp.zeros_like(acc_scr)
    qpos = q0 + lax.broadcasted_iota(I32, (tq, 1), 0)

    def scores_to(t, slot):
        ka_t = k_ref[0, jnp.minimum(t, nkt - 1)]
        causal = (t * tk + lax.broadcasted_iota(I32, (1, tk), 1)) <= qpos
        for h in range(HEADS_PER_GROUP):
            s_buf[slot, h] = jnp.where(causal, _dot(q_ref[h], ka_t), MASK_VALUE)

    def absorb_from(t, slot):
        k0 = pl.multiple_of(jnp.minimum(t, nkt - 1) * tk, tk)
        va = v_ref[0, pl.ds(k0, tk), :]
        half = tq // 2
        for h in range(HEADS_PER_GROUP):
            m_prev = m_scr[h]
            m_new = jnp.maximum(m_prev, jnp.max(s_buf[slot, h], axis=-1, keepdims=True))
            alpha = jnp.exp(m_prev - m_new)
            p = jnp.concatenate([jnp.exp(s_buf[slot, h, 0:half, :] - m_new[0:half]).astype(BF16),
                                 jnp.exp(s_buf[slot, h, half:tq, :] - m_new[half:tq]).astype(BF16)], axis=0)
            acc_scr[h] = alpha * acc_scr[h] + _dot(p, va)
            m_scr[h] = m_new

    n_tiles = q0 // tk + 1

    def tile_pair(pr, carry):
        t = 2 * pr
        scores_to(t + 1, 1)
        absorb_from(t, 0)

        @pl.when(t + 1 < n_tiles)
        def _():
            scores_to(t + 2, 0)
            absorb_from(t + 1, 1)

        return carry

    scores_to(0, 0)
    lax.fori_loop(0, (n_tiles + 1) // 2, tile_pair, 0)
    for h in range(HEADS_PER_GROUP):
        o_ref[h] = (acc_scr[h, :, 0:HEAD_DIM] / acc_scr[h, :, HEAD_DIM:HEAD_DIM + 1]).astype(o_ref.dtype)


def _sel_attn(q_aug, k_aug_t, v_aug, B, tq=512, tk=SEL_KEY_TILE):
    NH, T, dq = q_aug.shape
    dh = dq // 2
    S = T // B
    assert tk % tq == 0 and S % tk == 0
    nq = S // tq
    nkt = S // tk
    qmap = lambda bb, g, i: (g, bb * nq + i, 0)
    return pl.pallas_call(
        functools.partial(_sel_attn_kernel, tq=tq, tk=tk),
        grid=(B, N_KV_GROUPS, nq),
        in_specs=[
            pl.BlockSpec((HEADS_PER_GROUP, tq, dq), qmap),
            pl.BlockSpec((1, nkt, dq, tk), lambda bb, g, i: (g, bb, 0, 0)),
            pl.BlockSpec((1, S, dq), lambda bb, g, i: (g, bb, 0)),
        ],
        out_specs=pl.BlockSpec((HEADS_PER_GROUP, tq, dh), qmap),
        out_shape=jax.ShapeDtypeStruct((NH, T, dh), BF16),
        scratch_shapes=[
            pltpu.VMEM((HEADS_PER_GROUP, tq, 1), F32),
            pltpu.VMEM((HEADS_PER_GROUP, tq, dq), F32),
            pltpu.VMEM((2, HEADS_PER_GROUP, tq, tk), F32),
        ],
        compiler_params=_cparams(("parallel", "parallel", "parallel")),
        name="nsa_sel_attn",
    )(q_aug, k_aug_t, v_aug)


def _win_attn_kernel(q_ref, k_ref, v_ref, o_ref, *, tq):
    i = pl.program_id(2)
    q0 = i * tq
    back = WINDOW // tq
    qpos = q0 + lax.broadcasted_iota(I32, (tq, 1), 0)
    tiles = []
    for d in range(back, -1, -1):
        kt = i - d
        ktc = jnp.maximum(kt, 0)
        kpos = kt * tq + lax.broadcasted_iota(I32, (1, tq), 1)
        if d == back:
            valid = ((qpos - kpos) < WINDOW) & (kt >= 0)
        elif d == 0:
            valid = kpos <= qpos
        else:
            valid = kt >= 0
        tiles.append((k_ref[0, ktc], v_ref[0, pl.ds(pl.multiple_of(ktc * tq, tq), tq), :], valid))
    for h in range(HEADS_PER_GROUP):
        q = q_ref[h]
        s_all = [jnp.where(valid, _dot(q, k_t), MASK_VALUE) for k_t, _, valid in tiles]
        m = functools.reduce(jnp.maximum, [jnp.max(s, axis=-1, keepdims=True) for s in s_all])
        acc = functools.reduce(lambda a, b: a + b,
                               [_dot(jnp.exp(s - m).astype(BF16), va) for s, (_, va, _) in zip(s_all, tiles)])
        o_ref[h] = (acc[:, 0:HEAD_DIM] / acc[:, HEAD_DIM:HEAD_DIM + 1]).astype(o_ref.dtype)


def _win_attn(q_aug, k_aug_t, v_aug, B, tq=256):
    NH, T, dq = q_aug.shape
    dh = dq // 2
    S = T // B
    assert WINDOW % tq == 0 and S % tq == 0
    nq = S // tq
    qmap = lambda bb, g, i: (g, bb * nq + i, 0)
    return pl.pallas_call(
        functools.partial(_win_attn_kernel, tq=tq),
        grid=(B, N_KV_GROUPS, nq),
        in_specs=[
            pl.BlockSpec((HEADS_PER_GROUP, tq, dq), qmap),
            pl.BlockSpec((1, nq, dq, tq), lambda bb, g, i: (g, bb, 0, 0)),
            pl.BlockSpec((1, S, dq), lambda bb, g, i: (g, bb, 0)),
        ],
        out_specs=pl.BlockSpec((HEADS_PER_GROUP, tq, dh), qmap),
        out_shape=jax.ShapeDtypeStruct((NH, T, dh), BF16),
        compiler_params=_cparams(("parallel", "parallel", "parallel")),
        name="nsa_win_attn",
    )(q_aug, k_aug_t, v_aug)


def _attn_out_kernel(oc_ref, os_ref, ow_ref, gt_ref, ex_ref, x_ref, w_ref, g_ref, b_ref, of_ref, op_ref, lhs):
    gt = gt_ref[...]
    gt_hi = gt.astype(BF16)
    gt_lo = (gt - gt_hi.astype(F32)).astype(BF16)
    gt2 = jnp.concatenate([gt_hi, gt_lo], axis=-1)
    ge = [_dot(gt2, ex_ref[br]) for br in range(N_BRANCHES)]
    for hd in range(N_HEADS):
        cols = slice(hd * HEAD_DIM, (hd + 1) * HEAD_DIM)
        o = (ge[0][:, cols] * oc_ref[hd].astype(F32) + ge[1][:, cols] * os_ref[hd].astype(F32)
             + ge[2][:, cols] * ow_ref[hd].astype(F32))
        lhs[:, cols] = o.astype(BF16)
    mix = _dot(lhs[...], w_ref[...])
    h = _layer_norm(DEEPNORM_ALPHA * x_ref[...] + mix, g_ref[...], b_ref[...])
    of_ref[...] = h
    op_ref[...] = _pack_halves(h)


def _attn_out_res_ln(o_cmp, o_sel, o_win, gates, x, w, g, b, tm=256):
    T, D = x.shape
    NH = o_cmp.shape[0]
    hmap = lambda i: (0, i, 0)
    row = lambda i: (i, 0)
    fix = lambda i: (0, 0)
    expand = np.zeros((N_BRANCHES, 2 * LANES, NH * HEAD_DIM), np.float32)
    for hd in range(NH):
        for br in range(N_BRANCHES):
            for piece in range(2):
                expand[br, piece * LANES + N_BRANCHES * hd + br, hd * HEAD_DIM:(hd + 1) * HEAD_DIM] = 1.0
    expand = jnp.asarray(expand, dtype=BF16)
    return pl.pallas_call(
        _attn_out_kernel,
        grid=(T // tm,),
        in_specs=[
            pl.BlockSpec((NH, tm, HEAD_DIM), hmap), pl.BlockSpec((NH, tm, HEAD_DIM), hmap),
            pl.BlockSpec((NH, tm, HEAD_DIM), hmap),
            pl.BlockSpec((tm, LANES), row),
            pl.BlockSpec((N_BRANCHES, 2 * LANES, NH * HEAD_DIM), lambda i: (0, 0, 0)),
            pl.BlockSpec((tm, D), row),
            pl.BlockSpec((D, D), fix),
            pl.BlockSpec((1, D), fix), pl.BlockSpec((1, D), fix),
        ],
        out_specs=[pl.BlockSpec((tm, D), row), pl.BlockSpec((tm, D // 2), row)],
        out_shape=[jax.ShapeDtypeStruct((T, D), F32), jax.ShapeDtypeStruct((T, D // 2), U32)],
        scratch_shapes=[pltpu.VMEM((tm, D), BF16)],
        compiler_params=_cparams(("parallel",), VMEM_LIMIT),
        name="attn_out_res_ln",
    )(o_cmp, o_sel, o_win, gates, expand, x, w, g, b)


def _bf16_pieces(x, n=3):
    out, rest = [], np.asarray(x, np.float32)
    for _ in range(n):
        piece = rest.astype(BF16).astype(np.float32)
        out.append(piece)
        rest = (rest - piece).astype(np.float32)
    return out


def _nsa_tables(S):
    nhb = S // CMP_STRIDE
    n_sel = S // SEL_BLOCK
    heads = np.arange(1, N_HEADS + 1, dtype=np.float32)
    slope = (2.0 ** (-8.0 * heads / N_HEADS)).astype(np.float32)
    slopes = np.zeros((N_KV_GROUPS, 8, LANES), np.float32)
    slopes[:, :HEADS_PER_GROUP, :] = slope.reshape(N_KV_GROUPS, HEADS_PER_GROUP, 1)
    c0 = np.arange(nhb)[None, :] * CMP_STRIDE
    j0 = np.arange(SEL_ROWS)[:, None] * SEL_BLOCK
    ovt = np.maximum(np.minimum(c0 + CMP_BLOCK, j0 + SEL_BLOCK) - np.maximum(c0, j0), 0).astype(np.float32) / CMP_BLOCK
    ovt[:, nhb - 1] = 0.0
    ovt[n_sel:, :] = 0.0
    qfeat = np.zeros((N_KV_GROUPS, 8, LANES), np.float32)
    pieces = _bf16_pieces(slope)
    for t, piece in enumerate(pieces):
        qfeat[:, :HEADS_PER_GROUP, SEL_ROWS + t] = piece.reshape(N_KV_GROUPS, HEADS_PER_GROUP)
        qfeat[:, :HEADS_PER_GROUP, SEL_ROWS + 3 + t] = piece.reshape(N_KV_GROUPS, HEADS_PER_GROUP)
    kpos = np.arange(S)
    kx = np.zeros((S, LANES), np.float32)
    kx[kpos, kpos // SEL_BLOCK] = 1.0
    kx[:, SEL_ROWS:SEL_ROWS + 3] = (SEL_BLOCK * (kpos // SEL_BLOCK))[:, None]
    kx[:, SEL_ROWS + 3:SEL_ROWS + 6] = (kpos % SEL_BLOCK)[:, None]
    return jnp.asarray(slopes), jnp.asarray(ovt, dtype=BF16), jnp.asarray(qfeat), jnp.asarray(kx, dtype=BF16)


def _nsa_layer_res_ln(hf, hb, B, kv_w, cmp_pos, cmp_w1, cmp_w2, w_qg, w_o, g, b):
    T, D = hf.shape
    S = T // B
    hd = N_HEADS * HEAD_DIM
    gsz = N_KV_GROUPS * HEAD_DIM
    slopes, ovt, qfeat, kx = _nsa_tables(S)

    kvw = kv_w.astype(BF16)
    kvc = _proj_heads(hb, kvw[:, :2 * gsz], F32)
    kv_t = _proj_heads(hb, kvw[:, 2 * gsz:], BF16)
    wq = w_qg[:, :hd].astype(BF16)
    wgt = jnp.zeros((D, LANES), F32).at[:, :N_BRANCHES * N_HEADS].set(w_qg[:, hd:]).astype(BF16)
    q_t = _proj_heads(hb, wq, BF16, scale=HEAD_DIM ** -0.5)
    gates = _gates(hb, wgt)

    pos = cmp_pos.reshape(2, 2, CMP_STRIDE * HEAD_DIM)
    kvcmp = _cmp_mlp(kvc, pos, cmp_w1.astype(BF16), cmp_w2.astype(BF16), B)
    o_cmp, q_aug = _cmp_attn(q_t, kvcmp, slopes, ovt, qfeat, B)
    kx_all = jnp.broadcast_to(jnp.tile(kx, (B, 1))[None], (N_KV_GROUPS, T, LANES))
    k_aug = jnp.concatenate([kv_t[:N_KV_GROUPS], kx_all], axis=-1)
    k_aug_t = k_aug.reshape(N_KV_GROUPS, T // SEL_KEY_TILE, SEL_KEY_TILE, 2 * HEAD_DIM).transpose(0, 1, 3, 2)
    ones_col = jnp.zeros((1, 1, LANES), BF16).at[0, 0, 0].set(1.0)
    v_aug = jnp.concatenate([kv_t[N_KV_GROUPS:2 * N_KV_GROUPS],
                             jnp.broadcast_to(ones_col, (N_KV_GROUPS, T, LANES))], axis=-1)
    o_sel = _sel_attn(q_aug, k_aug_t, v_aug, B)
    win_tq = 256
    lane_id = jnp.arange(LANES)[None, None, :]
    kw_aug = jnp.concatenate([kv_t[2 * N_KV_GROUPS:3 * N_KV_GROUPS],
                              jnp.where(lane_id < SEL_ROWS, jnp.zeros((), BF16), kx_all)], axis=-1)
    kw_aug_t = kw_aug.reshape(N_KV_GROUPS, T // win_tq, win_tq, 2 * HEAD_DIM).transpose(0, 1, 3, 2)
    vw_aug = jnp.concatenate([kv_t[3 * N_KV_GROUPS:],
                              jnp.broadcast_to(ones_col, (N_KV_GROUPS, T, LANES))], axis=-1)
    o_win = _win_attn(q_aug, kw_aug_t, vw_aug, B, tq=win_tq)
    return _attn_out_res_ln(o_cmp, o_sel, o_win, gates, hf, w_o.astype(BF16), g, b)


def kernel(x, conv_w_pw1, conv_b_pw1, conv_w_dw, conv_b_dw, conv_ln_g, conv_ln_b, conv_w_pw2, kv_w, cmp_pos, cmp_w1, cmp_w2, nsa_w_qg, nsa_w_o, moe_wg, moe_bg, moe_we, moe_be, moe_w_gu, moe_w_down, ln_g, ln_b):
    B, S, D = x.shape
    T = B * S
    hf = x.reshape(T, D)
    hb = hf.astype(BF16)
    r = lambda v: v.reshape(1, -1)
    for l in range(DEPTH):
        if l < N_A_LAYERS:
            glu = _pw1_glu(hb, conv_w_pw1[l].astype(BF16), r(conv_b_pw1[l]))
            conv = _dwconv(glu.reshape(B, S, D), conv_w_dw[l], r(conv_b_dw[l])).reshape(T, D)
            hf, hp = _pw2_res_ln(conv, hf, conv_w_pw2[l].astype(BF16), r(conv_ln_g[l]), r(conv_ln_b[l]),
                                 r(ln_g[l, 0]), r(ln_b[l, 0]))
        else:
            j = l - N_A_LAYERS
            hf, hp = _nsa_layer_res_ln(hf, hb, B, kv_w, cmp_pos, cmp_w1, cmp_w2, nsa_w_qg[j], nsa_w_o[j],
                                       r(ln_g[l, 0]), r(ln_b[l, 0]))
        hf, hb = _hier_moe_res_ln(hf, hp, moe_wg[l], moe_bg[l], moe_we[l], moe_be[l], moe_w_gu, moe_w_down, l,
                                  r(ln_g[l, 1]), r(ln_b[l, 1]))
    return hf.reshape(B, S, D)
```
